```python
import jax, jax.numpy as jnp
from jax import lax
import numpy as np

D_MODEL = 1024
BATCH = 32
SEQ = 256
DEPTH = 2
DEC_BATCH = 8
DEC_SEQ = 2048
PAST_LEN = 512

GRID_W = 64
N_MIXERS = 2
N_ATTN_LAYERS = (DEPTH + 1) // 2
N_LRU_LAYERS = DEPTH // 2
N_HEADS = 16
HEAD_DIM = D_MODEL // N_HEADS
WIN_ROWS_MAX = 8
WIN_COLS = 16
D_RNN = D_MODEL
N_LRU_BLOCKS = 4
LRU_BLOCK = D_RNN // N_LRU_BLOCKS
CONV_WIDTH = 4
CONV_LEFT = 2
LRU_C = 8.0
N_EXPERTS = 16
N_GROUPS = 4
EXPERTS_PER_GROUP = N_EXPERTS // N_GROUPS
TOP_K = 2
D_FF_EXPERT = 1024

EPS = 1e-6
NEG_INF = -1e30

kernel_name = "hybrid_natten_rglru_moe_diffusion_step"


def rms_norm(x, g):
    xf = x.astype(jnp.float32)
    y = xf * lax.rsqrt(jnp.mean(xf * xf, axis=-1, keepdims=True) + EPS)
    return y.astype(x.dtype) * g


def ada_modulation(cond, w, b):
    m = jax.nn.silu(cond) @ w + b
    return [t[:, None, :] for t in jnp.split(m, 6, axis=-1)]


def attn_qkv(h, w_in, q_gain, k_gain):
    B, T, _ = h.shape
    qkv = (h @ w_in).reshape(B, T, 3, N_HEADS, HEAD_DIM)
    q = rms_norm(qkv[:, :, 0], q_gain)
    k = rms_norm(qkv[:, :, 1], k_gain)
    return q, k, qkv[:, :, 2]


def attn_context(h, w_in, q_gain, k_gain, w_out):
    B, T, _ = h.shape
    q, k, v = attn_qkv(h, w_in, q_gain, k_gain)
    s = jnp.einsum('bqhd,bkhd->bhqk', q, k).astype(jnp.float32) * HEAD_DIM ** -0.5
    p = jax.nn.softmax(s, axis=-1).astype(v.dtype)
    o = jnp.einsum('bhqk,bkhd->bqhd', p, v)
    return o.reshape(B, T, D_MODEL) @ w_out, k, v


def neighbourhood_attention(h, k_ctx, v_ctx, w_in, q_gain, k_gain, rpb, w_out):
    B, N, _ = h.shape
    rows = N // GRID_W
    kh = min(WIN_ROWS_MAX, rows)
    scale = HEAD_DIM ** -0.5
    q, k, v = attn_qkv(h, w_in, q_gain, k_gain)
    qg = q.reshape(B, rows, GRID_W, N_HEADS, HEAD_DIM)
    kg = k.reshape(B, rows, GRID_W, N_HEADS, HEAD_DIM)
    vg = v.reshape(B, rows, GRID_W, N_HEADS, HEAD_DIM)
    r = jnp.arange(rows)
    row_idx = jnp.clip(r - kh // 2, 0, rows - kh)[:, None] + jnp.arange(kh)[None, :]
    k_rows = kg[:, row_idx]
    v_rows = vg[:, row_idx]
    cw = jnp.arange(GRID_W)
    col_start = jnp.clip(cw - WIN_COLS // 2, 0, GRID_W - WIN_COLS)
    col_ok = (cw[None, :] >= col_start[:, None]) & (cw[None, :] < col_start[:, None] + WIN_COLS)
    dr = row_idx - r[:, None] + (WIN_ROWS_MAX - 1)
    dc = jnp.clip(cw[None, :] - cw[:, None], -(WIN_COLS - 1), WIN_COLS - 1) + (WIN_COLS - 1)
    bias = rpb.astype(jnp.float32)[:, dr[:, None, :, None], dc[None, :, None, :]]
    bias = jnp.transpose(bias, (1, 2, 0, 3, 4))
    bias = jnp.where(col_ok[None, :, None, None, :], bias, NEG_INF)
    s_loc = jnp.einsum('brqhd,brkwhd->brqhkw', qg, k_rows).astype(jnp.float32) * scale + bias
    s_ctx = jnp.einsum('brqhd,bchd->brqhc', qg, k_ctx).astype(jnp.float32) * scale
    n_loc = kh * GRID_W
    s = jnp.concatenate([s_loc.reshape(B, rows, GRID_W, N_HEADS, n_loc), s_ctx], axis=-1)
    p = jax.nn.softmax(s, axis=-1).astype(v.dtype)
    p_loc = p[..., :n_loc].reshape(B, rows, GRID_W, N_HEADS, kh, GRID_W)
    o = (jnp.einsum('brqhkw,brkwhd->brqhd', p_loc, v_rows)
         + jnp.einsum('brqhc,bchd->brqhd', p[..., n_loc:], v_ctx))
    return o.reshape(B, N, D_MODEL) @ w_out


def block_diag(x, w, b):
    B, T, _ = x.shape
    y = jnp.einsum('btnk,nkj->btnj', x.reshape(B, T, N_LRU_BLOCKS, LRU_BLOCK), w)
    return y.reshape(B, T, D_RNN) + b


def centred_depthwise_conv(x, w, b):
    T = x.shape[1]
    xp = jnp.pad(x, ((0, 0), (CONV_LEFT, CONV_WIDTH - 1 - CONV_LEFT), (0, 0)))
    return sum(w[j] * xp[:, j:j + T] for j in range(CONV_WIDTH)) + b


def _affine_combine(earlier, later):
    a1, b1 = earlier
    a2, b2 = later
    return a1 * a2, a2 * b1 + b2


def rg_lru_scan(xc, h0, wa, ba, wx, bx, lam, reverse):
    r = jax.nn.sigmoid(block_diag(xc, wa, ba).astype(jnp.float32))
    i = jax.nn.sigmoid(block_diag(xc, wx, bx).astype(jnp.float32))
    log_a = -LRU_C * r * jax.nn.softplus(-lam.astype(jnp.float32))
    a = jnp.exp(log_a)
    b = jnp.sqrt(-jnp.expm1(2.0 * log_a)) * (i * xc.astype(jnp.float32))
    edge = -1 if reverse else 0
    b = b.at[:, edge].add(a[:, edge] * h0.astype(jnp.float32))
    _, h = lax.associative_scan(_affine_combine, (a, b), axis=1, reverse=reverse)
    return h


def rg_lru_mixer(h, h0, w_in, conv_w, conv_b, ga_w, ga_b, gx_w, gx_b, lam, w_out):
    gate_br, x_br = jnp.split(h @ w_in, 2, axis=-1)
    xc = centred_depthwise_conv(x_br, conv_w, conv_b)
    hf = rg_lru_scan(xc, h0[:, 0], ga_w[0], ga_b[0], gx_w[0], gx_b[0], lam[0], False)
    hb = rg_lru_scan(xc, h0[:, 1], ga_w[1], ga_b[1], gx_w[1], gx_b[1], lam[1], True)
    y = jax.nn.gelu(gate_br) * (hf + hb).astype(h.dtype)
    final = jnp.stack([hf[:, -1], hb[:, 0]], axis=1).astype(h.dtype)
    return y @ w_out, final


def grouped_moe(h, router_w, router_b, w_gate, w_up, w_down):
    B, T, D = h.shape
    xt = h.reshape(B * T, D)
    scores = jax.nn.sigmoid((xt @ router_w).astype(jnp.float32))
    sel = scores + router_b.astype(jnp.float32)
    grp = sel.reshape(-1, N_GROUPS, EXPERTS_PER_GROUP)
    group_score = jnp.sum(lax.top_k(grp, TOP_K)[0], axis=-1)
    best = jnp.argmax(group_score, axis=-1)
    in_group = (jnp.arange(N_EXPERTS) // EXPERTS_PER_GROUP)[None, :] == best[:, None]
    _, eidx = lax.top_k(jnp.where(in_group, sel, NEG_INF), TOP_K)
    wsel = jnp.take_along_axis(scores, eidx, axis=-1)
    wsel = wsel / jnp.sum(wsel, axis=-1, keepdims=True)
    combine = jnp.sum(jax.nn.one_hot(eidx, N_EXPERTS, dtype=jnp.float32) * wsel[..., None], axis=1)
    combine = combine.astype(h.dtype)
    out = jnp.zeros_like(xt)
    for e in range(N_EXPERTS):
        he = jax.nn.silu(xt @ w_gate[e]) * (xt @ w_up[e])
        out = out + combine[:, e:e + 1] * (he @ w_down[e])
    return out.reshape(B, T, D)


def setup_inputs(seed: int = 0) -> dict:
    key = jax.random.key(seed)
    ks = iter(jax.random.split(key, 40))
    f32 = jnp.float32

    def nrm(shape, s):
        return jax.random.normal(next(ks), shape, f32) * s

    a0 = jax.random.uniform(next(ks), (N_LRU_LAYERS, 2, D_RNN), f32, 0.9, 0.999)
    p = a0 ** (1.0 / LRU_C)
    lam = jnp.log(p) - jnp.log1p(-p)
    return {
        "x_prompt": nrm((BATCH, SEQ, D_MODEL), 1.0),
        "x_sample": nrm((DEC_BATCH, DEC_SEQ, D_MODEL), 1.0),
        "c": nrm((DEC_BATCH, D_MODEL), 1.0),
        "cache_k": nrm((DEC_BATCH, N_ATTN_LAYERS, PAST_LEN, N_HEADS, HEAD_DIM), 1.0),
        "cache_v": nrm((DEC_BATCH, N_ATTN_LAYERS, PAST_LEN, N_HEADS, HEAD_DIM), 1.0),
        "state_lru": nrm((DEC_BATCH, N_LRU_LAYERS, 2, D_RNN), 0.5),
        "c_ctx": nrm((D_MODEL,), 1.0),
        "ada_w": nrm((DEPTH, D_MODEL, 6 * D_MODEL), 0.5 * D_MODEL ** -0.5),
        "ada_b": nrm((DEPTH, 6 * D_MODEL), 0.02),
        "norm_mix": 1.0 + nrm((DEPTH, D_MODEL), 0.02),
        "norm_ffn": 1.0 + nrm((DEPTH, D_MODEL), 0.02),
        "attn_w_in": nrm((N_ATTN_LAYERS, D_MODEL, 3 * D_MODEL), D_MODEL ** -0.5),
        "attn_q_gain": 1.0 + nrm((N_ATTN_LAYERS, HEAD_DIM), 0.02),
        "attn_k_gain": 1.0 + nrm((N_ATTN_LAYERS, HEAD_DIM), 0.02),
        "attn_rpb": nrm((N_ATTN_LAYERS, N_HEADS, 2 * WIN_ROWS_MAX - 1, 2 * WIN_COLS - 1), 0.1),
        "attn_w_out": nrm((N_ATTN_LAYERS, D_MODEL, D_MODEL), D_MODEL ** -0.5),
        "lru_w_in": nrm((N_LRU_LAYERS, D_MODEL, 2 * D_RNN), D_MODEL ** -0.5),
        "lru_conv_w": nrm((N_LRU_LAYERS, CONV_WIDTH, D_RNN), CONV_WIDTH ** -0.5),
        "lru_conv_b": nrm((N_LRU_LAYERS, D_RNN), 0.02),
        "lru_gate_a_w": nrm((N_LRU_LAYERS, 2, N_LRU_BLOCKS, LRU_BLOCK, LRU_BLOCK), LRU_BLOCK ** -0.5),
        "lru_gate_a_b": nrm((N_LRU_LAYERS, 2, D_RNN), 0.02),
        "lru_gate_x_w": nrm((N_LRU_LAYERS, 2, N_LRU_BLOCKS, LRU_BLOCK, LRU_BLOCK), LRU_BLOCK ** -0.5),
        "lru_gate_x_b": nrm((N_LRU_LAYERS, 2, D_RNN), 0.02),
        "lru_lambda": lam,
        "lru_w_out": nrm((N_LRU_LAYERS, D_RNN, D_MODEL), D_RNN ** -0.5),
        "router_w": nrm((D_MODEL, N_EXPERTS), D_MODEL ** -0.5),
        "router_b": nrm((N_EXPERTS,), 0.01),
        "moe_w_gate": nrm((DEPTH, N_EXPERTS, D_MODEL, D_FF_EXPERT), D_MODEL ** -0.5),
        "moe_w_up": nrm((DEPTH, N_EXPERTS, D_MODEL, D_FF_EXPERT), D_MODEL ** -0.5),
        "moe_w_down": nrm((DEPTH, N_EXPERTS, D_FF_EXPERT, D_MODEL), D_FF_EXPERT ** -0.5),
    }


def reference(x_prompt, x_sample, c, cache_k, cache_v, state_lru, c_ctx, ada_w, ada_b,
              norm_mix, norm_ffn, attn_w_in, attn_q_gain, attn_k_gain, attn_rpb, attn_w_out,
              lru_w_in, lru_conv_w, lru_conv_b, lru_gate_a_w, lru_gate_a_b, lru_gate_x_w,
              lru_gate_x_b, lru_lambda, lru_w_out, router_w, router_b,
              moe_w_gate, moe_w_up, moe_w_down):
    y_p = x_prompt
    ctx_cond = c_ctx[None, :]
    new_k, new_v, new_h = [], [], []
    for i in range(DEPTH):
        sh1, sc1, gt1, sh2, sc2, gt2 = ada_modulation(ctx_cond, ada_w[i], ada_b[i])
        hn = rms_norm(y_p, norm_mix[i]) * (1 + sc1) + sh1
        j = i // N_MIXERS
        if i % N_MIXERS == 0:
            out, k, v = attn_context(hn, attn_w_in[j], attn_q_gain[j], attn_k_gain[j], attn_w_out[j])
            new_k.append(k)
            new_v.append(v)
        else:
            h0 = jnp.zeros((y_p.shape[0], 2, D_RNN), y_p.dtype)
            out, hfin = rg_lru_mixer(hn, h0, lru_w_in[j], lru_conv_w[j], lru_conv_b[j],
                                     lru_gate_a_w[j], lru_gate_a_b[j], lru_gate_x_w[j],
                                     lru_gate_x_b[j], lru_lambda[j], lru_w_out[j])
            new_h.append(hfin)
        y_p = y_p + gt1 * out
        hn = rms_norm(y_p, norm_ffn[i]) * (1 + sc2) + sh2
        y_p = y_p + gt2 * grouped_moe(hn, router_w, router_b, moe_w_gate[i], moe_w_up[i], moe_w_down[i])

    y_s = x_sample
    for i in range(DEPTH):
        sh1, sc1, gt1, sh2, sc2, gt2 = ada_modulation(c, ada_w[i], ada_b[i])
        hn = rms_norm(y_s, norm_mix[i]) * (1 + sc1) + sh1
        j = i // N_MIXERS
        if i % N_MIXERS == 0:
            out = neighbourhood_attention(hn, cache_k[:, j], cache_v[:, j], attn_w_in[j],
                                          attn_q_gain[j], attn_k_gain[j], attn_rpb[j], attn_w_out[j])
        else:
            out, _ = rg_lru_mixer(hn, state_lru[:, j], lru_w_in[j], lru_conv_w[j], lru_conv_b[j],
                                  lru_gate_a_w[j], lru_gate_a_b[j], lru_gate_x_w[j],
                                  lru_gate_x_b[j], lru_lambda[j], lru_w_out[j])
        y_s = y_s + gt1 * out
        hn = rms_norm(y_s, norm_ffn[i]) * (1 + sc2) + sh2
        y_s = y_s + gt2 * grouped_moe(hn, router_w, router_b, moe_w_gate[i], moe_w_up[i], moe_w_down[i])

    return (y_p, y_s, jnp.stack(new_k, axis=1), jnp.stack(new_v, axis=1), jnp.stack(new_h, axis=1))
```

```python
import functools

import numpy as np
import jax
import jax.numpy as jnp
from jax import lax
from jax.experimental import pallas as pl
from jax.experimental.pallas import tpu as pltpu

F32 = jnp.float32
BF16 = jnp.bfloat16
I32 = jnp.int32

EPS = 1e-6
NEG_INF = -1e30
GRID_W = 64
WIN_ROWS = 8
WIN_COLS = 16
N_GROUPS = 4
LRU_C = 8.0
CONV_LEFT = 2
N_LRU_BLOCKS = 4

LANES = 128
TOKEN_TILE = 256
FFN_TILE = 256
Q_ROWS = 4
SCAN_ROWS = 2048
N_BUCKET_SLOTS = 32
PAIR_LO = (0, 0, 0, 1, 1, 2)
PAIR_HI = (1, 2, 3, 2, 3, 3)
VMEM_LIMIT = 48 * 1024 * 1024


def _cparams(sem, vmem=VMEM_LIMIT):
    return pltpu.CompilerParams(dimension_semantics=sem, vmem_limit_bytes=vmem)


def _norm_mod(x, g, sc, sh):
    ms = jnp.mean(x * x, axis=-1, keepdims=True)
    return (x * lax.rsqrt(ms + EPS)) * g * (1.0 + sc) + sh


def _expm1(x):
    u = jnp.exp(x)
    um1 = u - 1.0
    return jnp.where(um1 == 0.0, x, jnp.where(um1 == -1.0, -1.0, um1 * x / jnp.log(u)))


def _dot(a, b):
    return jnp.dot(a, b, preferred_element_type=F32)


def _dot_nt(a, b):
    return lax.dot_general(a, b, (((1,), (1,)), ((), ())), preferred_element_type=F32)


def _ada_kernel(cond_ref, w_ref, b_ref, o_ref):
    c = cond_ref[...]
    s = (c * jax.nn.sigmoid(c)).astype(BF16)
    o_ref[0] = _dot(s, w_ref[0].astype(BF16)) + b_ref[0]


def _ada_modulation(cond16, ada_w, ada_b):
    depth, d, d6 = ada_w.shape
    tn = d6 // 4
    return pl.pallas_call(
        _ada_kernel,
        grid=(depth, d6 // tn),
        in_specs=[
            pl.BlockSpec((16, d), lambda l, j: (0, 0)),
            pl.BlockSpec((1, d, tn), lambda l, j: (l, 0, j)),
            pl.BlockSpec((1, 1, tn), lambda l, j: (l, 0, j)),
        ],
        out_specs=pl.BlockSpec((1, 16, tn), lambda l, j: (l, 0, j)),
        out_shape=jax.ShapeDtypeStruct((depth, 16, d6), F32),
        compiler_params=_cparams(("arbitrary", "arbitrary")),
    )(cond16, ada_w, ada_b.reshape(depth, 1, d6))


def _mod_spec(layer, chunk, d, row_fn):
    return pl.BlockSpec((1, 1, 1, d), lambda *ids: (layer, row_fn(*ids), 0, chunk))


def _qkv_kernel(x_ref, sh_ref, sc_ref, g_ref, w_ref, qg_ref, kg_ref, q_ref, k_ref, v_ref, *kv_refs,
                d, head_dim):
    hn = _norm_mod(x_ref[...], g_ref[...], sc_ref[0, 0], sh_ref[0, 0]).astype(BF16)
    tm = hn.shape[0]
    lo = lax.broadcasted_iota(I32, (tm, LANES), 1) < head_dim
    heads_per_chunk = 256 // head_dim
    outs = (q_ref, k_ref, v_ref)
    gains = (qg_ref, kg_ref)
    for j in range(3 * d // 256):
        acc = _dot(hn, w_ref[:, 256 * j:256 * (j + 1)])
        part = (256 * j) // d
        col = (256 * j) % d
        for half in range(2):
            a = acc[:, LANES * half:LANES * (half + 1)]
            if part < 2:
                sq = a * a
                s_lo = jnp.sum(jnp.where(lo, sq, 0.0), axis=-1, keepdims=True)
                s_hi = jnp.sum(jnp.where(lo, 0.0, sq), axis=-1, keepdims=True)
                r = jnp.where(lo, lax.rsqrt(s_lo / head_dim + EPS), lax.rsqrt(s_hi / head_dim + EPS))
                a = (a * r) * gains[part][...]
            h0 = col // head_dim + half * (heads_per_chunk // 2)
            outs[part][h0] = a[:, :head_dim].astype(BF16)
            outs[part][h0 + 1] = a[:, head_dim:].astype(BF16)
            if part >= 1 and kv_refs:
                kv_refs[part - 1][:, col + LANES * half:col + LANES * (half + 1)] = a


def _qkv_proj(x2d, mod4, row_fn, g, w_bf16, q_gain, k_gain, n_heads, with_kv):
    n, d = x2d.shape
    head_dim = d // n_heads
    assert head_dim * 2 == LANES
    tm = TOKEN_TILE
    hspec = pl.BlockSpec((n_heads, tm, head_dim), lambda i: (0, i, 0))
    hshape = jax.ShapeDtypeStruct((n_heads, n, head_dim), BF16)
    out_specs = [hspec, hspec, hspec]
    out_shape = [hshape, hshape, hshape]
    if with_kv:
        out_specs += [pl.BlockSpec((tm, d), lambda i: (i, 0))] * 2
        out_shape += [jax.ShapeDtypeStruct((n, d), F32)] * 2
    return pl.pallas_call(
        functools.partial(_qkv_kernel, d=d, head_dim=head_dim),
        grid=(n // tm,),
        in_specs=[
            pl.BlockSpec((tm, d), lambda i: (i, 0)),
            _mod_spec(0, 0, d, row_fn),
            _mod_spec(0, 1, d, row_fn),
            pl.BlockSpec((1, d), lambda i: (0, 0)),
            pl.BlockSpec((d, 3 * d), lambda i: (0, 0)),
            pl.BlockSpec((1, LANES), lambda i: (0, 0)),
            pl.BlockSpec((1, LANES), lambda i: (0, 0)),
        ],
        out_specs=out_specs,
        out_shape=out_shape,
        compiler_params=_cparams(("arbitrary",)),
    )(x2d, mod4, mod4, g, w_bf16, jnp.tile(q_gain, 2)[None, :], jnp.tile(k_gain, 2)[None, :])


def _softmax_rows(parts):
    m = parts[0].max(axis=-1, keepdims=True)
    for s in parts[1:]:
        m = jnp.maximum(m, s.max(axis=-1, keepdims=True))
    es = [jnp.exp(s - m) for s in parts]
    l = es[0].sum(axis=-1, keepdims=True)
    for e in es[1:]:
        l = l + e.sum(axis=-1, keepdims=True)
    return [(e / l).astype(BF16) for e in es]


def _ctx_attn_kernel(q_ref, k_ref, v_ref, o_ref, *, scale, head_dim):
    for hh in range(2):
        s = _dot_nt(q_ref[hh], k_ref[hh]) * scale
        (p,) = _softmax_rows([s])
        o_ref[:, head_dim * hh:head_dim * (hh + 1)] = _dot(p, v_ref[hh]).astype(BF16)


def _ctx_attention(q, k, v, batch, seq):
    n_heads, n, head_dim = q.shape
    spec = pl.BlockSpec((2, seq, head_dim), lambda hp, b: (hp, b, 0))
    return pl.pallas_call(
        functools.partial(_ctx_attn_kernel, scale=head_dim ** -0.5, head_dim=head_dim),
        grid=(n_heads // 2, batch),
        in_specs=[spec, spec, spec],
        out_specs=pl.BlockSpec((seq, LANES), lambda hp, b: (b, hp)),
        out_shape=jax.ShapeDtypeStruct((n, n_heads * head_dim), BF16),
        compiler_params=_cparams(("arbitrary", "arbitrary")),
    )(q, k, v)


def _nbr_geometry(rows):
    n_key_rows = Q_ROWS + WIN_ROWS
    assert rows % Q_ROWS == 0 and rows >= n_key_rows
    nblk = rows // Q_ROWS
    kstart = np.clip(np.arange(nblk) * Q_ROWS - WIN_ROWS // 2, 0, rows - n_key_rows)
    start = np.clip(np.arange(rows) - WIN_ROWS // 2, 0, rows - WIN_ROWS)
    tables, variant = [], []
    for blk in range(nblk):
        r = blk * Q_ROWS + np.arange(Q_ROWS)
        kr = kstart[blk] + np.arange(n_key_rows)
        valid = (kr[None, :] >= start[r][:, None]) & (kr[None, :] < start[r][:, None] + WIN_ROWS)
        dr = np.clip(kr[None, :] - r[:, None] + WIN_ROWS - 1, 0, 2 * WIN_ROWS - 2)
        key = (valid.tobytes(), dr.tobytes())
        keys = [t[0] for t in tables]
        if key not in keys:
            tables.append((key, valid, dr))
        variant.append([t[0] for t in tables].index(key))
    return nblk, n_key_rows, kstart, np.array(variant), [(t[1], t[2]) for t in tables]


def _nbr_bias(rpb, tables):
    cw = np.arange(GRID_W)
    col_start = np.clip(cw - WIN_COLS // 2, 0, GRID_W - WIN_COLS)
    col_ok = (cw[None, :] >= col_start[:, None]) & (cw[None, :] < col_start[:, None] + WIN_COLS)
    dc = np.clip(cw[None, :] - cw[:, None], -(WIN_COLS - 1), WIN_COLS - 1) + (WIN_COLS - 1)
    out = []
    for valid, dr in tables:
        qr, kr = valid.shape
        ok = valid[:, None, :, None] & col_ok[None, :, None, :]
        dr_i = np.broadcast_to(dr[:, None, :, None], ok.shape).reshape(qr * GRID_W, kr * GRID_W)
        dc_i = np.broadcast_to(dc[None, :, None, :], ok.shape).reshape(qr * GRID_W, kr * GRID_W)
        b = rpb.astype(F32)[:, dr_i, dc_i]
        out.append(jnp.where(ok.reshape(qr * GRID_W, kr * GRID_W)[None], b, NEG_INF))
    return jnp.stack(out, axis=0)


def _nbr_attn_kernel(kstart_ref, q_ref, k_ref, v_ref, ck_ref, cv_ref, bias_ref, o_ref, *,
                     scale, head_dim, n_local):
    blk = pl.program_id(0)
    k0 = pl.multiple_of(kstart_ref[blk] * GRID_W, GRID_W)
    for hh in range(2):
        q = q_ref[hh]
        kl = k_ref[hh, pl.ds(k0, n_local), :]
        vl = v_ref[hh, pl.ds(k0, n_local), :]
        s_loc = _dot_nt(q, kl) * scale + bias_ref[0, hh]
        s_ctx = _dot_nt(q, ck_ref[hh, 0]) * scale
        p_loc, p_ctx = _softmax_rows([s_loc, s_ctx])
        o = _dot(p_loc, vl) + _dot(p_ctx, cv_ref[hh, 0])
        o_ref[:, head_dim * hh:head_dim * (hh + 1)] = o.astype(BF16)


def _nbr_attention(q, k, v, ck, cv, rpb, batch, seq):
    n_heads, n, head_dim = q.shape
    rows = seq // GRID_W
    nblk, n_key_rows, kstart, variant, tables = _nbr_geometry(rows)
    bias = _nbr_bias(rpb, tables)
    m = Q_ROWS * GRID_W
    n_local = n_key_rows * GRID_W
    past = ck.shape[2]
    variant = jnp.asarray(variant, I32)
    grid_spec = pltpu.PrefetchScalarGridSpec(
        num_scalar_prefetch=1,
        grid=(nblk, n_heads // 2, batch),
        in_specs=[
            pl.BlockSpec((2, m, head_dim), lambda r, hp, b, ks: (hp, b * nblk + r, 0)),
            pl.BlockSpec((2, seq, head_dim), lambda r, hp, b, ks: (hp, b, 0)),
            pl.BlockSpec((2, seq, head_dim), lambda r, hp, b, ks: (hp, b, 0)),
            pl.BlockSpec((2, 1, past, head_dim), lambda r, hp, b, ks: (hp, b, 0, 0)),
            pl.BlockSpec((2, 1, past, head_dim), lambda r, hp, b, ks: (hp, b, 0, 0)),
            pl.BlockSpec((1, 2, m, n_local), lambda r, hp, b, ks: (ks[nblk + r], hp, 0, 0)),
        ],
        out_specs=pl.BlockSpec((m, LANES), lambda r, hp, b, ks: (b * nblk + r, hp)),
    )
    return pl.pallas_call(
        functools.partial(_nbr_attn_kernel, scale=head_dim ** -0.5, head_dim=head_dim, n_local=n_local),
        grid_spec=grid_spec,
        out_shape=jax.ShapeDtypeStruct((n, n_heads * head_dim), BF16),
        compiler_params=_cparams(("arbitrary", "arbitrary", "arbitrary")),
    )(jnp.concatenate([jnp.asarray(kstart, I32), variant]), q, k, v, ck, cv, bias)


def _attn_out_kernel(xp_ref, xs_ref, op_ref, os_ref, gt_ref, w_ref, y_ref, *, ntp):
    is_p = pl.program_id(0) < ntp
    x = jnp.where(is_p, xp_ref[...], xs_ref[...])
    o = jnp.where(is_p, op_ref[...], os_ref[...])
    y_ref[...] = x + gt_ref[0, 0] * _dot(o, w_ref[...])


def _attn_out(xp, xs, o_p, o_s, mod4, row_fn, w_bf16):
    (n_p, d), n_s = xp.shape, xs.shape[0]
    tm = TOKEN_TILE
    ntp = n_p // tm
    p_map = lambda i: (jnp.minimum(i, ntp - 1), 0)
    s_map = lambda i: (jnp.maximum(i - ntp, 0), 0)
    return pl.pallas_call(
        functools.partial(_attn_out_kernel, ntp=ntp),
        grid=((n_p + n_s) // tm,),
        in_specs=[
            pl.BlockSpec((tm, d), p_map), pl.BlockSpec((tm, d), s_map),
            pl.BlockSpec((tm, d), p_map), pl.BlockSpec((tm, d), s_map),
            _mod_spec(0, 2, d, row_fn),
            pl.BlockSpec((d, d), lambda i: (0, 0)),
        ],
        out_specs=pl.BlockSpec((tm, d), lambda i: (i, 0)),
        out_shape=jax.ShapeDtypeStruct((n_p + n_s, d), F32),
        compiler_params=_cparams(("arbitrary",)),
    )(xp, xs, o_p, o_s, mod4, w_bf16)


def _lru_in_kernel(y_ref, sh_ref, sc_ref, g_ref, w_ref, gate_ref, xbr_ref, *, d_rnn):
    hn = _norm_mod(y_ref[...], g_ref[...], sc_ref[0, 0], sh_ref[0, 0]).astype(BF16)
    gate_ref[...] = _dot(hn, w_ref[:, :d_rnn])
    xbr_ref[...] = _dot(hn, w_ref[:, d_rnn:])


def _lru_in(y, tile_off, batch, seq, mod4, row_fn, g, w_bf16):
    d = y.shape[1]
    d_rnn = w_bf16.shape[1] // 2
    tt = TOKEN_TILE
    nt = seq // tt
    out_spec = pl.BlockSpec((tt, d_rnn), lambda b, t: (t, b))
    out_shape = jax.ShapeDtypeStruct((seq, batch * d_rnn), F32)
    return pl.pallas_call(
        functools.partial(_lru_in_kernel, d_rnn=d_rnn),
        grid=(batch, nt),
        in_specs=[
            pl.BlockSpec((tt, d), lambda b, t: (tile_off + b * nt + t, 0)),
            _mod_spec(1, 0, d, row_fn),
            _mod_spec(1, 1, d, row_fn),
            pl.BlockSpec((1, d), lambda b, t: (0, 0)),
            pl.BlockSpec((d, 2 * d_rnn), lambda b, t: (0, 0)),
        ],
        out_specs=[out_spec, out_spec],
        out_shape=[out_shape, out_shape],
        compiler_params=_cparams(("arbitrary", "arbitrary")),
    )(y, mod4, mod4, g, w_bf16)


def _scan_kernel(prev_ref, cur_ref, next_ref, cw_ref, cb_ref, wa_ref, wx_ref, ba_ref, bx_ref, lam_ref,
                 h0_ref, h_ref, hfin_ref, a_scr, b_scr, carry, *, batch, steps, nchunk):
    rev = pl.program_id(1) == 1
    c = pl.program_id(2)
    c_eff = jnp.where(rev, nchunk - 1 - c, c)
    rows = steps * batch
    prev = jnp.where(c_eff == 0, 0.0, prev_ref[...])
    nxt = jnp.where(c_eff == nchunk - 1, 0.0, next_ref[...])
    xfull = jnp.concatenate([prev, cur_ref[...], nxt], axis=0)
    cw = cw_ref[...]
    xc = cw[0:1, :] * xfull[0:rows]
    for j in range(1, cw.shape[0]):
        xc = xc + cw[j:j + 1, :] * xfull[j * batch:j * batch + rows]
    xc = xc + cb_ref[...]
    xcb = xc.astype(BF16)
    r = jax.nn.sigmoid(_dot(xcb, wa_ref[0, 0]) + ba_ref[0])
    ig = jax.nn.sigmoid(_dot(xcb, wx_ref[0, 0]) + bx_ref[0])
    neg_lam = -lam_ref[0]
    softplus = jnp.maximum(neg_lam, 0.0) + jnp.log1p(jnp.exp(-jnp.abs(neg_lam)))
    log_a = -LRU_C * r * softplus
    a_scr[...] = jnp.exp(log_a)
    b_scr[...] = jnp.sqrt(-_expm1(2.0 * log_a)) * (ig * xc)

    @pl.when(c == 0)
    def _():
        carry[...] = h0_ref[0]

    def step(t, h):
        te = jnp.where(rev, steps - 1 - t, t)
        r0 = pl.multiple_of(te * batch, batch)
        h = a_scr[pl.ds(r0, batch), :] * h + b_scr[pl.ds(r0, batch), :]
        h_ref[0, pl.ds(r0, batch), :] = h
        return h

    h = lax.fori_loop(0, steps, step, carry[...])
    carry[...] = h

    @pl.when(c == nchunk - 1)
    def _():
        hfin_ref[0] = h


def _lru_scan(xbr_t, batch, seq, conv_w, conv_b, wa_bf16, wx_bf16, ba, bx, lam, h0):
    d_rnn = conv_w.shape[1]
    cb = d_rnn // N_LRU_BLOCKS
    x2 = xbr_t.reshape(seq * batch, d_rnn)
    assert batch % 8 == 0 and conv_w.shape[0] == 4
    steps = max(SCAN_ROWS // batch, 2)
    rows = steps * batch
    nchunk = seq // steps
    halo = 2 * batch
    hpc = rows // halo
    n_halo = seq * batch // halo
    ce = lambda d, c: jnp.where(d == 1, nchunk - 1 - c, c)
    vec = lambda a: a.reshape(2, 1, d_rnn)
    vspec = pl.BlockSpec((1, 1, cb), lambda n, d, c: (d, 0, n))
    wspec = pl.BlockSpec((1, 1, cb, cb), lambda n, d, c: (d, n, 0, 0))
    return pl.pallas_call(
        functools.partial(_scan_kernel, batch=batch, steps=steps, nchunk=nchunk),
        grid=(N_LRU_BLOCKS, 2, nchunk),
        in_specs=[
            pl.BlockSpec((halo, cb), lambda n, d, c: (jnp.maximum(ce(d, c) * hpc - 1, 0), n)),
            pl.BlockSpec((rows, cb), lambda n, d, c: (ce(d, c), n)),
            pl.BlockSpec((halo, cb), lambda n, d, c: (jnp.minimum((ce(d, c) + 1) * hpc, n_halo - 1), n)),
            pl.BlockSpec((4, cb), lambda n, d, c: (0, n)),
            pl.BlockSpec((1, cb), lambda n, d, c: (0, n)),
            wspec, wspec, vspec, vspec, vspec,
            pl.BlockSpec((1, batch, cb), lambda n, d, c: (d, 0, n)),
        ],
        out_specs=[
            pl.BlockSpec((1, rows, cb), lambda n, d, c: (d, ce(d, c), n)),
            pl.BlockSpec((1, batch, cb), lambda n, d, c: (d, 0, n)),
        ],
        out_shape=[
            jax.ShapeDtypeStruct((2, seq * batch, d_rnn), F32),
            jax.ShapeDtypeStruct((2, batch, d_rnn), F32),
        ],
        scratch_shapes=[pltpu.VMEM((rows, cb), F32), pltpu.VMEM((rows, cb), F32),
                        pltpu.VMEM((batch, cb), F32)],
        compiler_params=_cparams(("arbitrary", "arbitrary", "arbitrary")),
    )(x2, x2, x2, conv_w, conv_b[None, :], wa_bf16, wx_bf16, vec(ba), vec(bx), vec(lam), h0)


def _lru_out_kernel(y_ref, gp_ref, gs_ref, hp_ref, hs_ref, gt_ref, w_ref, o_ref, *, ntp):
    is_p = pl.program_id(0) < ntp
    gate = jnp.where(is_p, gp_ref[...], gs_ref[...])
    hsum = jnp.where(is_p, hp_ref[0] + hp_ref[1], hs_ref[0] + hs_ref[1])
    z = (jax.nn.gelu(gate) * hsum).astype(BF16)
    o_ref[...] = y_ref[...] + gt_ref[0, 0] * _dot(z, w_ref[...])


def _lru_out(y, gate_p, gate_s, h_p, h_s, dims_p, dims_s, mod4, row_fn, w_bf16):
    n, d = y.shape
    d_rnn = w_bf16.shape[0]
    tt = TOKEN_TILE
    (bp, tp), (bs, ts) = dims_p, dims_s
    ntp = bp * tp // tt
    npt, nst = tp // tt, ts // tt
    ip = lambda i: jnp.minimum(i, ntp - 1)
    is_ = lambda i: jnp.maximum(i - ntp, 0)
    return pl.pallas_call(
        functools.partial(_lru_out_kernel, ntp=ntp),
        grid=(n // tt,),
        in_specs=[
            pl.BlockSpec((tt, d), lambda i: (i, 0)),
            pl.BlockSpec((tt, d_rnn), lambda i: (ip(i) % npt, ip(i) // npt)),
            pl.BlockSpec((tt, d_rnn), lambda i: (is_(i) % nst, is_(i) // nst)),
            pl.BlockSpec((2, tt, d_rnn), lambda i: (0, ip(i) % npt, ip(i) // npt)),
            pl.BlockSpec((2, tt, d_rnn), lambda i: (0, is_(i) % nst, is_(i) // nst)),
            _mod_spec(1, 2, d, row_fn),
            pl.BlockSpec((d_rnn, d), lambda i: (0, 0)),
        ],
        out_specs=pl.BlockSpec((tt, d), lambda i: (i, 0)),
        out_shape=jax.ShapeDtypeStruct((n, d), F32),
        compiler_params=_cparams(("arbitrary",)),
    )(y, gate_p, gate_s, h_p.reshape(2, tp, bp * d_rnn), h_s.reshape(2, ts, bs * d_rnn), mod4, w_bf16)


def _router_kernel(y_ref, sh_ref, sc_ref, g_ref, rwt_ref, rb_ref, bkt_ref, rank_ref, wext_ref, cnt_ref,
                   carry, *, n_experts):
    i = pl.program_id(0)

    @pl.when(i == 0)
    def _():
        carry[...] = jnp.zeros_like(carry)

    hn = _norm_mod(y_ref[...], g_ref[...], sc_ref[0, 0], sh_ref[0, 0])
    tm = hn.shape[0]
    logits = lax.dot_general(rwt_ref[...], hn, (((1,), (1,)), ((), ())),
                             precision=lax.Precision.HIGHEST, preferred_element_type=F32)
    scores = jax.nn.sigmoid(logits)
    sel = scores + rb_ref[...]
    per = n_experts // N_GROUPS
    assert per == 4
    srow = [sel[e:e + 1, :] for e in range(n_experts)]
    prow = [scores[e:e + 1, :] for e in range(n_experts)]

    def top2_sum(a, b, c, d):
        hi1, lo1, hi2, lo2 = jnp.maximum(a, b), jnp.minimum(a, b), jnp.maximum(c, d), jnp.minimum(c, d)
        return jnp.maximum(hi1, hi2) + jnp.maximum(jnp.minimum(hi1, hi2), jnp.maximum(lo1, lo2))

    gscore = [top2_sum(*srow[per * g:per * (g + 1)]) for g in range(N_GROUPS)]
    best = jnp.zeros((1, tm), I32)
    best_v = gscore[0]
    for g in range(1, N_GROUPS):
        upd = gscore[g] > best_v
        best = jnp.where(upd, g, best)
        best_v = jnp.where(upd, gscore[g], best_v)
    vs = list(srow[:per])
    ps = list(prow[:per])
    for g in range(1, N_GROUPS):
        for j in range(per):
            vs[j] = jnp.where(best == g, srow[per * g + j], vs[j])
            ps[j] = jnp.where(best == g, prow[per * g + j], ps[j])
    j1 = jnp.zeros((1, tm), I32)
    m1 = vs[0]
    for j in range(1, per):
        upd = vs[j] > m1
        j1 = jnp.where(upd, j, j1)
        m1 = jnp.where(upd, vs[j], m1)
    j2 = jnp.full((1, tm), -1, I32)
    m2 = jnp.full((1, tm), -jnp.inf, F32)
    for j in range(per):
        upd = (j1 != j) & (vs[j] > m2)
        j2 = jnp.where(upd, j, j2)
        m2 = jnp.where(upd, vs[j], m2)
    s1 = jnp.zeros((1, tm), F32)
    s2 = jnp.zeros((1, tm), F32)
    for j in range(per):
        s1 = jnp.where(j1 == j, ps[j], s1)
        s2 = jnp.where(j2 == j, ps[j], s2)
    den = s1 + s2
    w1, w2 = s1 / den, s2 / den
    lo = jnp.minimum(j1, j2)
    hi = jnp.maximum(j1, j2)
    w_lo = jnp.where(j1 < j2, w1, w2)
    w_hi = jnp.where(j1 < j2, w2, w1)
    base = jnp.where(lo == 0, 0, jnp.where(lo == 1, 3, 5))
    bucket = best * 6 + base + hi - lo - 1

    onehot = (lax.broadcasted_iota(I32, (N_BUCKET_SLOTS, tm), 0) == bucket).astype(F32)
    before = (lax.broadcasted_iota(I32, (tm, tm), 0) < lax.broadcasted_iota(I32, (tm, tm), 1)).astype(BF16)
    prefix = _dot(onehot.astype(BF16), before) + carry[...]
    rank = jnp.sum(onehot * prefix, axis=0, keepdims=True)
    carry[...] = carry[...] + jnp.sum(onehot, axis=1, keepdims=True)

    bkt_ref[0] = bucket
    rank_ref[0] = rank.astype(I32)
    slab_row = lax.broadcasted_iota(I32, (LANES, tm), 0)
    slab = jnp.where(slab_row == 0, w_lo, jnp.where(slab_row == 1, w_hi, 0.0))
    wext_ref[...] = slab.T

    @pl.when(i == pl.num_programs(0) - 1)
    def _():
        cnt_ref[...] = jnp.broadcast_to(carry[...], cnt_ref.shape).astype(I32)


def _router(y, mod4, layer, row_fn, g, router_wt, router_b):
    n, d = y.shape
    e = router_wt.shape[0]
    tm = TOKEN_TILE
    nt = n // tm
    tok_spec = pl.BlockSpec((1, 1, tm), lambda i: (i, 0, 0))
    return pl.pallas_call(
        functools.partial(_router_kernel, n_experts=e),
        grid=(nt,),
        in_specs=[
            pl.BlockSpec((tm, d), lambda i: (i, 0)),
            _mod_spec(layer, 3, d, row_fn),
            _mod_spec(layer, 4, d, row_fn),
            pl.BlockSpec((1, d), lambda i: (0, 0)),
            pl.BlockSpec((e, d), lambda i: (0, 0)),
            pl.BlockSpec((e, 1), lambda i: (0, 0)),
        ],
        out_specs=[tok_spec, tok_spec,
                   pl.BlockSpec((tm, LANES), lambda i: (i, 0)),
                   pl.BlockSpec((N_BUCKET_SLOTS, LANES), lambda i: (0, 0))],
        out_shape=[jax.ShapeDtypeStruct((nt, 1, tm), I32), jax.ShapeDtypeStruct((nt, 1, tm), I32),
                   jax.ShapeDtypeStruct((n, LANES), F32),
                   jax.ShapeDtypeStruct((N_BUCKET_SLOTS, LANES), I32)],
        scratch_shapes=[pltpu.VMEM((N_BUCKET_SLOTS, 1), F32)],
        compiler_params=_cparams(("arbitrary",)),
    )(y, mod4, mod4, g, router_wt, router_b[:, None])


def _bucket_layout(counts, n_tokens):
    n_buckets = N_GROUPS * len(PAIR_LO)
    tf = FFN_TILE
    n_tiles = n_tokens // tf + n_buckets
    c = counts[:n_buckets]
    padded = ((c + tf - 1) // tf) * tf
    ends = jnp.cumsum(padded)
    starts = ends - padded
    n_used = ends[-1] // tf
    tile = jnp.arange(n_tiles, dtype=I32)
    valid = tile < n_used
    tile_c = jnp.minimum(tile, n_used - 1)
    tb = jnp.minimum(jnp.searchsorted(ends, tile_c * tf, side="right"), n_buckets - 1).astype(I32)
    grp, pair = tb // len(PAIR_LO), tb % len(PAIR_LO)
    e_lo = grp * 4 + jnp.asarray(PAIR_LO, I32)[pair]
    e_hi = grp * 4 + jnp.asarray(PAIR_HI, I32)[pair]
    starts32 = jnp.zeros((N_BUCKET_SLOTS,), I32).at[:n_buckets].set(starts.astype(I32))
    sched = jnp.stack([tile_c, e_lo, e_hi, valid.astype(I32)]).astype(I32)
    return starts32, sched, n_tiles


def _dispatch_kernel(starts_ref, y_ref, sh_ref, sc_ref, g_ref, wext_ref, bkt_ref, rank_ref, init_ref,
                     xs_ref, buf, sem, *, d):
    del init_ref
    tm = buf.shape[0]
    buf[:, :d] = _norm_mod(y_ref[...], g_ref[...], sc_ref[0, 0], sh_ref[0, 0])
    buf[:, d:] = wext_ref[...]

    def issue(r, carry):
        pos = starts_ref[bkt_ref[0, 0, r]] + rank_ref[0, 0, r]
        pltpu.make_async_copy(buf.at[pl.ds(r, 1)], xs_ref.at[pl.ds(pos, 1)], sem).start()
        return carry

    lax.fori_loop(0, tm, issue, 0)

    def drain(r, carry):
        pltpu.make_async_copy(buf.at[pl.ds(0, 1)], xs_ref.at[pl.ds(0, 1)], sem).wait()
        return carry

    lax.fori_loop(0, tm, drain, 0)


def _dispatch(y, mod4, layer, row_fn, g, wext, bkt, rank, starts32, n_rows):
    n, d = y.shape
    tm = TOKEN_TILE
    width = d + LANES
    smem_spec = pl.BlockSpec((1, 1, tm), lambda i, st: (i, 0, 0), memory_space=pltpu.SMEM)
    mspec = lambda chunk: pl.BlockSpec((1, 1, 1, d), lambda i, st: (layer, row_fn(i), 0, chunk))
    grid_spec = pltpu.PrefetchScalarGridSpec(
        num_scalar_prefetch=1,
        grid=(n // tm,),
        in_specs=[
            pl.BlockSpec((tm, d), lambda i, st: (i, 0)),
            mspec(3), mspec(4),
            pl.BlockSpec((1, d), lambda i, st: (0, 0)),
            pl.BlockSpec((tm, LANES), lambda i, st: (i, 0)),
            smem_spec, smem_spec,
            pl.BlockSpec(memory_space=pl.ANY),
        ],
        out_specs=pl.BlockSpec(memory_space=pl.ANY),
        scratch_shapes=[pltpu.VMEM((tm, width), F32), pltpu.SemaphoreType.DMA(())],
    )
    return pl.pallas_call(
        functools.partial(_dispatch_kernel, d=d),
        grid_spec=grid_spec,
        out_shape=jax.ShapeDtypeStruct((n_rows, width), F32),
        input_output_aliases={8: 0},
        compiler_params=_cparams(("arbitrary",)),
    )(starts32, y, mod4, mod4, g, wext, bkt, rank, jnp.zeros((n_rows, width), F32))


def _ffn_kernel(sched_ref, xs_ref, wg_lo, wu_lo, wd_lo, wg_hi, wu_hi, wd_hi, o_ref, *, d):
    valid = sched_ref[3, pl.program_id(0)] == 1

    @pl.when(jnp.logical_not(valid))
    def _():
        o_ref[...] = jnp.zeros_like(o_ref)

    @pl.when(valid)
    def _():
        x = xs_ref[:, :d].astype(BF16)

        def expert(wg, wu, wd):
            gate = _dot(x, wg[0])
            he = (gate * jax.nn.sigmoid(gate)) * _dot(x, wu[0])
            return _dot(he.astype(BF16), wd[0])

        w_lo = xs_ref[:, d:d + 1]
        w_hi = xs_ref[:, d + 1:d + 2]
        o_ref[...] = w_lo * expert(wg_lo, wu_lo, wd_lo) + w_hi * expert(wg_hi, wu_hi, wd_hi)


def _expert_ffn(xs, sched, n_tiles, wg, wu, wd):
    n_rows, width = xs.shape
    d = width - LANES
    f = wg.shape[2]
    tf = FFN_TILE
    lo = lambda i, s: (s[1, i], 0, 0)
    hi = lambda i, s: (s[2, i], 0, 0)
    grid_spec = pltpu.PrefetchScalarGridSpec(
        num_scalar_prefetch=1,
        grid=(n_tiles,),
        in_specs=[
            pl.BlockSpec((tf, width), lambda i, s: (s[0, i], 0)),
            pl.BlockSpec((1, d, f), lo), pl.BlockSpec((1, d, f), lo), pl.BlockSpec((1, f, d), lo),
            pl.BlockSpec((1, d, f), hi), pl.BlockSpec((1, d, f), hi), pl.BlockSpec((1, f, d), hi),
        ],
        out_specs=pl.BlockSpec((tf, d), lambda i, s: (i, 0)),
    )
    return pl.pallas_call(
        functools.partial(_ffn_kernel, d=d),
        grid_spec=grid_spec,
        out_shape=jax.ShapeDtypeStruct((n_rows, d), F32),
        compiler_params=_cparams(("arbitrary",)),
    )(sched, xs, wg, wu, wd, wg, wu, wd)


def _combine_kernel(starts_ref, y_ref, gt_ref, bkt_ref, rank_ref, ms_ref, o_ref, buf, sem):
    tm = buf.shape[0]

    def issue(r, carry):
        pos = starts_ref[bkt_ref[0, 0, r]] + rank_ref[0, 0, r]
        pltpu.make_async_copy(ms_ref.at[pl.ds(pos, 1)], buf.at[pl.ds(r, 1)], sem).start()
        return carry

    lax.fori_loop(0, tm, issue, 0)

    def drain(r, carry):
        pltpu.make_async_copy(ms_ref.at[pl.ds(0, 1)], buf.at[pl.ds(0, 1)], sem).wait()
        return carry

    lax.fori_loop(0, tm, drain, 0)
    o_ref[...] = y_ref[...] + gt_ref[0, 0] * buf[...]


def _combine(y, tile_off, n_out, mod4, layer, row_fn, bkt, rank, starts32, ms):
    d = y.shape[1]
    tm = TOKEN_TILE
    smem_spec = pl.BlockSpec((1, 1, tm), lambda i, st: (tile_off + i, 0, 0), memory_space=pltpu.SMEM)
    grid_spec = pltpu.PrefetchScalarGridSpec(
        num_scalar_prefetch=1,
        grid=(n_out // tm,),
        in_specs=[
            pl.BlockSpec((tm, d), lambda i, st: (tile_off + i, 0)),
            pl.BlockSpec((1, 1, 1, d), lambda i, st: (layer, row_fn(tile_off + i), 0, 5)),
            smem_spec, smem_spec,
            pl.BlockSpec(memory_space=pl.ANY),
        ],
        out_specs=pl.BlockSpec((tm, d), lambda i, st: (i, 0)),
        scratch_shapes=[pltpu.VMEM((tm, d), F32), pltpu.SemaphoreType.DMA(())],
    )
    return pl.pallas_call(
        _combine_kernel,
        grid_spec=grid_spec,
        out_shape=jax.ShapeDtypeStruct((n_out, d), F32),
        compiler_params=_cparams(("arbitrary",)),
    )(starts32, y, mod4, bkt, rank, ms)


def _moe(y, mod4, layer, row_fn, g, router_wt, router_b, wg, wu, wd):
    bkt, rank, wext, counts = _router(y, mod4, layer, row_fn, g, router_wt, router_b)
    starts32, sched, n_tiles = _bucket_layout(counts[:, 0], y.shape[0])
    xs = _dispatch(y, mod4, layer, row_fn, g, wext, bkt, rank, starts32, n_tiles * FFN_TILE)
    ms = _expert_ffn(xs, sched, n_tiles, wg, wu, wd)
    return bkt, rank, starts32, ms


def kernel(x_prompt, x_sample, c, cache_k, cache_v, state_lru, c_ctx, ada_w, ada_b, norm_mix, norm_ffn,
           attn_w_in, attn_q_gain, attn_k_gain, attn_rpb, attn_w_out, lru_w_in, lru_conv_w, lru_conv_b,
           lru_gate_a_w, lru_gate_a_b, lru_gate_x_w, lru_gate_x_b, lru_lambda, lru_w_out, router_w,
           router_b, moe_w_gate, moe_w_up, moe_w_down):
    bp, tp, d = x_prompt.shape
    bs, ts, _ = x_sample.shape
    n_heads, head_dim = cache_k.shape[3], cache_k.shape[4]
    d_rnn = lru_conv_w.shape[2]
    assert ada_w.shape[0] == 2 and bs + 1 <= 16
    n_p, n_s = bp * tp, bs * ts
    tm = TOKEN_TILE
    assert tp % tm == 0 and ts % tm == 0
    ntp = n_p // tm

    cond16 = jnp.zeros((16, d), F32).at[0].set(c_ctx).at[1:1 + bs].set(c)
    mod4 = _ada_modulation(cond16, ada_w, ada_b).reshape(2, 16, 1, 6 * d)
    row_p = lambda i, *_: 0
    row_s = lambda i, *_: 1 + (i * tm) // ts
    row_all = lambda i, *_: jnp.where(i < ntp, 0, 1 + (jnp.maximum(i - ntp, 0) * tm) // ts)

    w_in = attn_w_in[0].astype(BF16)
    g_mix0 = norm_mix[0][None, :]
    xp2, xs2 = x_prompt.reshape(n_p, d), x_sample.reshape(n_s, d)
    q_p, k_p, v_p, new_k, new_v = _qkv_proj(xp2, mod4, row_p, g_mix0, w_in, attn_q_gain[0],
                                            attn_k_gain[0], n_heads, True)
    q_s, k_s, v_s = _qkv_proj(xs2, mod4, row_s, g_mix0, w_in, attn_q_gain[0], attn_k_gain[0],
                              n_heads, False)
    o_p = _ctx_attention(q_p, k_p, v_p, bp, tp)
    ck = jnp.transpose(cache_k[:, 0], (2, 0, 1, 3)).astype(BF16)
    cv = jnp.transpose(cache_v[:, 0], (2, 0, 1, 3)).astype(BF16)
    o_s = _nbr_attention(q_s, k_s, v_s, ck, cv, attn_rpb[0], bs, ts)
    y = _attn_out(xp2, xs2, o_p, o_s, mod4, row_all, attn_w_out[0].astype(BF16))

    router_wt = router_w.T
    moe_bf16 = lambda w: w.astype(BF16)
    bkt, rank, starts32, ms = _moe(y, mod4, 0, row_all, norm_ffn[0][None, :], router_wt, router_b,
                                   moe_bf16(moe_w_gate[0]), moe_bf16(moe_w_up[0]), moe_bf16(moe_w_down[0]))
    y = _combine(y, 0, n_p + n_s, mod4, 0, row_all, bkt, rank, starts32, ms)

    w_lru_in = lru_w_in[0].astype(BF16)
    g_mix1 = norm_mix[1][None, :]
    gate_p, xbr_p = _lru_in(y, 0, bp, tp, mod4, lambda b, t: 0, g_mix1, w_lru_in)
    gate_s, xbr_s = _lru_in(y, ntp, bs, ts, mod4, lambda b, t: 1 + b, g_mix1, w_lru_in)
    scan_args = (lru_conv_w[0], lru_conv_b[0], lru_gate_a_w[0].astype(BF16), lru_gate_x_w[0].astype(BF16),
                 lru_gate_a_b[0], lru_gate_x_b[0], lru_lambda[0])
    h_p, hfin_p = _lru_scan(xbr_p, bp, tp, *scan_args, jnp.zeros((2, bp, d_rnn), F32))
    h_s, _ = _lru_scan(xbr_s, bs, ts, *scan_args, jnp.transpose(state_lru[:, 0], (1, 0, 2)))
    y = _lru_out(y, gate_p, gate_s, h_p, h_s, (bp, tp), (bs, ts), mod4, row_all, lru_w_out[0].astype(BF16))

    bkt, rank, starts32, ms = _moe(y, mod4, 1, row_all, norm_ffn[1][None, :], router_wt, router_b,
                                   moe_bf16(moe_w_gate[1]), moe_bf16(moe_w_up[1]), moe_bf16(moe_w_down[1]))
    y_p = _combine(y, 0, n_p, mod4, 1, row_all, bkt, rank, starts32, ms)
    y_s = _combine(y, ntp, n_s, mod4, 1, row_all, bkt, rank, starts32, ms)

    return (y_p.reshape(bp, tp, d), y_s.reshape(bs, ts, d),
            new_k.reshape(bp, 1, tp, n_heads, head_dim), new_v.reshape(bp, 1, tp, n_heads, head_dim),
            jnp.transpose(hfin_p, (1, 0, 2))[:, None])
```

```python
import functools

import numpy as np
import jax
import jax.numpy as jnp
from jax import lax
from jax.experimental import pallas as pl
from jax.experimental.pallas import tpu as pltpu

F32 = jnp.float32
BF16 = jnp.bfloat16
I32 = jnp.int32

EPS = 1e-6
NEG_INF = -1e30
GRID_W = 64
WIN_ROWS = 8
WIN_COLS = 16
N_GROUPS = 4
LRU_C = 8.0
N_LRU_BLOCKS = 4

LANES = 128
SUBLANES = 8
TOKEN_TILE = 256
FFN_TILE = 256
Q_ROWS = 4
SCAN_ROWS = 2048
N_BUCKET_SLOTS = 32
PAIR_LO = (0, 0, 0, 1, 1, 2)
PAIR_HI = (1, 2, 3, 2, 3, 3)
N_COND_ROWS = 16
LATENT_COND_ROW = 8
VMEM_LIMIT = 48 * 1024 * 1024


def _cparams(sem, **kw):
    return pltpu.CompilerParams(dimension_semantics=sem, vmem_limit_bytes=VMEM_LIMIT, **kw)


def _norm_mod(x, g, sc, sh):
    ms = jnp.mean(x * x, axis=-1, keepdims=True)
    return (x * lax.rsqrt(ms + EPS)) * g * (1.0 + sc) + sh


def _expm1(x):
    u = jnp.exp(x)
    um1 = u - 1.0
    return jnp.where(um1 == 0.0, x, jnp.where(um1 == -1.0, -1.0, um1 * x / jnp.log(u)))


def _dot(a, b):
    return jnp.dot(a, b, preferred_element_type=F32)


def _dot_nt(a, b):
    return lax.dot_general(a, b, (((1,), (1,)), ((), ())), preferred_element_type=F32)


def _ada_kernel(cond_ref, w_ref, b_ref, o_ref):
    c = cond_ref[...]
    s = (c * jax.nn.sigmoid(c)).astype(BF16)
    o_ref[0] = _dot(s, w_ref[0].astype(BF16)) + b_ref[0]


def _ada_modulation(cond, ada_w, ada_b):
    depth, d, d6 = ada_w.shape
    tn = d6 // 4
    return pl.pallas_call(
        _ada_kernel,
        name="ada_modulation",
        grid=(depth, d6 // tn),
        in_specs=[
            pl.BlockSpec((N_COND_ROWS, d), lambda l, j: (0, 0)),
            pl.BlockSpec((1, d, tn), lambda l, j: (l, 0, j)),
            pl.BlockSpec((1, 1, tn), lambda l, j: (l, 0, j)),
        ],
        out_specs=pl.BlockSpec((1, N_COND_ROWS, tn), lambda l, j: (l, 0, j)),
        out_shape=jax.ShapeDtypeStruct((depth, N_COND_ROWS, d6), F32),
        compiler_params=_cparams(("arbitrary", "arbitrary")),
    )(cond, ada_w, ada_b.reshape(depth, 1, d6))


class _Mod:
    def __init__(self, table, d):
        self.t3 = table
        self.t4 = table.reshape(table.shape[0], N_COND_ROWS, 1, table.shape[2])
        self.d = d

    def row_spec(self, layer, chunk, row_fn):
        return pl.BlockSpec((1, 1, 1, self.d), lambda *ids: (layer, row_fn(*ids), 0, chunk))

    def latent_spec(self, layer, chunk):
        return pl.BlockSpec((1, SUBLANES, self.d), lambda *ids: (layer, LATENT_COND_ROW // SUBLANES, chunk))


class _Stream:
    def __init__(self, batch, seq, tile_off, latent):
        self.batch, self.seq, self.tile_off, self.latent = batch, seq, tile_off, latent
        self.n = batch * seq
        self.tiles = self.n // TOKEN_TILE

    def cond_row(self, i):
        if not self.latent:
            return 0
        return LATENT_COND_ROW + (i * TOKEN_TILE) // self.seq


def _qkv_kernel(x_ref, sh_ref, sc_ref, g_ref, w_ref, qg_ref, kg_ref, q_ref, k_ref, v_ref, *kv_refs,
                d, head_dim):
    hn = _norm_mod(x_ref[...], g_ref[...], sc_ref[0, 0], sh_ref[0, 0]).astype(BF16)
    tm = hn.shape[0]
    lo = lax.broadcasted_iota(I32, (tm, LANES), 1) < head_dim
    heads_per_chunk = 256 // head_dim
    outs = (q_ref, k_ref, v_ref)
    gains = (qg_ref, kg_ref)
    for j in range(3 * d // 256):
        acc = _dot(hn, w_ref[:, 256 * j:256 * (j + 1)])
        part = (256 * j) // d
        col = (256 * j) % d
        for half in range(2):
            a = acc[:, LANES * half:LANES * (half + 1)]
            if part < 2:
                sq = a * a
                s_lo = jnp.sum(jnp.where(lo, sq, 0.0), axis=-1, keepdims=True)
                s_hi = jnp.sum(jnp.where(lo, 0.0, sq), axis=-1, keepdims=True)
                r = jnp.where(lo, lax.rsqrt(s_lo / head_dim + EPS), lax.rsqrt(s_hi / head_dim + EPS))
                a = (a * r) * gains[part][...]
            h0 = col // head_dim + half * (heads_per_chunk // 2)
            outs[part][h0] = a[:, :head_dim].astype(BF16)
            outs[part][h0 + 1] = a[:, head_dim:].astype(BF16)
            if part >= 1 and kv_refs:
                kv_refs[part - 1][:, col + LANES * half:col + LANES * (half + 1)] = a


def _qkv_proj(x2d, st, mod, g, w_bf16, q_gain, k_gain, n_heads, with_kv):
    n, d = x2d.shape
    head_dim = d // n_heads
    assert head_dim * 2 == LANES
    tm = TOKEN_TILE
    hspec = pl.BlockSpec((n_heads, tm, head_dim), lambda i: (0, i, 0))
    hshape = jax.ShapeDtypeStruct((n_heads, n, head_dim), BF16)
    out_specs = [hspec, hspec, hspec]
    out_shape = [hshape, hshape, hshape]
    if with_kv:
        out_specs += [pl.BlockSpec((tm, d), lambda i: (i, 0))] * 2
        out_shape += [jax.ShapeDtypeStruct((n, d), F32)] * 2
    return pl.pallas_call(
        functools.partial(_qkv_kernel, d=d, head_dim=head_dim),
        name="qkv_proj",
        grid=(n // tm,),
        in_specs=[
            pl.BlockSpec((tm, d), lambda i: (i, 0)),
            mod.row_spec(0, 0, st.cond_row),
            mod.row_spec(0, 1, st.cond_row),
            pl.BlockSpec((1, d), lambda i: (0, 0)),
            pl.BlockSpec((d, 3 * d), lambda i: (0, 0)),
            pl.BlockSpec((1, LANES), lambda i: (0, 0)),
            pl.BlockSpec((1, LANES), lambda i: (0, 0)),
        ],
        out_specs=out_specs,
        out_shape=out_shape,
        compiler_params=_cparams(("arbitrary",)),
    )(x2d, mod.t4, mod.t4, g, w_bf16, jnp.tile(q_gain, 2)[None, :], jnp.tile(k_gain, 2)[None, :])


def _softmax_rows(parts):
    m = parts[0].max(axis=-1, keepdims=True)
    for s in parts[1:]:
        m = jnp.maximum(m, s.max(axis=-1, keepdims=True))
    es = [jnp.exp(s - m) for s in parts]
    l = es[0].sum(axis=-1, keepdims=True)
    for e in es[1:]:
        l = l + e.sum(axis=-1, keepdims=True)
    return [(e / l).astype(BF16) for e in es]


def _ctx_attn_kernel(q_ref, k_ref, v_ref, o_ref, *, scale, head_dim):
    for hh in range(2):
        s = _dot_nt(q_ref[hh], k_ref[hh]) * scale
        (p,) = _softmax_rows([s])
        o_ref[:, head_dim * hh:head_dim * (hh + 1)] = _dot(p, v_ref[hh]).astype(BF16)


def _ctx_attention(q, k, v, batch, seq):
    n_heads, n, head_dim = q.shape
    spec = pl.BlockSpec((2, seq, head_dim), lambda hp, b: (hp, b, 0))
    return pl.pallas_call(
        functools.partial(_ctx_attn_kernel, scale=head_dim ** -0.5, head_dim=head_dim),
        name="ctx_attention",
        grid=(n_heads // 2, batch),
        in_specs=[spec, spec, spec],
        out_specs=pl.BlockSpec((seq, LANES), lambda hp, b: (b, hp)),
        out_shape=jax.ShapeDtypeStruct((n, n_heads * head_dim), BF16),
        compiler_params=_cparams(("arbitrary", "arbitrary")),
    )(q, k, v)


def _nbr_geometry(rows):
    n_key_rows = Q_ROWS + WIN_ROWS
    assert rows % Q_ROWS == 0 and rows >= n_key_rows
    nblk = rows // Q_ROWS
    kstart = np.clip(np.arange(nblk) * Q_ROWS - WIN_ROWS // 2, 0, rows - n_key_rows)
    start = np.clip(np.arange(rows) - WIN_ROWS // 2, 0, rows - WIN_ROWS)
    tables, variant = [], []
    for blk in range(nblk):
        r = blk * Q_ROWS + np.arange(Q_ROWS)
        kr = kstart[blk] + np.arange(n_key_rows)
        valid = (kr[None, :] >= start[r][:, None]) & (kr[None, :] < start[r][:, None] + WIN_ROWS)
        dr = np.clip(kr[None, :] - r[:, None] + WIN_ROWS - 1, 0, 2 * WIN_ROWS - 2)
        key = (valid.tobytes(), dr.tobytes())
        keys = [t[0] for t in tables]
        if key not in keys:
            tables.append((key, valid, dr))
        variant.append([t[0] for t in tables].index(key))
    return nblk, n_key_rows, kstart, np.array(variant), [(t[1], t[2]) for t in tables]


def _nbr_bias(rpb, tables):
    h, nd, nc = rpb.shape
    w = GRID_W
    period = 2 * w
    pad_l = (w - 1) - (WIN_COLS - 1)
    r_ext = jnp.pad(rpb.astype(F32), ((0, 0), (0, 0), (pad_l, period - nc - pad_l)))
    toe = jnp.tile(r_ext, (1, 1, w))[:, :, :w * (period - 1)].reshape(h, nd, w, period - 1)
    toe = toe[..., w - 1:2 * w - 1]
    cw = np.arange(w)
    col_start = np.clip(cw - WIN_COLS // 2, 0, w - WIN_COLS)
    col_ok = (cw[None, :] >= col_start[:, None]) & (cw[None, :] < col_start[:, None] + WIN_COLS)
    toe = jnp.where(col_ok[None, None], toe, NEG_INF)
    tq = jnp.transpose(toe, (0, 2, 1, 3)).reshape(h, w, nd * w)
    out = []
    for valid, dr in tables:
        qr, kr = valid.shape
        rows = []
        for i in range(qr):
            js = np.nonzero(valid[i])[0]
            j0, j1 = int(js[0]), int(js[-1]) + 1
            assert (np.diff(dr[i, j0:j1]) == 1).all() and valid[i, j0:j1].all()
            d0 = int(dr[i, j0])
            seg = tq[:, :, d0 * w:(d0 + j1 - j0) * w]
            rows.append(jnp.pad(seg, ((0, 0), (0, 0), (j0 * w, (kr - j1) * w)), constant_values=NEG_INF))
        out.append(jnp.stack(rows, axis=1).reshape(h, qr * w, kr * w))
    return jnp.stack(out, axis=0)


def _nbr_attn_kernel(kstart_ref, q_ref, k_ref, v_ref, ck_ref, cv_ref, bias_ref, o_ref, *,
                     scale, head_dim, n_local):
    blk = pl.program_id(0)
    k0 = pl.multiple_of(kstart_ref[blk] * GRID_W, GRID_W)
    for hh in range(2):
        q = q_ref[hh]
        kl = k_ref[hh, pl.ds(k0, n_local), :]
        vl = v_ref[hh, pl.ds(k0, n_local), :]
        s_loc = _dot_nt(q, kl) * scale + bias_ref[0, hh]
        s_ctx = _dot_nt(q, ck_ref[hh, 0]) * scale
        p_loc, p_ctx = _softmax_rows([s_loc, s_ctx])
        o = _dot(p_loc, vl) + _dot(p_ctx, cv_ref[hh, 0])
        o_ref[:, head_dim * hh:head_dim * (hh + 1)] = o.astype(BF16)


def _nbr_attention(q, k, v, ck, cv, rpb, batch, seq):
    n_heads, n, head_dim = q.shape
    rows = seq // GRID_W
    nblk, n_key_rows, kstart, variant, tables = _nbr_geometry(rows)
    bias = _nbr_bias(rpb, tables)
    m = Q_ROWS * GRID_W
    n_local = n_key_rows * GRID_W
    past = ck.shape[2]
    sched = jnp.asarray(np.concatenate([kstart, variant]), I32)
    grid_spec = pltpu.PrefetchScalarGridSpec(
        num_scalar_prefetch=1,
        grid=(nblk, n_heads // 2, batch),
        in_specs=[
            pl.BlockSpec((2, m, head_dim), lambda r, hp, b, ks: (hp, b * nblk + r, 0)),
            pl.BlockSpec((2, seq, head_dim), lambda r, hp, b, ks: (hp, b, 0)),
            pl.BlockSpec((2, seq, head_dim), lambda r, hp, b, ks: (hp, b, 0)),
            pl.BlockSpec((2, 1, past, head_dim), lambda r, hp, b, ks: (hp, b, 0, 0)),
            pl.BlockSpec((2, 1, past, head_dim), lambda r, hp, b, ks: (hp, b, 0, 0)),
            pl.BlockSpec((1, 2, m, n_local), lambda r, hp, b, ks: (ks[nblk + r], hp, 0, 0)),
        ],
        out_specs=pl.BlockSpec((m, LANES), lambda r, hp, b, ks: (b * nblk + r, hp)),
    )
    return pl.pallas_call(
        functools.partial(_nbr_attn_kernel, scale=head_dim ** -0.5, head_dim=head_dim, n_local=n_local),
        name="nbr_attention",
        grid_spec=grid_spec,
        out_shape=jax.ShapeDtypeStruct((n, n_heads * head_dim), BF16),
        compiler_params=_cparams(("arbitrary", "arbitrary", "arbitrary")),
    )(sched, q, k, v, ck, cv, bias)


def _attn_out_kernel(x_ref, o_ref, gt_ref, w_ref, y_ref):
    y_ref[...] = x_ref[...] + gt_ref[0, 0] * _dot(o_ref[...], w_ref[...])


def _attn_out(x2d, o, st, mod, w_bf16):
    n, d = x2d.shape
    tm = TOKEN_TILE
    tile = pl.BlockSpec((tm, d), lambda i: (i, 0))
    return pl.pallas_call(
        _attn_out_kernel,
        name="attn_out",
        grid=(n // tm,),
        in_specs=[tile, tile, mod.row_spec(0, 2, st.cond_row), pl.BlockSpec((d, d), lambda i: (0, 0))],
        out_specs=tile,
        out_shape=jax.ShapeDtypeStruct((n, d), F32),
        compiler_params=_cparams(("arbitrary",)),
    )(x2d, o, mod.t4, w_bf16)


def _lru_in_kernel(y_ref, sh_ref, sc_ref, g_ref, w_ref, gate_ref, xbr_ref, *, d_rnn, latent):
    bsz, tt, d = y_ref.shape
    x = jnp.concatenate([y_ref[:, t, :] for t in range(tt)], axis=0)
    if latent:
        x3 = x.reshape(tt, bsz, d)
        hn = _norm_mod(x3, g_ref[...][None], sc_ref[0][None], sh_ref[0][None]).reshape(tt * bsz, d)
    else:
        hn = _norm_mod(x, g_ref[...], sc_ref[0, 0], sh_ref[0, 0])
    hn = hn.astype(BF16)
    gate_ref[...] = _dot(hn, w_ref[:, :d_rnn])
    xbr_ref[...] = _dot(hn, w_ref[:, d_rnn:])


def _lru_in(y3, st, mod, g, w_bf16):
    bsz, seq, d = y3.shape
    d_rnn = w_bf16.shape[1] // 2
    tt = TOKEN_TILE // bsz
    assert (not st.latent) or bsz == SUBLANES
    mspec = (lambda c: mod.latent_spec(1, c)) if st.latent else (lambda c: mod.row_spec(1, c, lambda t: 0))
    out_spec = pl.BlockSpec((tt * bsz, d_rnn), lambda t: (t, 0))
    out_shape = jax.ShapeDtypeStruct((seq * bsz, d_rnn), F32)
    return pl.pallas_call(
        functools.partial(_lru_in_kernel, d_rnn=d_rnn, latent=st.latent),
        name="lru_in",
        grid=(seq // tt,),
        in_specs=[
            pl.BlockSpec((bsz, tt, d), lambda t: (0, t, 0)),
            mspec(0), mspec(1),
            pl.BlockSpec((1, d), lambda t: (0, 0)),
            pl.BlockSpec((d, 2 * d_rnn), lambda t: (0, 0)),
        ],
        out_specs=[out_spec, out_spec],
        out_shape=[out_shape, out_shape],
        compiler_params=_cparams(("arbitrary",)),
    )(y3, mod.t3 if st.latent else mod.t4, mod.t3 if st.latent else mod.t4, g, w_bf16)


def _scan_kernel(prev_ref, cur_ref, next_ref, cw_ref, cb_ref, wa_ref, wx_ref, ba_ref, bx_ref, lam_ref,
                 h0_ref, h_ref, hfin_ref, a_scr, b_scr, carry, *, batch, steps, nchunk):
    rev = pl.program_id(1) == 1
    c = pl.program_id(2)
    c_eff = jnp.where(rev, nchunk - 1 - c, c)
    rows = steps * batch
    prev = jnp.where(c_eff == 0, 0.0, prev_ref[...])
    nxt = jnp.where(c_eff == nchunk - 1, 0.0, next_ref[...])
    xfull = jnp.concatenate([prev, cur_ref[...], nxt], axis=0)
    cw = cw_ref[...]
    xc = cw[0:1, :] * xfull[0:rows]
    for j in range(1, cw.shape[0]):
        xc = xc + cw[j:j + 1, :] * xfull[j * batch:j * batch + rows]
    xc = xc + cb_ref[...]
    xcb = xc.astype(BF16)
    r = jax.nn.sigmoid(_dot(xcb, wa_ref[0, 0]) + ba_ref[0])
    ig = jax.nn.sigmoid(_dot(xcb, wx_ref[0, 0]) + bx_ref[0])
    neg_lam = -lam_ref[0]
    softplus = jnp.maximum(neg_lam, 0.0) + jnp.log1p(jnp.exp(-jnp.abs(neg_lam)))
    log_a = -LRU_C * r * softplus
    a_scr[...] = jnp.exp(log_a)
    b_scr[...] = jnp.sqrt(-_expm1(2.0 * log_a)) * (ig * xc)

    @pl.when(c == 0)
    def _():
        carry[...] = h0_ref[0]

    def step(t, h):
        te = jnp.where(rev, steps - 1 - t, t)
        r0 = pl.multiple_of(te * batch, batch)
        h = a_scr[pl.ds(r0, batch), :] * h + b_scr[pl.ds(r0, batch), :]
        h_ref[0, pl.ds(r0, batch), :] = h
        return h

    h = lax.fori_loop(0, steps, step, carry[...])
    carry[...] = h

    @pl.when(c == nchunk - 1)
    def _():
        hfin_ref[0] = h


def _lru_scan(x2, batch, seq, conv_w, conv_b, wa_bf16, wx_bf16, ba, bx, lam, h0):
    d_rnn = conv_w.shape[1]
    cb = d_rnn // N_LRU_BLOCKS
    assert batch % SUBLANES == 0 and conv_w.shape[0] == 4
    steps = max(SCAN_ROWS // batch, 2)
    rows = steps * batch
    nchunk = seq // steps
    halo = 2 * batch
    hpc = rows // halo
    n_halo = seq * batch // halo
    ce = lambda d, c: jnp.where(d == 1, nchunk - 1 - c, c)
    vec = lambda a: a.reshape(2, 1, d_rnn)
    vspec = pl.BlockSpec((1, 1, cb), lambda n, d, c: (d, 0, n))
    wspec = pl.BlockSpec((1, 1, cb, cb), lambda n, d, c: (d, n, 0, 0))
    return pl.pallas_call(
        functools.partial(_scan_kernel, batch=batch, steps=steps, nchunk=nchunk),
        name="lru_scan",
        grid=(N_LRU_BLOCKS, 2, nchunk),
        in_specs=[
            pl.BlockSpec((halo, cb), lambda n, d, c: (jnp.maximum(ce(d, c) * hpc - 1, 0), n)),
            pl.BlockSpec((rows, cb), lambda n, d, c: (ce(d, c), n)),
            pl.BlockSpec((halo, cb), lambda n, d, c: (jnp.minimum((ce(d, c) + 1) * hpc, n_halo - 1), n)),
            pl.BlockSpec((4, cb), lambda n, d, c: (0, n)),
            pl.BlockSpec((1, cb), lambda n, d, c: (0, n)),
            wspec, wspec, vspec, vspec, vspec,
            pl.BlockSpec((1, batch, cb), lambda n, d, c: (d, 0, n)),
        ],
        out_specs=[
            pl.BlockSpec((1, rows, cb), lambda n, d, c: (d, ce(d, c), n)),
            pl.BlockSpec((1, batch, cb), lambda n, d, c: (d, 0, n)),
        ],
        out_shape=[
            jax.ShapeDtypeStruct((2, seq * batch, d_rnn), F32),
            jax.ShapeDtypeStruct((2, batch, d_rnn), F32),
        ],
        scratch_shapes=[pltpu.VMEM((rows, cb), F32), pltpu.VMEM((rows, cb), F32),
                        pltpu.VMEM((batch, cb), F32)],
        compiler_params=_cparams(("arbitrary", "arbitrary", "arbitrary")),
    )(x2, x2, x2, conv_w, conv_b[None, :], wa_bf16, wx_bf16, vec(ba), vec(bx), vec(lam), h0)


def _lru_out_kernel(y_ref, gate_ref, h_ref, gt_ref, w_ref, o_ref, z_scr, *, latent):
    bsz, tt, d = y_ref.shape
    z = jax.nn.gelu(gate_ref[...]) * (h_ref[0] + h_ref[1])
    z_scr[...] = z.reshape(tt, bsz, z.shape[1])
    zb = jnp.concatenate([z_scr[:, b, :] for b in range(bsz)], axis=0)
    out = _dot(zb.astype(BF16), w_ref[...]).reshape(bsz, tt, d)
    gt = gt_ref[0][:, None, :] if latent else gt_ref[0]
    o_ref[...] = y_ref[...] + gt * out


def _lru_out(y3, gate, h, st, mod, w_bf16):
    bsz, seq, d = y3.shape
    d_rnn = w_bf16.shape[0]
    tt = TOKEN_TILE // bsz
    gspec = mod.latent_spec(1, 2) if st.latent else mod.row_spec(1, 2, lambda t: 0)
    return pl.pallas_call(
        functools.partial(_lru_out_kernel, latent=st.latent),
        name="lru_out",
        grid=(seq // tt,),
        in_specs=[
            pl.BlockSpec((bsz, tt, d), lambda t: (0, t, 0)),
            pl.BlockSpec((tt * bsz, d_rnn), lambda t: (t, 0)),
            pl.BlockSpec((2, tt * bsz, d_rnn), lambda t: (0, t, 0)),
            gspec,
            pl.BlockSpec((d_rnn, d), lambda t: (0, 0)),
        ],
        out_specs=pl.BlockSpec((bsz, tt, d), lambda t: (0, t, 0)),
        out_shape=jax.ShapeDtypeStruct((bsz, seq, d), F32),
        scratch_shapes=[pltpu.VMEM((tt, bsz, d_rnn), F32)],
        compiler_params=_cparams(("arbitrary",)),
    )(y3, gate, h, mod.t3 if st.latent else mod.t4, w_bf16)


def _dual_specs(shape, sp, ss, extra=0):
    pmap = lambda i, *_: (jnp.minimum(i, sp.tiles - 1), 0)
    smap = lambda i, *_: (jnp.maximum(i - sp.tiles, 0), 0)
    return pl.BlockSpec(shape, pmap), pl.BlockSpec(shape, smap)


def _joint_cond_row(sp, ss):
    return lambda i, *_: jnp.where(i < sp.tiles, 0, ss.cond_row(jnp.maximum(i - sp.tiles, 0)))


def _router_kernel(yp_ref, ys_ref, sh_ref, sc_ref, g_ref, rwt_ref, rb_ref, bkt_ref, rank_ref, wext_ref,
                   cnt_ref, carry, *, n_experts, ntp):
    i = pl.program_id(0)

    @pl.when(i == 0)
    def _():
        carry[...] = jnp.zeros_like(carry)

    y = jnp.where(i < ntp, yp_ref[...], ys_ref[...])
    hn = _norm_mod(y, g_ref[...], sc_ref[0, 0], sh_ref[0, 0])
    tm = hn.shape[0]
    logits = lax.dot_general(rwt_ref[...], hn, (((1,), (1,)), ((), ())),
                             precision=lax.Precision.HIGHEST, preferred_element_type=F32)
    scores = jax.nn.sigmoid(logits)
    sel = scores + rb_ref[...]
    per = n_experts // N_GROUPS
    assert per == 4
    srow = [sel[e:e + 1, :] for e in range(n_experts)]
    prow = [scores[e:e + 1, :] for e in range(n_experts)]

    def top2_sum(a, b, c, d):
        hi1, lo1, hi2, lo2 = jnp.maximum(a, b), jnp.minimum(a, b), jnp.maximum(c, d), jnp.minimum(c, d)
        return jnp.maximum(hi1, hi2) + jnp.maximum(jnp.minimum(hi1, hi2), jnp.maximum(lo1, lo2))

    gscore = [top2_sum(*srow[per * g:per * (g + 1)]) for g in range(N_GROUPS)]
    best = jnp.zeros((1, tm), I32)
    best_v = gscore[0]
    for g in range(1, N_GROUPS):
        upd = gscore[g] > best_v
        best = jnp.where(upd, g, best)
        best_v = jnp.where(upd, gscore[g], best_v)
    vs = list(srow[:per])
    ps = list(prow[:per])
    for g in range(1, N_GROUPS):
        for j in range(per):
            vs[j] = jnp.where(best == g, srow[per * g + j], vs[j])
            ps[j] = jnp.where(best == g, prow[per * g + j], ps[j])
    j1 = jnp.zeros((1, tm), I32)
    m1 = vs[0]
    for j in range(1, per):
        upd = vs[j] > m1
        j1 = jnp.where(upd, j, j1)
        m1 = jnp.where(upd, vs[j], m1)
    j2 = jnp.full((1, tm), -1, I32)
    m2 = jnp.full((1, tm), -jnp.inf, F32)
    for j in range(per):
        upd = (j1 != j) & (vs[j] > m2)
        j2 = jnp.where(upd, j, j2)
        m2 = jnp.where(upd, vs[j], m2)
    s1 = jnp.zeros((1, tm), F32)
    s2 = jnp.zeros((1, tm), F32)
    for j in range(per):
        s1 = jnp.where(j1 == j, ps[j], s1)
        s2 = jnp.where(j2 == j, ps[j], s2)
    den = s1 + s2
    w1, w2 = s1 / den, s2 / den
    lo = jnp.minimum(j1, j2)
    hi = jnp.maximum(j1, j2)
    w_lo = jnp.where(j1 < j2, w1, w2)
    w_hi = jnp.where(j1 < j2, w2, w1)
    base = jnp.where(lo == 0, 0, jnp.where(lo == 1, 3, 5))
    bucket = best * 6 + base + hi - lo - 1

    onehot = (lax.broadcasted_iota(I32, (N_BUCKET_SLOTS, tm), 0) == bucket).astype(F32)
    before = (lax.broadcasted_iota(I32, (tm, tm), 0) < lax.broadcasted_iota(I32, (tm, tm), 1)).astype(BF16)
    prefix = _dot(onehot.astype(BF16), before) + carry[...]
    rank = jnp.sum(onehot * prefix, axis=0, keepdims=True)
    carry[...] = carry[...] + jnp.sum(onehot, axis=1, keepdims=True)

    bkt_ref[0] = bucket
    rank_ref[0] = rank.astype(I32)
    slab_row = lax.broadcasted_iota(I32, (LANES, tm), 0)
    slab = jnp.where(slab_row == 0, w_lo, jnp.where(slab_row == 1, w_hi, 0.0))
    wext_ref[...] = slab.T

    @pl.when(i == pl.num_programs(0) - 1)
    def _():
        cnt_ref[...] = jnp.broadcast_to(carry[...], cnt_ref.shape).astype(I32)


def _router(yp, ys, sp, ss, mod, layer, g, router_wt, router_b):
    d = yp.shape[1]
    e = router_wt.shape[0]
    tm = TOKEN_TILE
    nt = sp.tiles + ss.tiles
    row = _joint_cond_row(sp, ss)
    tok_spec = pl.BlockSpec((1, 1, tm), lambda i: (i, 0, 0))
    pspec, sspec = _dual_specs((tm, d), sp, ss)
    return pl.pallas_call(
        functools.partial(_router_kernel, n_experts=e, ntp=sp.tiles),
        name="moe_router",
        grid=(nt,),
        in_specs=[
            pspec, sspec,
            mod.row_spec(layer, 3, row),
            mod.row_spec(layer, 4, row),
            pl.BlockSpec((1, d), lambda i: (0, 0)),
            pl.BlockSpec((e, d), lambda i: (0, 0)),
            pl.BlockSpec((e, 1), lambda i: (0, 0)),
        ],
        out_specs=[tok_spec, tok_spec,
                   pl.BlockSpec((tm, LANES), lambda i: (i, 0)),
                   pl.BlockSpec((N_BUCKET_SLOTS, LANES), lambda i: (0, 0))],
        out_shape=[jax.ShapeDtypeStruct((nt, 1, tm), I32), jax.ShapeDtypeStruct((nt, 1, tm), I32),
                   jax.ShapeDtypeStruct((nt * tm, LANES), F32),
                   jax.ShapeDtypeStruct((N_BUCKET_SLOTS, LANES), I32)],
        scratch_shapes=[pltpu.VMEM((N_BUCKET_SLOTS, 1), F32)],
        compiler_params=_cparams(("arbitrary",)),
    )(yp, ys, mod.t4, mod.t4, g, router_wt, router_b[:, None])


def _bucket_layout(counts, n_tokens):
    n_buckets = N_GROUPS * len(PAIR_LO)
    tf = FFN_TILE
    n_tiles = n_tokens // tf + n_buckets
    c = counts[:n_buckets]
    padded = ((c + tf - 1) // tf) * tf
    ends = jnp.cumsum(padded)
    starts = ends - padded
    n_used = ends[-1] // tf
    tile = jnp.arange(n_tiles, dtype=I32)
    valid = tile < n_used
    tile_c = jnp.minimum(tile, n_used - 1)
    tb = jnp.sum((ends[None, :] <= (tile_c * tf)[:, None]).astype(I32), axis=1)
    tb = jnp.minimum(tb, n_buckets - 1)
    grp, pair = tb // len(PAIR_LO), tb % len(PAIR_LO)
    pair_lo = sum(jnp.where(pair == p, PAIR_LO[p], 0) for p in range(len(PAIR_LO)))
    pair_hi = sum(jnp.where(pair == p, PAIR_HI[p], 0) for p in range(len(PAIR_HI)))
    starts32 = jnp.concatenate([starts.astype(I32), jnp.zeros((N_BUCKET_SLOTS - n_buckets,), I32)])
    sched = jnp.stack([tile_c, grp * 4 + pair_lo, grp * 4 + pair_hi, valid.astype(I32)]).astype(I32)
    return starts32, sched, n_tiles


def _pos_kernel(starts_ref, bkt_ref, rank_ref, pos_ref):
    bkt = bkt_ref[...]
    pos = rank_ref[...]
    for b in range(N_GROUPS * len(PAIR_LO)):
        pos = pos + jnp.where(bkt == b, starts_ref[b], 0)
    pos_ref[...] = pos


def _positions(starts32, bkt, rank):
    nt, _, tm = bkt.shape
    grid_spec = pltpu.PrefetchScalarGridSpec(
        num_scalar_prefetch=1,
        grid=(1,),
        in_specs=[pl.BlockSpec((nt, 1, tm), lambda i, st: (0, 0, 0))] * 2,
        out_specs=pl.BlockSpec((nt, 1, tm), lambda i, st: (0, 0, 0)),
    )
    return pl.pallas_call(
        _pos_kernel, name="moe_positions", grid_spec=grid_spec,
        out_shape=jax.ShapeDtypeStruct((nt, 1, tm), I32),
        compiler_params=_cparams(("arbitrary",)),
    )(starts32, bkt, rank)


def _dispatch_kernel(yp_ref, ys_ref, sh_ref, sc_ref, g_ref, wext_ref, pos_ref, init_ref,
                     xs_ref, buf, sem, *, d, ntp):
    del init_ref
    tm = buf.shape[0]
    y = jnp.where(pl.program_id(0) < ntp, yp_ref[...], ys_ref[...])
    buf[:, :d] = _norm_mod(y, g_ref[...], sc_ref[0, 0], sh_ref[0, 0])
    buf[:, d:] = wext_ref[...]

    def issue(r, carry):
        pltpu.make_async_copy(buf.at[pl.ds(r, 1)], xs_ref.at[pl.ds(pos_ref[0, 0, r], 1)], sem).start()
        return carry

    lax.fori_loop(0, tm, issue, 0, unroll=8)
    pltpu.make_async_copy(buf, xs_ref.at[pl.ds(0, tm)], sem).wait()


def _dispatch(yp, ys, sp, ss, mod, layer, g, wext, pos, n_rows):
    d = yp.shape[1]
    tm = TOKEN_TILE
    width = d + LANES
    row = _joint_cond_row(sp, ss)
    pspec, sspec = _dual_specs((tm, d), sp, ss)
    return pl.pallas_call(
        functools.partial(_dispatch_kernel, d=d, ntp=sp.tiles),
        name="moe_dispatch",
        grid=(sp.tiles + ss.tiles,),
        in_specs=[
            pspec, sspec,
            mod.row_spec(layer, 3, row), mod.row_spec(layer, 4, row),
            pl.BlockSpec((1, d), lambda i: (0, 0)),
            pl.BlockSpec((tm, LANES), lambda i: (i, 0)),
            pl.BlockSpec((1, 1, tm), lambda i: (i, 0, 0), memory_space=pltpu.SMEM),
            pl.BlockSpec(memory_space=pl.ANY),
        ],
        out_specs=pl.BlockSpec(memory_space=pl.ANY),
        out_shape=jax.ShapeDtypeStruct((n_rows, width), F32),
        scratch_shapes=[pltpu.VMEM((tm, width), F32), pltpu.SemaphoreType.DMA(())],
        input_output_aliases={7: 0},
        compiler_params=_cparams(("arbitrary",), disable_bounds_checks=True),
    )(yp, ys, mod.t4, mod.t4, g, wext, pos, jnp.zeros((n_rows, width), F32))


def _ffn_kernel(sched_ref, xs_ref, wg_lo, wu_lo, wd_lo, wg_hi, wu_hi, wd_hi, o_ref, *, d):
    valid = sched_ref[3, pl.program_id(0)] == 1

    @pl.when(jnp.logical_not(valid))
    def _():
        o_ref[...] = jnp.zeros_like(o_ref)

    @pl.when(valid)
    def _():
        x = xs_ref[:, :d].astype(BF16)

        def expert(wg, wu, wd):
            gate = _dot(x, wg[0])
            he = (gate * jax.nn.sigmoid(gate)) * _dot(x, wu[0])
            return _dot(he.astype(BF16), wd[0])

        w_lo = xs_ref[:, d:d + 1]
        w_hi = xs_ref[:, d + 1:d + 2]
        o_ref[...] = w_lo * expert(wg_lo, wu_lo, wd_lo) + w_hi * expert(wg_hi, wu_hi, wd_hi)


def _expert_ffn(xs, sched, n_tiles, wg, wu, wd):
    n_rows, width = xs.shape
    d = width - LANES
    f = wg.shape[2]
    tf = FFN_TILE
    lo = lambda i, s: (s[1, i], 0, 0)
    hi = lambda i, s: (s[2, i], 0, 0)
    grid_spec = pltpu.PrefetchScalarGridSpec(
        num_scalar_prefetch=1,
        grid=(n_tiles,),
        in_specs=[
            pl.BlockSpec((tf, width), lambda i, s: (s[0, i], 0)),
            pl.BlockSpec((1, d, f), lo), pl.BlockSpec((1, d, f), lo), pl.BlockSpec((1, f, d), lo),
            pl.BlockSpec((1, d, f), hi), pl.BlockSpec((1, d, f), hi), pl.BlockSpec((1, f, d), hi),
        ],
        out_specs=pl.BlockSpec((tf, d), lambda i, s: (i, 0)),
    )
    return pl.pallas_call(
        functools.partial(_ffn_kernel, d=d),
        name="moe_expert_ffn",
        grid_spec=grid_spec,
        out_shape=jax.ShapeDtypeStruct((n_rows, d), F32),
        compiler_params=_cparams(("arbitrary",)),
    )(sched, xs, wg, wu, wd, wg, wu, wd)


def _combine_kernel(y_ref, gt_ref, pos_ref, ms_ref, o_ref, buf, sem):
    tm = buf.shape[0]

    def issue(r, carry):
        pltpu.make_async_copy(ms_ref.at[pl.ds(pos_ref[0, 0, r], 1)], buf.at[pl.ds(r, 1)], sem).start()
        return carry

    lax.fori_loop(0, tm, issue, 0, unroll=8)
    pltpu.make_async_copy(ms_ref.at[pl.ds(0, tm)], buf, sem).wait()
    o_ref[...] = y_ref[...] + gt_ref[0, 0] * buf[...]


def _combine(y2d, st, mod, layer, pos, ms):
    n, d = y2d.shape
    tm = TOKEN_TILE
    tile = pl.BlockSpec((tm, d), lambda i: (i, 0))
    return pl.pallas_call(
        _combine_kernel,
        name="moe_combine",
        grid=(n // tm,),
        in_specs=[
            tile,
            mod.row_spec(layer, 5, st.cond_row),
            pl.BlockSpec((1, 1, tm), lambda i: (st.tile_off + i, 0, 0), memory_space=pltpu.SMEM),
            pl.BlockSpec(memory_space=pl.ANY),
        ],
        out_specs=tile,
        out_shape=jax.ShapeDtypeStruct((n, d), F32),
        scratch_shapes=[pltpu.VMEM((tm, d), F32), pltpu.SemaphoreType.DMA(())],
        compiler_params=_cparams(("arbitrary",), disable_bounds_checks=True),
    )(y2d, mod.t4, pos, ms)


def _moe_residual(yp, ys, sp, ss, mod, layer, g, router_wt, router_b, wg, wu, wd):
    bkt, rank, wext, counts = _router(yp, ys, sp, ss, mod, layer, g, router_wt, router_b)
    starts32, sched, n_tiles = _bucket_layout(counts[:, 0], sp.n + ss.n)
    pos = _positions(starts32, bkt, rank)
    xs = _dispatch(yp, ys, sp, ss, mod, layer, g, wext, pos, n_tiles * FFN_TILE)
    ms = _expert_ffn(xs, sched, n_tiles, wg, wu, wd)
    return _combine(yp, sp, mod, layer, pos, ms), _combine(ys, ss, mod, layer, pos, ms)


def kernel(x_prompt, x_sample, c, cache_k, cache_v, state_lru, c_ctx, ada_w, ada_b, norm_mix, norm_ffn,
           attn_w_in, attn_q_gain, attn_k_gain, attn_rpb, attn_w_out, lru_w_in, lru_conv_w, lru_conv_b,
           lru_gate_a_w, lru_gate_a_b, lru_gate_x_w, lru_gate_x_b, lru_lambda, lru_w_out, router_w,
           router_b, moe_w_gate, moe_w_up, moe_w_down):
    bp, tp, d = x_prompt.shape
    bs, ts, _ = x_sample.shape
    n_heads, head_dim = cache_k.shape[3], cache_k.shape[4]
    d_rnn = lru_conv_w.shape[2]
    assert ada_w.shape[0] == 2 and bs <= N_COND_ROWS - LATENT_COND_ROW
    assert tp % TOKEN_TILE == 0 and ts % TOKEN_TILE == 0
    sp = _Stream(bp, tp, 0, latent=False)
    ss = _Stream(bs, ts, sp.tiles, latent=True)

    cond = jnp.zeros((N_COND_ROWS, d), F32).at[0].set(c_ctx).at[LATENT_COND_ROW:LATENT_COND_ROW + bs].set(c)
    mod = _Mod(_ada_modulation(cond, ada_w, ada_b), d)

    w_in = attn_w_in[0].astype(BF16)
    w_out = attn_w_out[0].astype(BF16)
    g_mix0 = norm_mix[0][None, :]
    xp2, xs2 = x_prompt.reshape(sp.n, d), x_sample.reshape(ss.n, d)
    q_p, k_p, v_p, new_k, new_v = _qkv_proj(xp2, sp, mod, g_mix0, w_in, attn_q_gain[0], attn_k_gain[0],
                                            n_heads, True)
    q_s, k_s, v_s = _qkv_proj(xs2, ss, mod, g_mix0, w_in, attn_q_gain[0], attn_k_gain[0], n_heads, False)
    o_p = _ctx_attention(q_p, k_p, v_p, bp, tp)
    ck = jnp.transpose(cache_k[:, 0], (2, 0, 1, 3)).astype(BF16)
    cv = jnp.transpose(cache_v[:, 0], (2, 0, 1, 3)).astype(BF16)
    o_s = _nbr_attention(q_s, k_s, v_s, ck, cv, attn_rpb[0], bs, ts)
    yp = _attn_out(xp2, o_p, sp, mod, w_out)
    ys = _attn_out(xs2, o_s, ss, mod, w_out)

    router_wt = router_w.T
    moe_w = lambda l: (moe_w_gate[l].astype(BF16), moe_w_up[l].astype(BF16), moe_w_down[l].astype(BF16))
    yp, ys = _moe_residual(yp, ys, sp, ss, mod, 0, norm_ffn[0][None, :], router_wt, router_b, *moe_w(0))

    w_lru_in = lru_w_in[0].astype(BF16)
    w_lru_out = lru_w_out[0].astype(BF16)
    g_mix1 = norm_mix[1][None, :]
    scan_args = (lru_conv_w[0], lru_conv_b[0], lru_gate_a_w[0].astype(BF16), lru_gate_x_w[0].astype(BF16),
                 lru_gate_a_b[0], lru_gate_x_b[0], lru_lambda[0])
    hfin = {}
    ys_out = []
    for y2, st, h0 in ((yp, sp, jnp.zeros((2, bp, d_rnn), F32)),
                       (ys, ss, jnp.transpose(state_lru[:, 0], (1, 0, 2)))):
        y3 = y2.reshape(st.batch, st.seq, d)
        gate, xbr = _lru_in(y3, st, mod, g_mix1, w_lru_in)
        h, hfin[st.latent] = _lru_scan(xbr, st.batch, st.seq, *scan_args, h0)
        ys_out.append(_lru_out(y3, gate, h, st, mod, w_lru_out).reshape(st.n, d))
    yp, ys = ys_out
    yp, ys = _moe_residual(yp, ys, sp, ss, mod, 1, norm_ffn[1][None, :], router_wt, router_b, *moe_w(1))

    return (yp.reshape(bp, tp, d), ys.reshape(bs, ts, d),
            new_k.reshape(bp, 1, tp, n_heads, head_dim), new_v.reshape(bp, 1, tp, n_heads, head_dim),
            jnp.transpose(hfin[False], (1, 0, 2))[:, None])
```

```python
import functools

import numpy as np
import jax
import jax.numpy as jnp
from jax import lax
from jax.experimental import pallas as pl
from jax.experimental.pallas import tpu as pltpu

F32 = jnp.float32
BF16 = jnp.bfloat16
I32 = jnp.int32

EPS = 1e-6
NEG_INF = -1e30
GRID_W = 64
WIN_ROWS = 8
WIN_COLS = 16
N_GROUPS = 4
LRU_C = 8.0
N_LRU_BLOCKS = 4

LANES = 128
SUBLANES = 8
TOKEN_TILE = 256
FFN_TILE = 256
Q_ROWS = 4
SCAN_ROWS = 512
ATTN_HEADS = 4
N_BUCKET_SLOTS = 32
PAIR_LO = (0, 0, 0, 1, 1, 2)
PAIR_HI = (1, 2, 3, 2, 3, 3)
N_COND_ROWS = 16
LATENT_COND_ROW = 8
VMEM_LIMIT = 48 * 1024 * 1024


def _cparams(sem, **kw):
    return pltpu.CompilerParams(dimension_semantics=sem, vmem_limit_bytes=VMEM_LIMIT, **kw)


def _norm_mod(x, g, sc, sh):
    ms = jnp.mean(x * x, axis=-1, keepdims=True)
    return (x * lax.rsqrt(ms + EPS)) * g * (1.0 + sc) + sh


def _expm1(x):
    u = jnp.exp(x)
    um1 = u - 1.0
    return jnp.where(um1 == 0.0, x, jnp.where(um1 == -1.0, -1.0, um1 * x / jnp.log(u)))


def _dot(a, b):
    return jnp.dot(a, b, preferred_element_type=F32)


def _dot_nt(a, b):
    return lax.dot_general(a, b, (((1,), (1,)), ((), ())), preferred_element_type=F32)


def _ada_kernel(cond_ref, w_ref, b_ref, o_ref):
    c = cond_ref[...]
    s = (c * jax.nn.sigmoid(c)).astype(BF16)
    o_ref[0] = _dot(s, w_ref[0].astype(BF16)) + b_ref[0]


def _ada_modulation(cond, ada_w, ada_b):
    depth, d, d6 = ada_w.shape
    tn = d6 // 4
    return pl.pallas_call(
        _ada_kernel,
        name="ada_modulation",
        grid=(depth, d6 // tn),
        in_specs=[
            pl.BlockSpec((N_COND_ROWS, d), lambda l, j: (0, 0)),
            pl.BlockSpec((1, d, tn), lambda l, j: (l, 0, j)),
            pl.BlockSpec((1, 1, tn), lambda l, j: (l, 0, j)),
        ],
        out_specs=pl.BlockSpec((1, N_COND_ROWS, tn), lambda l, j: (l, 0, j)),
        out_shape=jax.ShapeDtypeStruct((depth, N_COND_ROWS, d6), F32),
        compiler_params=_cparams(("arbitrary", "arbitrary")),
    )(cond, ada_w, ada_b.reshape(depth, 1, d6))


class _Mod:
    def __init__(self, table, d):
        self.t3 = table
        self.t4 = table.reshape(table.shape[0], N_COND_ROWS, 1, table.shape[2])
        self.d = d

    def row_spec(self, layer, chunk, row_fn):
        return pl.BlockSpec((1, 1, 1, self.d), lambda *ids: (layer, row_fn(*ids), 0, chunk))

    def latent_spec(self, layer, chunk):
        return pl.BlockSpec((1, SUBLANES, self.d), lambda *ids: (layer, LATENT_COND_ROW // SUBLANES, chunk))


class _Stream:
    def __init__(self, batch, seq, tile_off, latent):
        self.batch, self.seq, self.tile_off, self.latent = batch, seq, tile_off, latent
        self.n = batch * seq
        self.tiles = self.n // TOKEN_TILE

    def cond_row(self, i):
        if not self.latent:
            return 0
        return LATENT_COND_ROW + (i * TOKEN_TILE) // self.seq


def _qkv_kernel(x_ref, sh_ref, sc_ref, g_ref, w_ref, qg_ref, kg_ref, q_ref, k_ref, v_ref, *kv_refs,
                d, head_dim, q_scale):
    hn = _norm_mod(x_ref[...], g_ref[...], sc_ref[0, 0], sh_ref[0, 0]).astype(BF16)
    tm = hn.shape[0]
    lo = lax.broadcasted_iota(I32, (tm, LANES), 1) < head_dim
    heads_per_chunk = 256 // head_dim
    outs = (q_ref, k_ref, v_ref)
    gains = (qg_ref, kg_ref)
    for j in range(3 * d // 256):
        acc = _dot(hn, w_ref[:, 256 * j:256 * (j + 1)])
        part = (256 * j) // d
        col = (256 * j) % d
        for half in range(2):
            a = acc[:, LANES * half:LANES * (half + 1)]
            if part < 2:
                sq = a * a
                s_lo = jnp.sum(jnp.where(lo, sq, 0.0), axis=-1, keepdims=True)
                s_hi = jnp.sum(jnp.where(lo, 0.0, sq), axis=-1, keepdims=True)
                r = jnp.where(lo, lax.rsqrt(s_lo / head_dim + EPS), lax.rsqrt(s_hi / head_dim + EPS))
                a = (a * r) * gains[part][...]
                if part == 0:
                    a = a * q_scale
            h0 = col // head_dim + half * (heads_per_chunk // 2)
            outs[part][h0] = a[:, :head_dim].astype(BF16)
            outs[part][h0 + 1] = a[:, head_dim:].astype(BF16)
            if part >= 1 and kv_refs:
                kv_refs[part - 1][:, col + LANES * half:col + LANES * (half + 1)] = a


def _qkv_proj(x2d, st, mod, g, w_bf16, q_gain, k_gain, n_heads, with_kv):
    n, d = x2d.shape
    head_dim = d // n_heads
    assert head_dim * 2 == LANES
    tm = TOKEN_TILE
    hspec = pl.BlockSpec((n_heads, tm, head_dim), lambda i: (0, i, 0))
    hshape = jax.ShapeDtypeStruct((n_heads, n, head_dim), BF16)
    out_specs = [hspec, hspec, hspec]
    out_shape = [hshape, hshape, hshape]
    if with_kv:
        out_specs += [pl.BlockSpec((tm, d), lambda i: (i, 0))] * 2
        out_shape += [jax.ShapeDtypeStruct((n, d), F32)] * 2
    return pl.pallas_call(
        functools.partial(_qkv_kernel, d=d, head_dim=head_dim, q_scale=_attn_scale(head_dim)),
        name="qkv_proj",
        grid=(n // tm,),
        in_specs=[
            pl.BlockSpec((tm, d), lambda i: (i, 0)),
            mod.row_spec(0, 0, st.cond_row),
            mod.row_spec(0, 1, st.cond_row),
            pl.BlockSpec((1, d), lambda i: (0, 0)),
            pl.BlockSpec((d, 3 * d), lambda i: (0, 0)),
            pl.BlockSpec((1, LANES), lambda i: (0, 0)),
            pl.BlockSpec((1, LANES), lambda i: (0, 0)),
        ],
        out_specs=out_specs,
        out_shape=out_shape,
        compiler_params=_cparams(("arbitrary",)),
    )(x2d, mod.t4, mod.t4, g, w_bf16, jnp.tile(q_gain, 2)[None, :], jnp.tile(k_gain, 2)[None, :])


def _attn_scale(head_dim):
    scale = head_dim ** -0.5
    assert float(np.log2(scale)).is_integer()
    return scale


def _softmax_pv(scores, values):
    m = scores[0].max(axis=-1, keepdims=True)
    for s in scores[1:]:
        m = jnp.maximum(m, s.max(axis=-1, keepdims=True))
    es = [jnp.exp(s - m) for s in scores]
    l = es[0].sum(axis=-1, keepdims=True)
    for e in es[1:]:
        l = l + e.sum(axis=-1, keepdims=True)
    o = _dot(es[0].astype(BF16), values[0])
    for e, v in zip(es[1:], values[1:]):
        o = o + _dot(e.astype(BF16), v)
    return o / l


def _ctx_attn_kernel(q_ref, k_ref, v_ref, o_ref, *, head_dim):
    for h in range(q_ref.shape[0]):
        o = _softmax_pv([_dot_nt(q_ref[h], k_ref[h])], [v_ref[h]])
        o_ref[:, head_dim * h:head_dim * (h + 1)] = o.astype(BF16)


def _ctx_attention(q, k, v, batch, seq):
    n_heads, n, head_dim = q.shape
    spec = pl.BlockSpec((n_heads, seq, head_dim), lambda b: (0, b, 0))
    return pl.pallas_call(
        functools.partial(_ctx_attn_kernel, head_dim=head_dim),
        name="ctx_attention",
        grid=(batch,),
        in_specs=[spec, spec, spec],
        out_specs=pl.BlockSpec((seq, n_heads * head_dim), lambda b: (b, 0)),
        out_shape=jax.ShapeDtypeStruct((n, n_heads * head_dim), BF16),
        compiler_params=_cparams(("arbitrary",)),
    )(q, k, v)


def _nbr_geometry(rows):
    n_key_rows = Q_ROWS + WIN_ROWS
    assert rows % Q_ROWS == 0 and rows >= n_key_rows
    nblk = rows // Q_ROWS
    kstart = np.clip(np.arange(nblk) * Q_ROWS - WIN_ROWS // 2, 0, rows - n_key_rows)
    start = np.clip(np.arange(rows) - WIN_ROWS // 2, 0, rows - WIN_ROWS)
    tables, variant = [], []
    for blk in range(nblk):
        r = blk * Q_ROWS + np.arange(Q_ROWS)
        kr = kstart[blk] + np.arange(n_key_rows)
        valid = (kr[None, :] >= start[r][:, None]) & (kr[None, :] < start[r][:, None] + WIN_ROWS)
        dr = np.clip(kr[None, :] - r[:, None] + WIN_ROWS - 1, 0, 2 * WIN_ROWS - 2)
        key = (valid.tobytes(), dr.tobytes())
        keys = [t[0] for t in tables]
        if key not in keys:
            tables.append((key, valid, dr))
        variant.append([t[0] for t in tables].index(key))
    return nblk, n_key_rows, kstart, np.array(variant), [(t[1], t[2]) for t in tables]


def _nbr_bias(rpb, tables):
    h, nd, nc = rpb.shape
    w = GRID_W
    period = 2 * w
    pad_l = (w - 1) - (WIN_COLS - 1)
    r_ext = jnp.pad(rpb.astype(F32), ((0, 0), (0, 0), (pad_l, period - nc - pad_l)))
    toe = jnp.tile(r_ext, (1, 1, w))[:, :, :w * (period - 1)].reshape(h, nd, w, period - 1)
    toe = toe[..., w - 1:2 * w - 1]
    cw = np.arange(w)
    col_start = np.clip(cw - WIN_COLS // 2, 0, w - WIN_COLS)
    col_ok = (cw[None, :] >= col_start[:, None]) & (cw[None, :] < col_start[:, None] + WIN_COLS)
    toe = jnp.where(col_ok[None, None], toe, NEG_INF)
    tq = jnp.transpose(toe, (0, 2, 1, 3)).reshape(h, w, nd * w)
    out = []
    for valid, dr in tables:
        qr, kr = valid.shape
        rows = []
        for i in range(qr):
            js = np.nonzero(valid[i])[0]
            j0, j1 = int(js[0]), int(js[-1]) + 1
            assert (np.diff(dr[i, j0:j1]) == 1).all() and valid[i, j0:j1].all()
            d0 = int(dr[i, j0])
            seg = tq[:, :, d0 * w:(d0 + j1 - j0) * w]
            rows.append(jnp.pad(seg, ((0, 0), (0, 0), (j0 * w, (kr - j1) * w)), constant_values=NEG_INF))
        out.append(jnp.stack(rows, axis=1).reshape(h, qr * w, kr * w))
    return jnp.stack(out, axis=0)


def _nbr_attn_kernel(kstart_ref, q_ref, k_ref, v_ref, ck_ref, cv_ref, bias_ref, o_ref, *,
                     head_dim, n_local):
    blk = pl.program_id(0)
    k0 = pl.multiple_of(kstart_ref[blk] * GRID_W, GRID_W)
    for h in range(q_ref.shape[0]):
        q = q_ref[h]
        kl = k_ref[h, pl.ds(k0, n_local), :]
        vl = v_ref[h, pl.ds(k0, n_local), :]
        s_loc = _dot_nt(q, kl) + bias_ref[0, h]
        s_ctx = _dot_nt(q, ck_ref[h, 0])
        o = _softmax_pv([s_loc, s_ctx], [vl, cv_ref[h, 0]])
        o_ref[:, head_dim * h:head_dim * (h + 1)] = o.astype(BF16)


def _nbr_attention(q, k, v, ck, cv, rpb, batch, seq):
    n_heads, n, head_dim = q.shape
    rows = seq // GRID_W
    nblk, n_key_rows, kstart, variant, tables = _nbr_geometry(rows)
    bias = _nbr_bias(rpb, tables)
    m = Q_ROWS * GRID_W
    n_local = n_key_rows * GRID_W
    past = ck.shape[2]
    sched = jnp.asarray(np.concatenate([kstart, variant]), I32)
    nh = ATTN_HEADS
    grid_spec = pltpu.PrefetchScalarGridSpec(
        num_scalar_prefetch=1,
        grid=(nblk, n_heads // nh, batch),
        in_specs=[
            pl.BlockSpec((nh, m, head_dim), lambda r, hg, b, ks: (hg, b * nblk + r, 0)),
            pl.BlockSpec((nh, seq, head_dim), lambda r, hg, b, ks: (hg, b, 0)),
            pl.BlockSpec((nh, seq, head_dim), lambda r, hg, b, ks: (hg, b, 0)),
            pl.BlockSpec((nh, 1, past, head_dim), lambda r, hg, b, ks: (hg, b, 0, 0)),
            pl.BlockSpec((nh, 1, past, head_dim), lambda r, hg, b, ks: (hg, b, 0, 0)),
            pl.BlockSpec((1, nh, m, n_local), lambda r, hg, b, ks: (ks[nblk + r], hg, 0, 0)),
        ],
        out_specs=pl.BlockSpec((m, nh * head_dim), lambda r, hg, b, ks: (b * nblk + r, hg)),
    )
    return pl.pallas_call(
        functools.partial(_nbr_attn_kernel, head_dim=head_dim, n_local=n_local),
        name="nbr_attention",
        grid_spec=grid_spec,
        out_shape=jax.ShapeDtypeStruct((n, n_heads * head_dim), BF16),
        compiler_params=_cparams(("arbitrary", "arbitrary", "arbitrary")),
    )(sched, q, k, v, ck, cv, bias)


def _attn_out_kernel(x_ref, o_ref, gt_ref, w_ref, y_ref):
    y_ref[...] = x_ref[...] + gt_ref[0, 0] * _dot(o_ref[...], w_ref[...])


def _attn_out(x2d, o, st, mod, w_bf16):
    n, d = x2d.shape
    tm = TOKEN_TILE
    tile = pl.BlockSpec((tm, d), lambda i: (i, 0))
    return pl.pallas_call(
        _attn_out_kernel,
        name="attn_out",
        grid=(n // tm,),
        in_specs=[tile, tile, mod.row_spec(0, 2, st.cond_row), pl.BlockSpec((d, d), lambda i: (0, 0))],
        out_specs=tile,
        out_shape=jax.ShapeDtypeStruct((n, d), F32),
        compiler_params=_cparams(("arbitrary",)),
    )(x2d, o, mod.t4, w_bf16)


def _lru_in_kernel(y_ref, sh_ref, sc_ref, g_ref, w_ref, gate_ref, xbr_ref, *, d_rnn, latent):
    bsz, tt, d = y_ref.shape
    x = jnp.concatenate([y_ref[:, t, :] for t in range(tt)], axis=0)
    if latent:
        x3 = x.reshape(tt, bsz, d)
        hn = _norm_mod(x3, g_ref[...][None], sc_ref[0][None], sh_ref[0][None]).reshape(tt * bsz, d)
    else:
        hn = _norm_mod(x, g_ref[...], sc_ref[0, 0], sh_ref[0, 0])
    hn = hn.astype(BF16)
    gate_ref[...] = _dot(hn, w_ref[:, :d_rnn])
    xbr_ref[...] = _dot(hn, w_ref[:, d_rnn:])


def _lru_in(y3, st, mod, g, w_bf16):
    bsz, seq, d = y3.shape
    d_rnn = w_bf16.shape[1] // 2
    tt = TOKEN_TILE // bsz
    assert (not st.latent) or bsz == SUBLANES
    mspec = (lambda c: mod.latent_spec(1, c)) if st.latent else (lambda c: mod.row_spec(1, c, lambda t: 0))
    out_spec = pl.BlockSpec((tt * bsz, d_rnn), lambda t: (t, 0))
    out_shape = jax.ShapeDtypeStruct((seq * bsz, d_rnn), F32)
    return pl.pallas_call(
        functools.partial(_lru_in_kernel, d_rnn=d_rnn, latent=st.latent),
        name="lru_in",
        grid=(seq // tt,),
        in_specs=[
            pl.BlockSpec((bsz, tt, d), lambda t: (0, t, 0)),
            mspec(0), mspec(1),
            pl.BlockSpec((1, d), lambda t: (0, 0)),
            pl.BlockSpec((d, 2 * d_rnn), lambda t: (0, 0)),
        ],
        out_specs=[out_spec, out_spec],
        out_shape=[out_shape, out_shape],
        compiler_params=_cparams(("arbitrary",)),
    )(y3, mod.t3 if st.latent else mod.t4, mod.t3 if st.latent else mod.t4, g, w_bf16)


def _scan_kernel(prev_ref, cur_ref, next_ref, cw_ref, cb_ref, wa_ref, wx_ref, ba_ref, bx_ref, lam_ref,
                 h0_ref, h_ref, hfin_ref, a_scr, b_scr, carry, *, batch, steps, nchunk):
    rev = pl.program_id(0) == 1
    c = pl.program_id(1)
    c_eff = jnp.where(rev, nchunk - 1 - c, c)
    rows = steps * batch
    cb = wa_ref.shape[2]
    for n in range(wa_ref.shape[1]):
        ch = slice(n * cb, (n + 1) * cb)
        prev = jnp.where(c_eff == 0, 0.0, prev_ref[:, ch])
        nxt = jnp.where(c_eff == nchunk - 1, 0.0, next_ref[:, ch])
        xfull = jnp.concatenate([prev, cur_ref[:, ch], nxt], axis=0)
        cw = cw_ref[:, ch]
        xc = cw[0:1, :] * xfull[0:rows]
        for j in range(1, cw.shape[0]):
            xc = xc + cw[j:j + 1, :] * xfull[j * batch:j * batch + rows]
        xc = xc + cb_ref[:, ch]
        xcb = xc.astype(BF16)
        r = jax.nn.sigmoid(_dot(xcb, wa_ref[0, n]) + ba_ref[0, :, ch])
        ig = jax.nn.sigmoid(_dot(xcb, wx_ref[0, n]) + bx_ref[0, :, ch])
        neg_lam = -lam_ref[0, :, ch]
        softplus = jnp.maximum(neg_lam, 0.0) + jnp.log1p(jnp.exp(-jnp.abs(neg_lam)))
        log_a = -LRU_C * r * softplus
        a_scr[:, ch] = jnp.exp(log_a)
        b_scr[:, ch] = jnp.sqrt(-_expm1(2.0 * log_a)) * (ig * xc)

    @pl.when(c == 0)
    def _():
        carry[...] = h0_ref[0]

    def step(t, h):
        te = jnp.where(rev, steps - 1 - t, t)
        r0 = pl.multiple_of(te * batch, batch)
        h = a_scr[pl.ds(r0, batch), :] * h + b_scr[pl.ds(r0, batch), :]
        h_ref[0, pl.ds(r0, batch), :] = h
        return h

    h = lax.fori_loop(0, steps, step, carry[...], unroll=min(steps, 8))
    carry[...] = h

    @pl.when(c == nchunk - 1)
    def _():
        hfin_ref[0] = h


def _lru_scan(x2, batch, seq, conv_w, conv_b, wa_bf16, wx_bf16, ba, bx, lam, h0):
    d_rnn = conv_w.shape[1]
    nb, cb = wa_bf16.shape[1], wa_bf16.shape[2]
    assert batch % SUBLANES == 0 and conv_w.shape[0] == 4 and nb * cb == d_rnn
    steps = max(SCAN_ROWS // batch, 2)
    rows = steps * batch
    nchunk = seq // steps
    halo = 2 * batch
    hpc = rows // halo
    n_halo = seq * batch // halo
    ce = lambda d, c: jnp.where(d == 1, nchunk - 1 - c, c)
    vec = lambda a: a.reshape(2, 1, d_rnn)
    vspec = pl.BlockSpec((1, 1, d_rnn), lambda d, c: (d, 0, 0))
    wspec = pl.BlockSpec((1, nb, cb, cb), lambda d, c: (d, 0, 0, 0))
    return pl.pallas_call(
        functools.partial(_scan_kernel, batch=batch, steps=steps, nchunk=nchunk),
        name="lru_scan",
        grid=(2, nchunk),
        in_specs=[
            pl.BlockSpec((halo, d_rnn), lambda d, c: (jnp.maximum(ce(d, c) * hpc - 1, 0), 0)),
            pl.BlockSpec((rows, d_rnn), lambda d, c: (ce(d, c), 0)),
            pl.BlockSpec((halo, d_rnn), lambda d, c: (jnp.minimum((ce(d, c) + 1) * hpc, n_halo - 1), 0)),
            pl.BlockSpec((4, d_rnn), lambda d, c: (0, 0)),
            pl.BlockSpec((1, d_rnn), lambda d, c: (0, 0)),
            wspec, wspec, vspec, vspec, vspec,
            pl.BlockSpec((1, batch, d_rnn), lambda d, c: (d, 0, 0)),
        ],
        out_specs=[
            pl.BlockSpec((1, rows, d_rnn), lambda d, c: (d, ce(d, c), 0)),
            pl.BlockSpec((1, batch, d_rnn), lambda d, c: (d, 0, 0)),
        ],
        out_shape=[
            jax.ShapeDtypeStruct((2, seq * batch, d_rnn), F32),
            jax.ShapeDtypeStruct((2, batch, d_rnn), F32),
        ],
        scratch_shapes=[pltpu.VMEM((rows, d_rnn), F32), pltpu.VMEM((rows, d_rnn), F32),
                        pltpu.VMEM((batch, d_rnn), F32)],
        compiler_params=_cparams(("arbitrary", "arbitrary")),
    )(x2, x2, x2, conv_w, conv_b[None, :], wa_bf16, wx_bf16, vec(ba), vec(bx), vec(lam), h0)


def _lru_out_kernel(y_ref, gate_ref, h_ref, gt_ref, w_ref, o_ref, z_scr, *, latent):
    bsz, tt, d = y_ref.shape
    z = jax.nn.gelu(gate_ref[...]) * (h_ref[0] + h_ref[1])
    z_scr[...] = z.reshape(tt, bsz, z.shape[1])
    zb = jnp.concatenate([z_scr[:, b, :] for b in range(bsz)], axis=0)
    out = _dot(zb.astype(BF16), w_ref[...]).reshape(bsz, tt, d)
    gt = gt_ref[0][:, None, :] if latent else gt_ref[0]
    o_ref[...] = y_ref[...] + gt * out


def _lru_out(y3, gate, h, st, mod, w_bf16):
    bsz, seq, d = y3.shape
    d_rnn = w_bf16.shape[0]
    tt = TOKEN_TILE // bsz
    gspec = mod.latent_spec(1, 2) if st.latent else mod.row_spec(1, 2, lambda t: 0)
    return pl.pallas_call(
        functools.partial(_lru_out_kernel, latent=st.latent),
        name="lru_out",
        grid=(seq // tt,),
        in_specs=[
            pl.BlockSpec((bsz, tt, d), lambda t: (0, t, 0)),
            pl.BlockSpec((tt * bsz, d_rnn), lambda t: (t, 0)),
            pl.BlockSpec((2, tt * bsz, d_rnn), lambda t: (0, t, 0)),
            gspec,
            pl.BlockSpec((d_rnn, d), lambda t: (0, 0)),
        ],
        out_specs=pl.BlockSpec((bsz, tt, d), lambda t: (0, t, 0)),
        out_shape=jax.ShapeDtypeStruct((bsz, seq, d), F32),
        scratch_shapes=[pltpu.VMEM((tt, bsz, d_rnn), F32)],
        compiler_params=_cparams(("arbitrary",)),
    )(y3, gate, h, mod.t3 if st.latent else mod.t4, w_bf16)


def _dual_specs(shape, sp, ss, extra=0):
    pmap = lambda i, *_: (jnp.minimum(i, sp.tiles - 1), 0)
    smap = lambda i, *_: (jnp.maximum(i - sp.tiles, 0), 0)
    return pl.BlockSpec(shape, pmap), pl.BlockSpec(shape, smap)


def _joint_cond_row(sp, ss):
    return lambda i, *_: jnp.where(i < sp.tiles, 0, ss.cond_row(jnp.maximum(i - sp.tiles, 0)))


def _router_kernel(yp_ref, ys_ref, sh_ref, sc_ref, g_ref, rwt_ref, rb_ref, bkt_ref, rank_ref, wext_ref,
                   cnt_ref, carry, *, n_experts, ntp):
    i = pl.program_id(0)

    @pl.when(i == 0)
    def _():
        carry[...] = jnp.zeros_like(carry)

    y = jnp.where(i < ntp, yp_ref[...], ys_ref[...])
    hn = _norm_mod(y, g_ref[...], sc_ref[0, 0], sh_ref[0, 0])
    tm = hn.shape[0]
    logits = lax.dot_general(rwt_ref[...], hn, (((1,), (1,)), ((), ())),
                             precision=lax.Precision.HIGHEST, preferred_element_type=F32)
    scores = jax.nn.sigmoid(logits)
    sel = scores + rb_ref[...]
    per = n_experts // N_GROUPS
    assert per == 4
    srow = [sel[e:e + 1, :] for e in range(n_experts)]
    prow = [scores[e:e + 1, :] for e in range(n_experts)]

    def top2_sum(a, b, c, d):
        hi1, lo1, hi2, lo2 = jnp.maximum(a, b), jnp.minimum(a, b), jnp.maximum(c, d), jnp.minimum(c, d)
        return jnp.maximum(hi1, hi2) + jnp.maximum(jnp.minimum(hi1, hi2), jnp.maximum(lo1, lo2))

    gscore = [top2_sum(*srow[per * g:per * (g + 1)]) for g in range(N_GROUPS)]
    best = jnp.zeros((1, tm), I32)
    best_v = gscore[0]
    for g in range(1, N_GROUPS):
        upd = gscore[g] > best_v
        best = jnp.where(upd, g, best)
        best_v = jnp.where(upd, gscore[g], best_v)
    vs = list(srow[:per])
    ps = list(prow[:per])
    for g in range(1, N_GROUPS):
        for j in range(per):
            vs[j] = jnp.where(best == g, srow[per * g + j], vs[j])
            ps[j] = jnp.where(best == g, prow[per * g + j], ps[j])
    j1 = jnp.zeros((1, tm), I32)
    m1 = vs[0]
    for j in range(1, per):
        upd = vs[j] > m1
        j1 = jnp.where(upd, j, j1)
        m1 = jnp.where(upd, vs[j], m1)
    j2 = jnp.full((1, tm), -1, I32)
    m2 = jnp.full((1, tm), -jnp.inf, F32)
    for j in range(per):
        upd = (j1 != j) & (vs[j] > m2)
        j2 = jnp.where(upd, j, j2)
        m2 = jnp.where(upd, vs[j], m2)
    s1 = jnp.zeros((1, tm), F32)
    s2 = jnp.zeros((1, tm), F32)
    for j in range(per):
        s1 = jnp.where(j1 == j, ps[j], s1)
        s2 = jnp.where(j2 == j, ps[j], s2)
    den = s1 + s2
    w1, w2 = s1 / den, s2 / den
    lo = jnp.minimum(j1, j2)
    hi = jnp.maximum(j1, j2)
    w_lo = jnp.where(j1 < j2, w1, w2)
    w_hi = jnp.where(j1 < j2, w2, w1)
    base = jnp.where(lo == 0, 0, jnp.where(lo == 1, 3, 5))
    bucket = best * 6 + base + hi - lo - 1

    onehot = (lax.broadcasted_iota(I32, (N_BUCKET_SLOTS, tm), 0) == bucket).astype(F32)
    before = (lax.broadcasted_iota(I32, (tm, tm), 0) < lax.broadcasted_iota(I32, (tm, tm), 1)).astype(BF16)
    prefix = _dot(onehot.astype(BF16), before) + carry[...]
    rank = jnp.sum(onehot * prefix, axis=0, keepdims=True)
    carry[...] = carry[...] + jnp.sum(onehot, axis=1, keepdims=True)

    bkt_ref[0] = bucket
    rank_ref[0] = rank.astype(I32)
    slab_row = lax.broadcasted_iota(I32, (LANES, tm), 0)
    slab = jnp.where(slab_row == 0, w_lo, jnp.where(slab_row == 1, w_hi, 0.0))
    wext_ref[...] = slab.T

    @pl.when(i == pl.num_programs(0) - 1)
    def _():
        cnt_ref[...] = jnp.broadcast_to(carry[...], cnt_ref.shape).astype(I32)


def _router(yp, ys, sp, ss, mod, layer, g, router_wt, router_b):
    d = yp.shape[1]
    e = router_wt.shape[0]
    tm = TOKEN_TILE
    nt = sp.tiles + ss.tiles
    row = _joint_cond_row(sp, ss)
    tok_spec = pl.BlockSpec((1, 1, tm), lambda i: (i, 0, 0))
    pspec, sspec = _dual_specs((tm, d), sp, ss)
    return pl.pallas_call(
        functools.partial(_router_kernel, n_experts=e, ntp=sp.tiles),
        name="moe_router",
        grid=(nt,),
        in_specs=[
            pspec, sspec,
            mod.row_spec(layer, 3, row),
            mod.row_spec(layer, 4, row),
            pl.BlockSpec((1, d), lambda i: (0, 0)),
            pl.BlockSpec((e, d), lambda i: (0, 0)),
            pl.BlockSpec((e, 1), lambda i: (0, 0)),
        ],
        out_specs=[tok_spec, tok_spec,
                   pl.BlockSpec((tm, LANES), lambda i: (i, 0)),
                   pl.BlockSpec((N_BUCKET_SLOTS, LANES), lambda i: (0, 0))],
        out_shape=[jax.ShapeDtypeStruct((nt, 1, tm), I32), jax.ShapeDtypeStruct((nt, 1, tm), I32),
                   jax.ShapeDtypeStruct((nt * tm, LANES), F32),
                   jax.ShapeDtypeStruct((N_BUCKET_SLOTS, LANES), I32)],
        scratch_shapes=[pltpu.VMEM((N_BUCKET_SLOTS, 1), F32)],
        compiler_params=_cparams(("arbitrary",)),
    )(yp, ys, mod.t4, mod.t4, g, router_wt, router_b[:, None])


def _bucket_layout(counts, n_tokens):
    n_buckets = N_GROUPS * len(PAIR_LO)
    tf = FFN_TILE
    n_tiles = n_tokens // tf + n_buckets
    c = counts[:n_buckets]
    padded = ((c + tf - 1) // tf) * tf
    ends = jnp.cumsum(padded)
    starts = ends - padded
    n_used = ends[-1] // tf
    tile = jnp.arange(n_tiles, dtype=I32)
    valid = tile < n_used
    tile_c = jnp.minimum(tile, n_used - 1)
    tb = jnp.sum((ends[None, :] <= (tile_c * tf)[:, None]).astype(I32), axis=1)
    tb = jnp.minimum(tb, n_buckets - 1)
    grp, pair = tb // len(PAIR_LO), tb % len(PAIR_LO)
    pair_lo = sum(jnp.where(pair == p, PAIR_LO[p], 0) for p in range(len(PAIR_LO)))
    pair_hi = sum(jnp.where(pair == p, PAIR_HI[p], 0) for p in range(len(PAIR_HI)))
    starts32 = jnp.concatenate([starts.astype(I32), jnp.zeros((N_BUCKET_SLOTS - n_buckets,), I32)])
    sched = jnp.stack([tile_c, grp * 4 + pair_lo, grp * 4 + pair_hi, valid.astype(I32)]).astype(I32)
    return starts32, sched, n_tiles


def _pos_kernel(starts_ref, bkt_ref, rank_ref, pos_ref):
    bkt = bkt_ref[...]
    pos = rank_ref[...]
    for b in range(N_GROUPS * len(PAIR_LO)):
        pos = pos + jnp.where(bkt == b, starts_ref[b], 0)
    pos_ref[...] = pos


def _positions(starts32, bkt, rank):
    nt, _, tm = bkt.shape
    grid_spec = pltpu.PrefetchScalarGridSpec(
        num_scalar_prefetch=1,
        grid=(1,),
        in_specs=[pl.BlockSpec((nt, 1, tm), lambda i, st: (0, 0, 0))] * 2,
        out_specs=pl.BlockSpec((nt, 1, tm), lambda i, st: (0, 0, 0)),
    )
    return pl.pallas_call(
        _pos_kernel, name="moe_positions", grid_spec=grid_spec,
        out_shape=jax.ShapeDtypeStruct((nt, 1, tm), I32),
        compiler_params=_cparams(("arbitrary",)),
    )(starts32, bkt, rank)


def _dispatch_kernel(yp_ref, ys_ref, sh_ref, sc_ref, g_ref, wext_ref, pos_ref, init_ref,
                     xs_ref, buf, sem, *, d, ntp):
    del init_ref
    tm = buf.shape[0]
    y = jnp.where(pl.program_id(0) < ntp, yp_ref[...], ys_ref[...])
    buf[:, :d] = _norm_mod(y, g_ref[...], sc_ref[0, 0], sh_ref[0, 0])
    buf[:, d:] = wext_ref[...]

    def issue(r, carry):
        pltpu.make_async_copy(buf.at[pl.ds(r, 1)], xs_ref.at[pl.ds(pos_ref[0, 0, r], 1)], sem).start()
        return carry

    lax.fori_loop(0, tm, issue, 0, unroll=8)
    pltpu.make_async_copy(buf, xs_ref.at[pl.ds(0, tm)], sem).wait()


def _dispatch(yp, ys, sp, ss, mod, layer, g, wext, pos, n_rows):
    d = yp.shape[1]
    tm = TOKEN_TILE
    width = d + LANES
    row = _joint_cond_row(sp, ss)
    pspec, sspec = _dual_specs((tm, d), sp, ss)
    return pl.pallas_call(
        functools.partial(_dispatch_kernel, d=d, ntp=sp.tiles),
        name="moe_dispatch",
        grid=(sp.tiles + ss.tiles,),
        in_specs=[
            pspec, sspec,
            mod.row_spec(layer, 3, row), mod.row_spec(layer, 4, row),
            pl.BlockSpec((1, d), lambda i: (0, 0)),
            pl.BlockSpec((tm, LANES), lambda i: (i, 0)),
            pl.BlockSpec((1, 1, tm), lambda i: (i, 0, 0), memory_space=pltpu.SMEM),
            pl.BlockSpec(memory_space=pl.ANY),
        ],
        out_specs=pl.BlockSpec(memory_space=pl.ANY),
        out_shape=jax.ShapeDtypeStruct((n_rows, width), F32),
        scratch_shapes=[pltpu.VMEM((tm, width), F32), pltpu.SemaphoreType.DMA(())],
        input_output_aliases={7: 0},
        compiler_params=_cparams(("arbitrary",), disable_bounds_checks=True),
    )(yp, ys, mod.t4, mod.t4, g, wext, pos, jnp.zeros((n_rows, width), F32))


def _ffn_kernel(sched_ref, xs_ref, wg_lo, wu_lo, wd_lo, wg_hi, wu_hi, wd_hi, o_ref, *, d):
    valid = sched_ref[3, pl.program_id(0)] == 1

    @pl.when(jnp.logical_not(valid))
    def _():
        o_ref[...] = jnp.zeros_like(o_ref)

    @pl.when(valid)
    def _():
        x = xs_ref[:, :d].astype(BF16)

        def expert(wg, wu, wd):
            gate = _dot(x, wg[0, 0])
            he = (gate * jax.nn.sigmoid(gate)) * _dot(x, wu[0, 0])
            return _dot(he.astype(BF16), wd[0, 0])

        w_lo = xs_ref[:, d:d + 1]
        w_hi = xs_ref[:, d + 1:d + 2]
        o_ref[...] = w_lo * expert(wg_lo, wu_lo, wd_lo) + w_hi * expert(wg_hi, wu_hi, wd_hi)


def _expert_ffn(xs, sched, n_tiles, layer, wg, wu, wd):
    n_rows, width = xs.shape
    d = width - LANES
    f = wg.shape[3]
    tf = FFN_TILE
    lo = lambda i, s: (layer, s[1, i], 0, 0)
    hi = lambda i, s: (layer, s[2, i], 0, 0)
    grid_spec = pltpu.PrefetchScalarGridSpec(
        num_scalar_prefetch=1,
        grid=(n_tiles,),
        in_specs=[
            pl.BlockSpec((tf, width), lambda i, s: (s[0, i], 0)),
            pl.BlockSpec((1, 1, d, f), lo), pl.BlockSpec((1, 1, d, f), lo), pl.BlockSpec((1, 1, f, d), lo),
            pl.BlockSpec((1, 1, d, f), hi), pl.BlockSpec((1, 1, d, f), hi), pl.BlockSpec((1, 1, f, d), hi),
        ],
        out_specs=pl.BlockSpec((tf, d), lambda i, s: (i, 0)),
    )
    return pl.pallas_call(
        functools.partial(_ffn_kernel, d=d),
        name="moe_expert_ffn",
        grid_spec=grid_spec,
        out_shape=jax.ShapeDtypeStruct((n_rows, d), F32),
        compiler_params=_cparams(("arbitrary",)),
    )(sched, xs, wg, wu, wd, wg, wu, wd)


def _combine_kernel(y_ref, gt_ref, pos_ref, ms_ref, o_ref, buf, sem):
    tm = buf.shape[0]

    def issue(r, carry):
        pltpu.make_async_copy(ms_ref.at[pl.ds(pos_ref[0, 0, r], 1)], buf.at[pl.ds(r, 1)], sem).start()
        return carry

    lax.fori_loop(0, tm, issue, 0, unroll=8)
    pltpu.make_async_copy(ms_ref.at[pl.ds(0, tm)], buf, sem).wait()
    o_ref[...] = y_ref[...] + gt_ref[0, 0] * buf[...]


def _combine(y2d, st, mod, layer, pos, ms):
    n, d = y2d.shape
    tm = TOKEN_TILE
    tile = pl.BlockSpec((tm, d), lambda i: (i, 0))
    return pl.pallas_call(
        _combine_kernel,
        name="moe_combine",
        grid=(n // tm,),
        in_specs=[
            tile,
            mod.row_spec(layer, 5, st.cond_row),
            pl.BlockSpec((1, 1, tm), lambda i: (st.tile_off + i, 0, 0), memory_space=pltpu.SMEM),
            pl.BlockSpec(memory_space=pl.ANY),
        ],
        out_specs=tile,
        out_shape=jax.ShapeDtypeStruct((n, d), F32),
        scratch_shapes=[pltpu.VMEM((tm, d), F32), pltpu.SemaphoreType.DMA(())],
        compiler_params=_cparams(("arbitrary",), disable_bounds_checks=True),
    )(y2d, mod.t4, pos, ms)


def _moe_residual(yp, ys, sp, ss, mod, layer, g, router_wt, router_b, wg, wu, wd):
    bkt, rank, wext, counts = _router(yp, ys, sp, ss, mod, layer, g, router_wt, router_b)
    starts32, sched, n_tiles = _bucket_layout(counts[:, 0], sp.n + ss.n)
    pos = _positions(starts32, bkt, rank)
    xs = _dispatch(yp, ys, sp, ss, mod, layer, g, wext, pos, n_tiles * FFN_TILE)
    ms = _expert_ffn(xs, sched, n_tiles, layer, wg, wu, wd)
    return _combine(yp, sp, mod, layer, pos, ms), _combine(ys, ss, mod, layer, pos, ms)


def kernel(x_prompt, x_sample, c, cache_k, cache_v, state_lru, c_ctx, ada_w, ada_b, norm_mix, norm_ffn,
           attn_w_in, attn_q_gain, attn_k_gain, attn_rpb, attn_w_out, lru_w_in, lru_conv_w, lru_conv_b,
           lru_gate_a_w, lru_gate_a_b, lru_gate_x_w, lru_gate_x_b, lru_lambda, lru_w_out, router_w,
           router_b, moe_w_gate, moe_w_up, moe_w_down):
    bp, tp, d = x_prompt.shape
    bs, ts, _ = x_sample.shape
    n_heads, head_dim = cache_k.shape[3], cache_k.shape[4]
    d_rnn = lru_conv_w.shape[2]
    assert ada_w.shape[0] == 2 and bs <= N_COND_ROWS - LATENT_COND_ROW
    assert tp % TOKEN_TILE == 0 and ts % TOKEN_TILE == 0
    sp = _Stream(bp, tp, 0, latent=False)
    ss = _Stream(bs, ts, sp.tiles, latent=True)

    cond = jnp.zeros((N_COND_ROWS, d), F32).at[0].set(c_ctx).at[LATENT_COND_ROW:LATENT_COND_ROW + bs].set(c)
    mod = _Mod(_ada_modulation(cond, ada_w, ada_b), d)

    w_in = attn_w_in[0].astype(BF16)
    w_out = attn_w_out[0].astype(BF16)
    g_mix0 = norm_mix[0][None, :]
    xp2, xs2 = x_prompt.reshape(sp.n, d), x_sample.reshape(ss.n, d)
    q_p, k_p, v_p, new_k, new_v = _qkv_proj(xp2, sp, mod, g_mix0, w_in, attn_q_gain[0], attn_k_gain[0],
                                            n_heads, True)
    q_s, k_s, v_s = _qkv_proj(xs2, ss, mod, g_mix0, w_in, attn_q_gain[0], attn_k_gain[0], n_heads, False)
    o_p = _ctx_attention(q_p, k_p, v_p, bp, tp)
    ck = jnp.transpose(cache_k[:, 0], (2, 0, 1, 3)).astype(BF16)
    cv = jnp.transpose(cache_v[:, 0], (2, 0, 1, 3)).astype(BF16)
    o_s = _nbr_attention(q_s, k_s, v_s, ck, cv, attn_rpb[0], bs, ts)
    yp = _attn_out(xp2, o_p, sp, mod, w_out)
    ys = _attn_out(xs2, o_s, ss, mod, w_out)

    router_wt = router_w.T
    moe_w = (moe_w_gate.astype(BF16), moe_w_up.astype(BF16), moe_w_down.astype(BF16))
    yp, ys = _moe_residual(yp, ys, sp, ss, mod, 0, norm_ffn[0][None, :], router_wt, router_b, *moe_w)

    w_lru_in = lru_w_in[0].astype(BF16)
    w_lru_out = lru_w_out[0].astype(BF16)
    g_mix1 = norm_mix[1][None, :]
    scan_args = (lru_conv_w[0], lru_conv_b[0], lru_gate_a_w[0].astype(BF16), lru_gate_x_w[0].astype(BF16),
                 lru_gate_a_b[0], lru_gate_x_b[0], lru_lambda[0])
    hfin = {}
    ys_out = []
    for y2, st, h0 in ((yp, sp, jnp.zeros((2, bp, d_rnn), F32)),
                       (ys, ss, jnp.transpose(state_lru[:, 0], (1, 0, 2)))):
        y3 = y2.reshape(st.batch, st.seq, d)
        gate, xbr = _lru_in(y3, st, mod, g_mix1, w_lru_in)
        h, hfin[st.latent] = _lru_scan(xbr, st.batch, st.seq, *scan_args, h0)
        ys_out.append(_lru_out(y3, gate, h, st, mod, w_lru_out).reshape(st.n, d))
    yp, ys = ys_out
    yp, ys = _moe_residual(yp, ys, sp, ss, mod, 1, norm_ffn[1][None, :], router_wt, router_b, *moe_w)

    return (yp.reshape(bp, tp, d), ys.reshape(bs, ts, d),
            new_k.reshape(bp, 1, tp, n_heads, head_dim), new_v.reshape(bp, 1, tp, n_heads, head_dim),
            jnp.transpose(hfin[False], (1, 0, 2))[:, None])
```

```python
import functools

import numpy as np
import jax
import jax.numpy as jnp
from jax import lax
from jax.experimental import pallas as pl
from jax.experimental.pallas import tpu as pltpu

F32 = jnp.float32
BF16 = jnp.bfloat16
I32 = jnp.int32

EPS = 1e-6
NEG_INF = -1e30
GRID_W = 64
WIN_ROWS = 8
WIN_COLS = 16
N_GROUPS = 4
LRU_C = 8.0
N_LRU_BLOCKS = 4

LANES = 128
SUBLANES = 8
MXU_COLS = 256
TOKEN_TILE = 256
FFN_TILE = 256
Q_ROWS = 4
SCAN_ROWS = 512
ATTN_PAIRS = 2
N_BUCKET_SLOTS = 32
PAIR_LO = (0, 0, 0, 1, 1, 2)
PAIR_HI = (1, 2, 3, 2, 3, 3)
N_COND_ROWS = 16
LATENT_COND_ROW = 8
VMEM_LIMIT = 48 * 1024 * 1024


def _cparams(sem, **kw):
    return pltpu.CompilerParams(dimension_semantics=sem, vmem_limit_bytes=VMEM_LIMIT, **kw)


def _norm_mod(x, g, sc, sh):
    ms = jnp.mean(x * x, axis=-1, keepdims=True)
    return (x * lax.rsqrt(ms + EPS)) * g * (1.0 + sc) + sh


def _dot(a, b):
    return jnp.dot(a, b, preferred_element_type=F32)


def _dot_nt(a, b):
    return lax.dot_general(a, b, (((1,), (1,)), ((), ())), preferred_element_type=F32)


def _ada_kernel(cond_ref, w_ref, b_ref, o_ref):
    c = cond_ref[...]
    s = (c * jax.nn.sigmoid(c)).astype(BF16)
    o_ref[0] = _dot(s, w_ref[0].astype(BF16)) + b_ref[0]


def _ada_modulation(cond, ada_w, ada_b):
    depth, d, d6 = ada_w.shape
    tn = d6 // 4
    return pl.pallas_call(
        _ada_kernel,
        name="ada_modulation",
        grid=(depth, d6 // tn),
        in_specs=[
            pl.BlockSpec((N_COND_ROWS, d), lambda l, j: (0, 0)),
            pl.BlockSpec((1, d, tn), lambda l, j: (l, 0, j)),
            pl.BlockSpec((1, 1, tn), lambda l, j: (l, 0, j)),
        ],
        out_specs=pl.BlockSpec((1, N_COND_ROWS, tn), lambda l, j: (l, 0, j)),
        out_shape=jax.ShapeDtypeStruct((depth, N_COND_ROWS, d6), F32),
        compiler_params=_cparams(("arbitrary", "arbitrary")),
    )(cond, ada_w, ada_b.reshape(depth, 1, d6))


class _Mod:
    def __init__(self, table, d):
        self.t3 = table
        self.t4 = table.reshape(table.shape[0], N_COND_ROWS, 1, table.shape[2])
        self.d = d

    def row_spec(self, layer, chunk, row_fn):
        return pl.BlockSpec((1, 1, 1, self.d), lambda *ids: (layer, row_fn(*ids), 0, chunk))

    def latent_spec(self, layer, chunk):
        return pl.BlockSpec((1, SUBLANES, self.d), lambda *ids: (layer, LATENT_COND_ROW // SUBLANES, chunk))


class _Stream:
    def __init__(self, batch, seq, tile_off, latent):
        self.batch, self.seq, self.tile_off, self.latent = batch, seq, tile_off, latent
        self.n = batch * seq
        self.tiles = self.n // TOKEN_TILE

    def cond_row(self, i):
        if not self.latent:
            return 0
        return LATENT_COND_ROW + (i * TOKEN_TILE) // self.seq


def _qkv_kernel(x_ref, sh_ref, sc_ref, g_ref, w_ref, qg_ref, kg_ref, gsum_ref, q_ref, k_ref, v_ref, *kv_refs,
                d, head_dim):
    hn = _norm_mod(x_ref[...], g_ref[...], sc_ref[0, 0], sh_ref[0, 0]).astype(BF16)
    outs = (q_ref, k_ref, v_ref)
    gains = (qg_ref, kg_ref)
    for j in range(3 * d // MXU_COLS):
        acc = _dot(hn, w_ref[:, MXU_COLS * j:MXU_COLS * (j + 1)])
        part = (MXU_COLS * j) // d
        col = (MXU_COLS * j) % d
        if part < 2:
            sq = acc * acc
            hi = sq.astype(BF16)
            lo = (sq - hi.astype(F32)).astype(BF16)
            ss = _dot(hi, gsum_ref[...]) + _dot(lo, gsum_ref[...])
            acc = (acc * lax.rsqrt(ss * (1.0 / head_dim) + EPS)) * gains[part][...]
        for half in range(MXU_COLS // LANES):
            outs[part][col // LANES + half] = acc[:, LANES * half:LANES * (half + 1)].astype(BF16)
        if part >= 1 and kv_refs:
            kv_refs[part - 1][:, col:col + MXU_COLS] = acc


def _qkv_proj(x2d, st, mod, g, w_bf16, q_gain, k_gain, n_heads, with_kv):
    n, d = x2d.shape
    head_dim = d // n_heads
    assert head_dim * 2 == LANES
    tm = TOKEN_TILE
    n_pairs = n_heads // 2
    hspec = pl.BlockSpec((n_pairs, tm, LANES), lambda i: (0, i, 0))
    hshape = jax.ShapeDtypeStruct((n_pairs, n, LANES), BF16)
    out_specs = [hspec, hspec, hspec]
    out_shape = [hshape, hshape, hshape]
    if with_kv:
        out_specs += [pl.BlockSpec((tm, d), lambda i: (i, 0))] * 2
        out_shape += [jax.ShapeDtypeStruct((n, d), F32)] * 2
    lane_head = np.arange(MXU_COLS) // head_dim
    gsum = jnp.asarray(lane_head[:, None] == lane_head[None, :], BF16)
    reps = MXU_COLS // head_dim
    qg = jnp.tile(q_gain * _attn_scale(head_dim), reps)[None, :]
    kg = jnp.tile(k_gain, reps)[None, :]
    const = lambda shape: pl.BlockSpec(shape, lambda i: (0,) * len(shape))
    return pl.pallas_call(
        functools.partial(_qkv_kernel, d=d, head_dim=head_dim),
        name="qkv_proj",
        grid=(n // tm,),
        in_specs=[
            pl.BlockSpec((tm, d), lambda i: (i, 0)),
            mod.row_spec(0, 0, st.cond_row),
            mod.row_spec(0, 1, st.cond_row),
            const((1, d)), const((d, 3 * d)), const((1, MXU_COLS)), const((1, MXU_COLS)),
            const((MXU_COLS, MXU_COLS)),
        ],
        out_specs=out_specs,
        out_shape=out_shape,
        compiler_params=_cparams(("arbitrary",)),
    )(x2d, mod.t4, mod.t4, g, w_bf16, qg, kg, gsum)


def _attn_scale(head_dim):
    scale = head_dim ** -0.5
    assert float(np.log2(scale)).is_integer()
    return scale


def _softmax_pv(scores, values):
    m = scores[0].max(axis=-1, keepdims=True)
    for s in scores[1:]:
        m = jnp.maximum(m, s.max(axis=-1, keepdims=True))
    es = [jnp.exp(s - m) for s in scores]
    l = es[0].sum(axis=-1, keepdims=True)
    for e in es[1:]:
        l = l + e.sum(axis=-1, keepdims=True)
    o = _dot(es[0].astype(BF16), values[0])
    for e, v in zip(es[1:], values[1:]):
        o = o + _dot(e.astype(BF16), v)
    return o / l


def _pair_attention(q2, score_keys, values, biases):
    first = lax.broadcasted_iota(I32, q2.shape, 1) < q2.shape[1] // 2
    zero = jnp.zeros_like(q2)
    outs = []
    for head in range(2):
        qh = jnp.where(first, q2, zero) if head == 0 else jnp.where(first, zero, q2)
        scores = []
        for keys, bias in zip(score_keys, biases[head]):
            s = _dot_nt(qh, keys)
            scores.append(s if bias is None else s + bias)
        outs.append(_softmax_pv(scores, values))
    return jnp.where(first, outs[0], outs[1])


def _ctx_attn_kernel(q_ref, k_ref, v_ref, o_ref):
    for hp in range(q_ref.shape[0]):
        o = _pair_attention(q_ref[hp], [k_ref[hp]], [v_ref[hp]], [[None], [None]])
        o_ref[:, LANES * hp:LANES * (hp + 1)] = o.astype(BF16)


def _ctx_attention(q, k, v, batch, seq):
    n_pairs, n, _ = q.shape
    spec = pl.BlockSpec((n_pairs, seq, LANES), lambda b: (0, b, 0))
    return pl.pallas_call(
        _ctx_attn_kernel,
        name="ctx_attention",
        grid=(batch,),
        in_specs=[spec, spec, spec],
        out_specs=pl.BlockSpec((seq, n_pairs * LANES), lambda b: (b, 0)),
        out_shape=jax.ShapeDtypeStruct((n, n_pairs * LANES), BF16),
        compiler_params=_cparams(("arbitrary",)),
    )(q, k, v)


def _nbr_geometry(rows):
    n_key_rows = Q_ROWS + WIN_ROWS
    assert rows % Q_ROWS == 0 and rows >= n_key_rows
    nblk = rows // Q_ROWS
    kstart = np.clip(np.arange(nblk) * Q_ROWS - WIN_ROWS // 2, 0, rows - n_key_rows)
    start = np.clip(np.arange(rows) - WIN_ROWS // 2, 0, rows - WIN_ROWS)
    tables, variant = [], []
    for blk in range(nblk):
        r = blk * Q_ROWS + np.arange(Q_ROWS)
        kr = kstart[blk] + np.arange(n_key_rows)
        valid = (kr[None, :] >= start[r][:, None]) & (kr[None, :] < start[r][:, None] + WIN_ROWS)
        dr = np.clip(kr[None, :] - r[:, None] + WIN_ROWS - 1, 0, 2 * WIN_ROWS - 2)
        key = (valid.tobytes(), dr.tobytes())
        keys = [t[0] for t in tables]
        if key not in keys:
            tables.append((key, valid, dr))
        variant.append([t[0] for t in tables].index(key))
    return nblk, n_key_rows, kstart, np.array(variant), [(t[1], t[2]) for t in tables]


def _nbr_bias(rpb, tables):
    h, nd, nc = rpb.shape
    w = GRID_W
    period = 2 * w
    pad_l = (w - 1) - (WIN_COLS - 1)
    r_ext = jnp.pad(rpb.astype(F32), ((0, 0), (0, 0), (pad_l, period - nc - pad_l)))
    toe = jnp.tile(r_ext, (1, 1, w))[:, :, :w * (period - 1)].reshape(h, nd, w, period - 1)
    toe = toe[..., w - 1:2 * w - 1]
    cw = np.arange(w)
    col_start = np.clip(cw - WIN_COLS // 2, 0, w - WIN_COLS)
    col_ok = (cw[None, :] >= col_start[:, None]) & (cw[None, :] < col_start[:, None] + WIN_COLS)
    toe = jnp.where(col_ok[None, None], toe, NEG_INF)
    tq = jnp.transpose(toe, (0, 2, 1, 3)).reshape(h, w, nd * w)
    out = []
    for valid, dr in tables:
        qr, kr = valid.shape
        rows = []
        for i in range(qr):
            js = np.nonzero(valid[i])[0]
            j0, j1 = int(js[0]), int(js[-1]) + 1
            assert (np.diff(dr[i, j0:j1]) == 1).all() and valid[i, j0:j1].all()
            d0 = int(dr[i, j0])
            seg = tq[:, :, d0 * w:(d0 + j1 - j0) * w]
            rows.append(jnp.pad(seg, ((0, 0), (0, 0), (j0 * w, (kr - j1) * w)), constant_values=NEG_INF))
        out.append(jnp.stack(rows, axis=1).reshape(h, qr * w, kr * w))
    return jnp.stack(out, axis=0)


def _nbr_attn_kernel(kstart_ref, q_ref, k_ref, v_ref, ck_ref, cv_ref, bias_ref, o_ref, *, n_local):
    blk = pl.program_id(0)
    k0 = pl.multiple_of(kstart_ref[blk] * GRID_W, GRID_W)
    for hp in range(q_ref.shape[0]):
        kl = k_ref[hp, pl.ds(k0, n_local), :]
        vl = v_ref[hp, pl.ds(k0, n_local), :]
        biases = [[bias_ref[0, 2 * hp + head], None] for head in range(2)]
        o = _pair_attention(q_ref[hp], [kl, ck_ref[hp, 0]], [vl, cv_ref[hp, 0]], biases)
        o_ref[:, LANES * hp:LANES * (hp + 1)] = o.astype(BF16)


def _nbr_attention(q, k, v, ck, cv, rpb, batch, seq):
    n_pairs, n, _ = q.shape
    rows = seq // GRID_W
    nblk, n_key_rows, kstart, variant, tables = _nbr_geometry(rows)
    bias = _nbr_bias(rpb, tables)
    m = Q_ROWS * GRID_W
    n_local = n_key_rows * GRID_W
    past = ck.shape[2]
    sched = jnp.asarray(np.concatenate([kstart, variant]), I32)
    npr = ATTN_PAIRS
    grid_spec = pltpu.PrefetchScalarGridSpec(
        num_scalar_prefetch=1,
        grid=(nblk, n_pairs // npr, batch),
        in_specs=[
            pl.BlockSpec((npr, m, LANES), lambda r, hg, b, ks: (hg, b * nblk + r, 0)),
            pl.BlockSpec((npr, seq, LANES), lambda r, hg, b, ks: (hg, b, 0)),
            pl.BlockSpec((npr, seq, LANES), lambda r, hg, b, ks: (hg, b, 0)),
            pl.BlockSpec((npr, 1, past, LANES), lambda r, hg, b, ks: (hg, b, 0, 0)),
            pl.BlockSpec((npr, 1, past, LANES), lambda r, hg, b, ks: (hg, b, 0, 0)),
            pl.BlockSpec((1, 2 * npr, m, n_local), lambda r, hg, b, ks: (ks[nblk + r], hg, 0, 0)),
        ],
        out_specs=pl.BlockSpec((m, npr * LANES), lambda r, hg, b, ks: (b * nblk + r, hg)),
    )
    return pl.pallas_call(
        functools.partial(_nbr_attn_kernel, n_local=n_local),
        name="nbr_attention",
        grid_spec=grid_spec,
        out_shape=jax.ShapeDtypeStruct((n, n_pairs * LANES), BF16),
        compiler_params=_cparams(("arbitrary", "arbitrary", "arbitrary")),
    )(sched, q, k, v, ck, cv, bias)


def _attn_out_kernel(x_ref, o_ref, gt_ref, w_ref, y_ref):
    y_ref[...] = x_ref[...] + gt_ref[0, 0] * _dot(o_ref[...], w_ref[...])


def _attn_out(x2d, o, st, mod, w_bf16):
    n, d = x2d.shape
    tm = TOKEN_TILE
    tile = pl.BlockSpec((tm, d), lambda i: (i, 0))
    return pl.pallas_call(
        _attn_out_kernel,
        name="attn_out",
        grid=(n // tm,),
        in_specs=[tile, tile, mod.row_spec(0, 2, st.cond_row), pl.BlockSpec((d, d), lambda i: (0, 0))],
        out_specs=tile,
        out_shape=jax.ShapeDtypeStruct((n, d), F32),
        compiler_params=_cparams(("arbitrary",)),
    )(x2d, o, mod.t4, w_bf16)


def _lru_in_kernel(y_ref, sh_ref, sc_ref, g_ref, w_ref, gate_ref, xbr_ref, *, d_rnn, latent):
    bsz, tt, d = y_ref.shape
    x = jnp.concatenate([y_ref[:, t, :] for t in range(tt)], axis=0)
    if latent:
        x3 = x.reshape(tt, bsz, d)
        hn = _norm_mod(x3, g_ref[...][None], sc_ref[0][None], sh_ref[0][None]).reshape(tt * bsz, d)
    else:
        hn = _norm_mod(x, g_ref[...], sc_ref[0, 0], sh_ref[0, 0])
    hn = hn.astype(BF16)
    gate_ref[...] = _dot(hn, w_ref[:, :d_rnn])
    xbr_ref[...] = _dot(hn, w_ref[:, d_rnn:])


def _lru_in(y3, st, mod, g, w_bf16):
    bsz, seq, d = y3.shape
    d_rnn = w_bf16.shape[1] // 2
    tt = TOKEN_TILE // bsz
    assert (not st.latent) or bsz == SUBLANES
    mspec = (lambda c: mod.latent_spec(1, c)) if st.latent else (lambda c: mod.row_spec(1, c, lambda t: 0))
    out_spec = pl.BlockSpec((tt * bsz, d_rnn), lambda t: (t, 0))
    out_shape = jax.ShapeDtypeStruct((seq * bsz, d_rnn), F32)
    return pl.pallas_call(
        functools.partial(_lru_in_kernel, d_rnn=d_rnn, latent=st.latent),
        name="lru_in",
        grid=(seq // tt,),
        in_specs=[
            pl.BlockSpec((bsz, tt, d), lambda t: (0, t, 0)),
            mspec(0), mspec(1),
            pl.BlockSpec((1, d), lambda t: (0, 0)),
            pl.BlockSpec((d, 2 * d_rnn), lambda t: (0, 0)),
        ],
        out_specs=[out_spec, out_spec],
        out_shape=[out_shape, out_shape],
        compiler_params=_cparams(("arbitrary",)),
    )(y3, mod.t3 if st.latent else mod.t4, mod.t3 if st.latent else mod.t4, g, w_bf16)


def _scan_kernel(prev_ref, cur_ref, next_ref, cw_ref, cb_ref, wa_ref, wx_ref, ba_ref, bx_ref, lam_ref,
                 h0_ref, h_ref, hfin_ref, a_scr, b_scr, carry, *, batch, steps, nchunk):
    rev = pl.program_id(0) == 1
    c = pl.program_id(1)
    c_eff = jnp.where(rev, nchunk - 1 - c, c)
    rows = steps * batch
    cb = wa_ref.shape[2]
    for n in range(wa_ref.shape[1]):
        ch = slice(n * cb, (n + 1) * cb)
        prev = jnp.where(c_eff == 0, 0.0, prev_ref[:, ch])
        nxt = jnp.where(c_eff == nchunk - 1, 0.0, next_ref[:, ch])
        xfull = jnp.concatenate([prev, cur_ref[:, ch], nxt], axis=0)
        cw = cw_ref[:, ch]
        xc = cw[0:1, :] * xfull[0:rows]
        for j in range(1, cw.shape[0]):
            xc = xc + cw[j:j + 1, :] * xfull[j * batch:j * batch + rows]
        xc = xc + cb_ref[:, ch]
        xcb = xc.astype(BF16)
        t_r = jnp.tanh(0.5 * (_dot(xcb, wa_ref[0, n]) + ba_ref[0, :, ch]))
        t_i = jnp.tanh(0.5 * (_dot(xcb, wx_ref[0, n]) + bx_ref[0, :, ch]))
        neg_lam = -lam_ref[0, :, ch]
        softplus = jnp.maximum(neg_lam, 0.0) + jnp.log1p(jnp.exp(-jnp.abs(neg_lam)))
        log_a = ((-0.5 * LRU_C) * softplus) * (t_r + 1.0)
        a = jnp.exp(log_a)
        a_scr[:, ch] = a
        b_scr[:, ch] = jnp.sqrt(jnp.tanh(-log_a) * (1.0 + a * a)) * ((0.5 * xc) * (t_i + 1.0))

    @pl.when(c == 0)
    def _():
        carry[...] = h0_ref[0]

    def step(t, h):
        te = jnp.where(rev, steps - 1 - t, t)
        r0 = pl.multiple_of(te * batch, batch)
        h = a_scr[pl.ds(r0, batch), :] * h + b_scr[pl.ds(r0, batch), :]
        h_ref[0, pl.ds(r0, batch), :] = h
        return h

    h = lax.fori_loop(0, steps, step, carry[...], unroll=min(steps, 8))
    carry[...] = h

    @pl.when(c == nchunk - 1)
    def _():
        hfin_ref[0] = h


def _lru_scan(x2, batch, seq, conv_w, conv_b, wa_bf16, wx_bf16, ba, bx, lam, h0):
    d_rnn = conv_w.shape[1]
    nb, cb = wa_bf16.shape[1], wa_bf16.shape[2]
    assert batch % SUBLANES == 0 and conv_w.shape[0] == 4 and nb * cb == d_rnn
    steps = max(SCAN_ROWS // batch, 2)
    rows = steps * batch
    nchunk = seq // steps
    halo = 2 * batch
    hpc = rows // halo
    n_halo = seq * batch // halo
    ce = lambda d, c: jnp.where(d == 1, nchunk - 1 - c, c)
    vec = lambda a: a.reshape(2, 1, d_rnn)
    vspec = pl.BlockSpec((1, 1, d_rnn), lambda d, c: (d, 0, 0))
    wspec = pl.BlockSpec((1, nb, cb, cb), lambda d, c: (d, 0, 0, 0))
    return pl.pallas_call(
        functools.partial(_scan_kernel, batch=batch, steps=steps, nchunk=nchunk),
        name="lru_scan",
        grid=(2, nchunk),
        in_specs=[
            pl.BlockSpec((halo, d_rnn), lambda d, c: (jnp.maximum(ce(d, c) * hpc - 1, 0), 0)),
            pl.BlockSpec((rows, d_rnn), lambda d, c: (ce(d, c), 0)),
            pl.BlockSpec((halo, d_rnn), lambda d, c: (jnp.minimum((ce(d, c) + 1) * hpc, n_halo - 1), 0)),
            pl.BlockSpec((4, d_rnn), lambda d, c: (0, 0)),
            pl.BlockSpec((1, d_rnn), lambda d, c: (0, 0)),
            wspec, wspec, vspec, vspec, vspec,
            pl.BlockSpec((1, batch, d_rnn), lambda d, c: (d, 0, 0)),
        ],
        out_specs=[
            pl.BlockSpec((1, rows, d_rnn), lambda d, c: (d, ce(d, c), 0)),
            pl.BlockSpec((1, batch, d_rnn), lambda d, c: (d, 0, 0)),
        ],
        out_shape=[
            jax.ShapeDtypeStruct((2, seq * batch, d_rnn), F32),
            jax.ShapeDtypeStruct((2, batch, d_rnn), F32),
        ],
        scratch_shapes=[pltpu.VMEM((rows, d_rnn), F32), pltpu.VMEM((rows, d_rnn), F32),
                        pltpu.VMEM((batch, d_rnn), F32)],
        compiler_params=_cparams(("arbitrary", "arbitrary")),
    )(x2, x2, x2, conv_w, conv_b[None, :], wa_bf16, wx_bf16, vec(ba), vec(bx), vec(lam), h0)


def _lru_out_kernel(y_ref, gate_ref, h_ref, gt_ref, w_ref, o_ref, z_scr, *, latent):
    bsz, tt, d = y_ref.shape
    z = jax.nn.gelu(gate_ref[...]) * (h_ref[0] + h_ref[1])
    z_scr[...] = z.reshape(tt, bsz, z.shape[1])
    zb = jnp.concatenate([z_scr[:, b, :] for b in range(bsz)], axis=0)
    out = _dot(zb.astype(BF16), w_ref[...]).reshape(bsz, tt, d)
    gt = gt_ref[0][:, None, :] if latent else gt_ref[0]
    o_ref[...] = y_ref[...] + gt * out


def _lru_out(y3, gate, h, st, mod, w_bf16):
    bsz, seq, d = y3.shape
    d_rnn = w_bf16.shape[0]
    tt = TOKEN_TILE // bsz
    gspec = mod.latent_spec(1, 2) if st.latent else mod.row_spec(1, 2, lambda t: 0)
    return pl.pallas_call(
        functools.partial(_lru_out_kernel, latent=st.latent),
        name="lru_out",
        grid=(seq // tt,),
        in_specs=[
            pl.BlockSpec((bsz, tt, d), lambda t: (0, t, 0)),
            pl.BlockSpec((tt * bsz, d_rnn), lambda t: (t, 0)),
            pl.BlockSpec((2, tt * bsz, d_rnn), lambda t: (0, t, 0)),
            gspec,
            pl.BlockSpec((d_rnn, d), lambda t: (0, 0)),
        ],
        out_specs=pl.BlockSpec((bsz, tt, d), lambda t: (0, t, 0)),
        out_shape=jax.ShapeDtypeStruct((bsz, seq, d), F32),
        scratch_shapes=[pltpu.VMEM((tt, bsz, d_rnn), F32)],
        compiler_params=_cparams(("arbitrary",)),
    )(y3, gate, h, mod.t3 if st.latent else mod.t4, w_bf16)


def _dual_specs(shape, sp, ss, extra=0):
    pmap = lambda i, *_: (jnp.minimum(i, sp.tiles - 1), 0)
    smap = lambda i, *_: (jnp.maximum(i - sp.tiles, 0), 0)
    return pl.BlockSpec(shape, pmap), pl.BlockSpec(shape, smap)


def _joint_cond_row(sp, ss):
    return lambda i, *_: jnp.where(i < sp.tiles, 0, ss.cond_row(jnp.maximum(i - sp.tiles, 0)))


def _router_kernel(yp_ref, ys_ref, sh_ref, sc_ref, g_ref, rwt_ref, rb_ref, bkt_ref, rank_ref, wext_ref,
                   cnt_ref, carry, *, n_experts, ntp):
    i = pl.program_id(0)

    @pl.when(i == 0)
    def _():
        carry[...] = jnp.zeros_like(carry)

    y = jnp.where(i < ntp, yp_ref[...], ys_ref[...])
    hn = _norm_mod(y, g_ref[...], sc_ref[0, 0], sh_ref[0, 0])
    tm = hn.shape[0]
    logits = lax.dot_general(rwt_ref[...], hn, (((1,), (1,)), ((), ())),
                             precision=lax.Precision.HIGHEST, preferred_element_type=F32)
    scores = jax.nn.sigmoid(logits)
    sel = scores + rb_ref[...]
    per = n_experts // N_GROUPS
    assert per == 4
    srow = [sel[e:e + 1, :] for e in range(n_experts)]
    prow = [scores[e:e + 1, :] for e in range(n_experts)]

    def top2_sum(a, b, c, d):
        hi1, lo1, hi2, lo2 = jnp.maximum(a, b), jnp.minimum(a, b), jnp.maximum(c, d), jnp.minimum(c, d)
        return jnp.maximum(hi1, hi2) + jnp.maximum(jnp.minimum(hi1, hi2), jnp.maximum(lo1, lo2))

    gscore = [top2_sum(*srow[per * g:per * (g + 1)]) for g in range(N_GROUPS)]
    best = jnp.zeros((1, tm), I32)
    best_v = gscore[0]
    for g in range(1, N_GROUPS):
        upd = gscore[g] > best_v
        best = jnp.where(upd, g, best)
        best_v = jnp.where(upd, gscore[g], best_v)
    vs = list(srow[:per])
    ps = list(prow[:per])
    for g in range(1, N_GROUPS):
        for j in range(per):
            vs[j] = jnp.where(best == g, srow[per * g + j], vs[j])
            ps[j] = jnp.where(best == g, prow[per * g + j], ps[j])
    j1 = jnp.zeros((1, tm), I32)
    m1 = vs[0]
    for j in range(1, per):
        upd = vs[j] > m1
        j1 = jnp.where(upd, j, j1)
        m1 = jnp.where(upd, vs[j], m1)
    j2 = jnp.full((1, tm), -1, I32)
    m2 = jnp.full((1, tm), -jnp.inf, F32)
    for j in range(per):
        upd = (j1 != j) & (vs[j] > m2)
        j2 = jnp.where(upd, j, j2)
        m2 = jnp.where(upd, vs[j], m2)
    s1 = jnp.zeros((1, tm), F32)
    s2 = jnp.zeros((1, tm), F32)
    for j in range(per):
        s1 = jnp.where(j1 == j, ps[j], s1)
        s2 = jnp.where(j2 == j, ps[j], s2)
    den = s1 + s2
    w1, w2 = s1 / den, s2 / den
    lo = jnp.minimum(j1, j2)
    hi = jnp.maximum(j1, j2)
    w_lo = jnp.where(j1 < j2, w1, w2)
    w_hi = jnp.where(j1 < j2, w2, w1)
    base = jnp.where(lo == 0, 0, jnp.where(lo == 1, 3, 5))
    bucket = best * 6 + base + hi - lo - 1

    onehot = (lax.broadcasted_iota(I32, (N_BUCKET_SLOTS, tm), 0) == bucket).astype(F32)
    before = (lax.broadcasted_iota(I32, (tm, tm), 0) < lax.broadcasted_iota(I32, (tm, tm), 1)).astype(BF16)
    prefix = _dot(onehot.astype(BF16), before) + carry[...]
    rank = jnp.sum(onehot * prefix, axis=0, keepdims=True)
    carry[...] = carry[...] + jnp.sum(onehot, axis=1, keepdims=True)

    bkt_ref[0] = bucket
    rank_ref[0] = rank.astype(I32)
    slab_row = lax.broadcasted_iota(I32, (LANES, tm), 0)
    slab = jnp.where(slab_row == 0, w_lo, jnp.where(slab_row == 1, w_hi, 0.0))
    wext_ref[...] = slab.T

    @pl.when(i == pl.num_programs(0) - 1)
    def _():
        cnt_ref[...] = jnp.broadcast_to(carry[...], cnt_ref.shape).astype(I32)


def _router(yp, ys, sp, ss, mod, layer, g, router_wt, router_b):
    d = yp.shape[1]
    e = router_wt.shape[0]
    tm = TOKEN_TILE
    nt = sp.tiles + ss.tiles
    row = _joint_cond_row(sp, ss)
    tok_spec = pl.BlockSpec((1, 1, tm), lambda i: (i, 0, 0))
    pspec, sspec = _dual_specs((tm, d), sp, ss)
    return pl.pallas_call(
        functools.partial(_router_kernel, n_experts=e, ntp=sp.tiles),
        name="moe_router",
        grid=(nt,),
        in_specs=[
            pspec, sspec,
            mod.row_spec(layer, 3, row),
            mod.row_spec(layer, 4, row),
            pl.BlockSpec((1, d), lambda i: (0, 0)),
            pl.BlockSpec((e, d), lambda i: (0, 0)),
            pl.BlockSpec((e, 1), lambda i: (0, 0)),
        ],
        out_specs=[tok_spec, tok_spec,
                   pl.BlockSpec((tm, LANES), lambda i: (i, 0)),
                   pl.BlockSpec((N_BUCKET_SLOTS, LANES), lambda i: (0, 0))],
        out_shape=[jax.ShapeDtypeStruct((nt, 1, tm), I32), jax.ShapeDtypeStruct((nt, 1, tm), I32),
                   jax.ShapeDtypeStruct((nt * tm, LANES), F32),
                   jax.ShapeDtypeStruct((N_BUCKET_SLOTS, LANES), I32)],
        scratch_shapes=[pltpu.VMEM((N_BUCKET_SLOTS, 1), F32)],
        compiler_params=_cparams(("arbitrary",)),
    )(yp, ys, mod.t4, mod.t4, g, router_wt, router_b[:, None])


def _bucket_layout(counts, n_tokens):
    n_buckets = N_GROUPS * len(PAIR_LO)
    tf = FFN_TILE
    n_tiles = n_tokens // tf + n_buckets
    c = counts[:n_buckets]
    padded = ((c + tf - 1) // tf) * tf
    ends = jnp.cumsum(padded)
    starts = ends - padded
    n_used = ends[-1] // tf
    tile = jnp.arange(n_tiles, dtype=I32)
    valid = tile < n_used
    tile_c = jnp.minimum(tile, n_used - 1)
    tb = jnp.sum((ends[None, :] <= (tile_c * tf)[:, None]).astype(I32), axis=1)
    tb = jnp.minimum(tb, n_buckets - 1)
    grp, pair = tb // len(PAIR_LO), tb % len(PAIR_LO)
    pair_lo = sum(jnp.where(pair == p, PAIR_LO[p], 0) for p in range(len(PAIR_LO)))
    pair_hi = sum(jnp.where(pair == p, PAIR_HI[p], 0) for p in range(len(PAIR_HI)))
    starts32 = jnp.concatenate([starts.astype(I32), jnp.zeros((N_BUCKET_SLOTS - n_buckets,), I32)])
    sched = jnp.stack([tile_c, grp * 4 + pair_lo, grp * 4 + pair_hi, valid.astype(I32)]).astype(I32)
    return starts32, sched, n_tiles


def _pos_kernel(starts_ref, bkt_ref, rank_ref, pos_ref):
    bkt = bkt_ref[...]
    pos = rank_ref[...]
    for b in range(N_GROUPS * len(PAIR_LO)):
        pos = pos + jnp.where(bkt == b, starts_ref[b], 0)
    pos_ref[...] = pos


def _positions(starts32, bkt, rank):
    nt, _, tm = bkt.shape
    grid_spec = pltpu.PrefetchScalarGridSpec(
        num_scalar_prefetch=1,
        grid=(1,),
        in_specs=[pl.BlockSpec((nt, 1, tm), lambda i, st: (0, 0, 0))] * 2,
        out_specs=pl.BlockSpec((nt, 1, tm), lambda i, st: (0, 0, 0)),
    )
    return pl.pallas_call(
        _pos_kernel, name="moe_positions", grid_spec=grid_spec,
        out_shape=jax.ShapeDtypeStruct((nt, 1, tm), I32),
        compiler_params=_cparams(("arbitrary",)),
    )(starts32, bkt, rank)


def _dispatch_kernel(yp_ref, ys_ref, sh_ref, sc_ref, g_ref, wext_ref, pos_ref, init_ref,
                     xs_ref, buf, sem, *, d, ntp):
    del init_ref
    tm = buf.shape[0]
    y = jnp.where(pl.program_id(0) < ntp, yp_ref[...], ys_ref[...])
    buf[:, :d] = _norm_mod(y, g_ref[...], sc_ref[0, 0], sh_ref[0, 0])
    buf[:, d:] = wext_ref[...]

    def issue(r, carry):
        pltpu.make_async_copy(buf.at[pl.ds(r, 1)], xs_ref.at[pl.ds(pos_ref[0, 0, r], 1)], sem).start()
        return carry

    lax.fori_loop(0, tm, issue, 0, unroll=8)
    pltpu.make_async_copy(buf, xs_ref.at[pl.ds(0, tm)], sem).wait()


def _dispatch(yp, ys, sp, ss, mod, layer, g, wext, pos, n_rows):
    d = yp.shape[1]
    tm = TOKEN_TILE
    width = d + LANES
    row = _joint_cond_row(sp, ss)
    pspec, sspec = _dual_specs((tm, d), sp, ss)
    return pl.pallas_call(
        functools.partial(_dispatch_kernel, d=d, ntp=sp.tiles),
        name="moe_dispatch",
        grid=(sp.tiles + ss.tiles,),
        in_specs=[
            pspec, sspec,
            mod.row_spec(layer, 3, row), mod.row_spec(layer, 4, row),
            pl.BlockSpec((1, d), lambda i: (0, 0)),
            pl.BlockSpec((tm, LANES), lambda i: (i, 0)),
            pl.BlockSpec((1, 1, tm), lambda i: (i, 0, 0), memory_space=pltpu.SMEM),
            pl.BlockSpec(memory_space=pl.ANY),
        ],
        out_specs=pl.BlockSpec(memory_space=pl.ANY),
        out_shape=jax.ShapeDtypeStruct((n_rows, width), F32),
        scratch_shapes=[pltpu.VMEM((tm, width), F32), pltpu.SemaphoreType.DMA(())],
        input_output_aliases={7: 0},
        compiler_params=_cparams(("arbitrary",), disable_bounds_checks=True),
    )(yp, ys, mod.t4, mod.t4, g, wext, pos, jnp.zeros((n_rows, width), F32))


def _ffn_kernel(sched_ref, xs_ref, wg_lo, wu_lo, wd_lo, wg_hi, wu_hi, wd_hi, o_ref, *, d):
    valid = sched_ref[3, pl.program_id(0)] == 1

    @pl.when(jnp.logical_not(valid))
    def _():
        o_ref[...] = jnp.zeros_like(o_ref)

    @pl.when(valid)
    def _():
        x = xs_ref[:, :d].astype(BF16)

        def expert(wg, wu, wd):
            gate = _dot(x, wg[0, 0])
            he = (gate * jax.nn.sigmoid(gate)) * _dot(x, wu[0, 0])
            return _dot(he.astype(BF16), wd[0, 0])

        w_lo = xs_ref[:, d:d + 1]
        w_hi = xs_ref[:, d + 1:d + 2]
        o_ref[...] = w_lo * expert(wg_lo, wu_lo, wd_lo) + w_hi * expert(wg_hi, wu_hi, wd_hi)


def _expert_ffn(xs, sched, n_tiles, layer, wg, wu, wd):
    n_rows, width = xs.shape
    d = width - LANES
    f = wg.shape[3]
    tf = FFN_TILE
    lo = lambda i, s: (layer, s[1, i], 0, 0)
    hi = lambda i, s: (layer, s[2, i], 0, 0)
    grid_spec = pltpu.PrefetchScalarGridSpec(
        num_scalar_prefetch=1,
        grid=(n_tiles,),
        in_specs=[
            pl.BlockSpec((tf, width), lambda i, s: (s[0, i], 0)),
            pl.BlockSpec((1, 1, d, f), lo), pl.BlockSpec((1, 1, d, f), lo), pl.BlockSpec((1, 1, f, d), lo),
            pl.BlockSpec((1, 1, d, f), hi), pl.BlockSpec((1, 1, d, f), hi), pl.BlockSpec((1, 1, f, d), hi),
        ],
        out_specs=pl.BlockSpec((tf, d), lambda i, s: (i, 0)),
    )
    return pl.pallas_call(
        functools.partial(_ffn_kernel, d=d),
        name="moe_expert_ffn",
        grid_spec=grid_spec,
        out_shape=jax.ShapeDtypeStruct((n_rows, d), F32),
        compiler_params=_cparams(("arbitrary",)),
    )(sched, xs, wg, wu, wd, wg, wu, wd)


def _combine_kernel(y_ref, gt_ref, pos_ref, ms_ref, o_ref, buf, sem):
    tm = buf.shape[0]

    def issue(r, carry):
        pltpu.make_async_copy(ms_ref.at[pl.ds(pos_ref[0, 0, r], 1)], buf.at[pl.ds(r, 1)], sem).start()
        return carry

    lax.fori_loop(0, tm, issue, 0, unroll=8)
    pltpu.make_async_copy(ms_ref.at[pl.ds(0, tm)], buf, sem).wait()
    o_ref[...] = y_ref[...] + gt_ref[0, 0] * buf[...]


def _combine(y2d, st, mod, layer, pos, ms):
    n, d = y2d.shape
    tm = TOKEN_TILE
    tile = pl.BlockSpec((tm, d), lambda i: (i, 0))
    return pl.pallas_call(
        _combine_kernel,
        name="moe_combine",
        grid=(n // tm,),
        in_specs=[
            tile,
            mod.row_spec(layer, 5, st.cond_row),
            pl.BlockSpec((1, 1, tm), lambda i: (st.tile_off + i, 0, 0), memory_space=pltpu.SMEM),
            pl.BlockSpec(memory_space=pl.ANY),
        ],
        out_specs=tile,
        out_shape=jax.ShapeDtypeStruct((n, d), F32),
        scratch_shapes=[pltpu.VMEM((tm, d), F32), pltpu.SemaphoreType.DMA(())],
        compiler_params=_cparams(("arbitrary",), disable_bounds_checks=True),
    )(y2d, mod.t4, pos, ms)


def _moe_residual(yp, ys, sp, ss, mod, layer, g, router_wt, router_b, wg, wu, wd):
    bkt, rank, wext, counts = _router(yp, ys, sp, ss, mod, layer, g, router_wt, router_b)
    starts32, sched, n_tiles = _bucket_layout(counts[:, 0], sp.n + ss.n)
    pos = _positions(starts32, bkt, rank)
    xs = _dispatch(yp, ys, sp, ss, mod, layer, g, wext, pos, n_tiles * FFN_TILE)
    ms = _expert_ffn(xs, sched, n_tiles, layer, wg, wu, wd)
    return _combine(yp, sp, mod, layer, pos, ms), _combine(ys, ss, mod, layer, pos, ms)


def kernel(x_prompt, x_sample, c, cache_k, cache_v, state_lru, c_ctx, ada_w, ada_b, norm_mix, norm_ffn,
           attn_w_in, attn_q_gain, attn_k_gain, attn_rpb, attn_w_out, lru_w_in, lru_conv_w, lru_conv_b,
           lru_gate_a_w, lru_gate_a_b, lru_gate_x_w, lru_gate_x_b, lru_lambda, lru_w_out, router_w,
           router_b, moe_w_gate, moe_w_up, moe_w_down):
    bp, tp, d = x_prompt.shape
    bs, ts, _ = x_sample.shape
    n_heads, head_dim = cache_k.shape[3], cache_k.shape[4]
    d_rnn = lru_conv_w.shape[2]
    assert ada_w.shape[0] == 2 and bs <= N_COND_ROWS - LATENT_COND_ROW
    assert tp % TOKEN_TILE == 0 and ts % TOKEN_TILE == 0
    sp = _Stream(bp, tp, 0, latent=False)
    ss = _Stream(bs, ts, sp.tiles, latent=True)

    cond = jnp.zeros((N_COND_ROWS, d), F32).at[0].set(c_ctx).at[LATENT_COND_ROW:LATENT_COND_ROW + bs].set(c)
    mod = _Mod(_ada_modulation(cond, ada_w, ada_b), d)

    w_in = attn_w_in[0].astype(BF16)
    w_out = attn_w_out[0].astype(BF16)
    g_mix0 = norm_mix[0][None, :]
    xp2, xs2 = x_prompt.reshape(sp.n, d), x_sample.reshape(ss.n, d)
    q_p, k_p, v_p, new_k, new_v = _qkv_proj(xp2, sp, mod, g_mix0, w_in, attn_q_gain[0], attn_k_gain[0],
                                            n_heads, True)
    q_s, k_s, v_s = _qkv_proj(xs2, ss, mod, g_mix0, w_in, attn_q_gain[0], attn_k_gain[0], n_heads, False)
    o_p = _ctx_attention(q_p, k_p, v_p, bp, tp)
    pair_major = lambda cache: jnp.transpose(
        cache[:, 0].astype(BF16).reshape(bs, cache.shape[2], n_heads // 2, LANES), (2, 0, 1, 3))
    ck, cv = pair_major(cache_k), pair_major(cache_v)
    o_s = _nbr_attention(q_s, k_s, v_s, ck, cv, attn_rpb[0], bs, ts)
    yp = _attn_out(xp2, o_p, sp, mod, w_out)
    ys = _attn_out(xs2, o_s, ss, mod, w_out)

    router_wt = router_w.T
    moe_w = (moe_w_gate.astype(BF16), moe_w_up.astype(BF16), moe_w_down.astype(BF16))
    yp, ys = _moe_residual(yp, ys, sp, ss, mod, 0, norm_ffn[0][None, :], router_wt, router_b, *moe_w)

    w_lru_in = lru_w_in[0].astype(BF16)
    w_lru_out = lru_w_out[0].astype(BF16)
    g_mix1 = norm_mix[1][None, :]
    scan_args = (lru_conv_w[0], lru_conv_b[0], lru_gate_a_w[0].astype(BF16), lru_gate_x_w[0].astype(BF16),
                 lru_gate_a_b[0], lru_gate_x_b[0], lru_lambda[0])
    hfin = {}
    ys_out = []
    for y2, st, h0 in ((yp, sp, jnp.zeros((2, bp, d_rnn), F32)),
                       (ys, ss, jnp.transpose(state_lru[:, 0], (1, 0, 2)))):
        y3 = y2.reshape(st.batch, st.seq, d)
        gate, xbr = _lru_in(y3, st, mod, g_mix1, w_lru_in)
        h, hfin[st.latent] = _lru_scan(xbr, st.batch, st.seq, *scan_args, h0)
        ys_out.append(_lru_out(y3, gate, h, st, mod, w_lru_out).reshape(st.n, d))
    yp, ys = ys_out
    yp, ys = _moe_residual(yp, ys, sp, ss, mod, 1, norm_ffn[1][None, :], router_wt, router_b, *moe_w)

    return (yp.reshape(bp, tp, d), ys.reshape(bs, ts, d),
            new_k.reshape(bp, 1, tp, n_heads, head_dim), new_v.reshape(bp, 1, tp, n_heads, head_dim),
            jnp.transpose(hfin[False], (1, 0, 2))[:, None])
```

```python
import functools

import numpy as np
import jax
import jax.numpy as jnp
from jax import lax
from jax.experimental import pallas as pl
from jax.experimental.pallas import tpu as pltpu

F32 = jnp.float32
BF16 = jnp.bfloat16
I32 = jnp.int32

EPS = 1e-6
NEG_INF = -1e30
GRID_W = 64
WIN_ROWS = 8
WIN_COLS = 16
N_GROUPS = 4
LRU_C = 8.0
N_LRU_BLOCKS = 4

LANES = 128
SUBLANES = 8
MXU_COLS = 256
TOKEN_TILE = 256
QKV_TILE = 512
FFN_TILE = 256
Q_ROWS = 4
SCAN_ROWS = 512
ATTN_PAIRS = 2
N_BUCKET_SLOTS = 32
PAIR_LO = (0, 0, 0, 1, 1, 2)
PAIR_HI = (1, 2, 3, 2, 3, 3)
N_COND_ROWS = 16
LATENT_COND_ROW = 8
VMEM_LIMIT = 48 * 1024 * 1024


def _cparams(sem, **kw):
    return pltpu.CompilerParams(dimension_semantics=sem, vmem_limit_bytes=VMEM_LIMIT, **kw)


def _norm_mod(x, g, sc, sh):
    ms = jnp.mean(x * x, axis=-1, keepdims=True)
    return (x * lax.rsqrt(ms + EPS)) * g * (1.0 + sc) + sh


def _dot(a, b):
    return jnp.dot(a, b, preferred_element_type=F32)


def _dot_nt(a, b):
    return lax.dot_general(a, b, (((1,), (1,)), ((), ())), preferred_element_type=F32)


def _ada_kernel(cond_ref, w_ref, b_ref, o_ref):
    c = cond_ref[...]
    s = (c * jax.nn.sigmoid(c)).astype(BF16)
    o_ref[0] = _dot(s, w_ref[0].astype(BF16)) + b_ref[0]


def _ada_modulation(cond, ada_w, ada_b):
    depth, d, d6 = ada_w.shape
    tn = d6 // 4
    return pl.pallas_call(
        _ada_kernel,
        name="ada_modulation",
        grid=(depth, d6 // tn),
        in_specs=[
            pl.BlockSpec((N_COND_ROWS, d), lambda l, j: (0, 0)),
            pl.BlockSpec((1, d, tn), lambda l, j: (l, 0, j)),
            pl.BlockSpec((1, 1, tn), lambda l, j: (l, 0, j)),
        ],
        out_specs=pl.BlockSpec((1, N_COND_ROWS, tn), lambda l, j: (l, 0, j)),
        out_shape=jax.ShapeDtypeStruct((depth, N_COND_ROWS, d6), F32),
        compiler_params=_cparams(("arbitrary", "arbitrary")),
    )(cond, ada_w, ada_b.reshape(depth, 1, d6))


class _Mod:
    def __init__(self, table, d):
        self.t3 = table
        self.t4 = table.reshape(table.shape[0], N_COND_ROWS, 1, table.shape[2])
        self.d = d

    def row_spec(self, layer, chunk, row_fn):
        return pl.BlockSpec((1, 1, 1, self.d), lambda *ids: (layer, row_fn(*ids), 0, chunk))

    def latent_spec(self, layer, chunk):
        return pl.BlockSpec((1, SUBLANES, self.d), lambda *ids: (layer, LATENT_COND_ROW // SUBLANES, chunk))


class _Stream:
    def __init__(self, batch, seq, tile_off, latent):
        self.batch, self.seq, self.tile_off, self.latent = batch, seq, tile_off, latent
        self.n = batch * seq
        self.tiles = self.n // TOKEN_TILE

    def cond_row(self, i, tile=TOKEN_TILE):
        if not self.latent:
            return 0
        return LATENT_COND_ROW + (i * tile) // self.seq


def _qkv_kernel(x_ref, sh_ref, sc_ref, g_ref, w_ref, qg_ref, kg_ref, gmean_ref, q_ref, k_ref, v_ref, *kv_refs,
                d, head_dim):
    hn = _norm_mod(x_ref[...], g_ref[...], sc_ref[0, 0], sh_ref[0, 0]).astype(BF16)
    outs = (q_ref, k_ref, v_ref)
    gains = (qg_ref, kg_ref)
    n_chunks = 3 * d // MXU_COLS
    n_norm = 2 * d // MXU_COLS
    accs = [_dot(hn, w_ref[:, MXU_COLS * j:MXU_COLS * (j + 1)]) for j in range(n_chunks)]
    means = []
    for j in range(n_norm):
        sq = accs[j] * accs[j]
        hi = sq.astype(BF16)
        lo = (sq - hi.astype(F32)).astype(BF16)
        means.append(_dot(hi, gmean_ref[...]) + _dot(lo, gmean_ref[...]))
    for j in range(n_chunks):
        part = (MXU_COLS * j) // d
        col = (MXU_COLS * j) % d
        acc = accs[j]
        if j < n_norm:
            acc = (acc * lax.rsqrt(means[j] + EPS)) * gains[part][...]
        for half in range(MXU_COLS // LANES):
            outs[part][col // LANES + half] = acc[:, LANES * half:LANES * (half + 1)].astype(BF16)
        if part >= 1 and kv_refs:
            kv_refs[part - 1][:, col:col + MXU_COLS] = acc


def _qkv_proj(x2d, st, mod, g, w_bf16, q_gain, k_gain, n_heads, with_kv):
    n, d = x2d.shape
    head_dim = d // n_heads
    assert head_dim * 2 == LANES
    tm = QKV_TILE
    n_pairs = n_heads // 2
    hspec = pl.BlockSpec((n_pairs, tm, LANES), lambda i: (0, i, 0))
    hshape = jax.ShapeDtypeStruct((n_pairs, n, LANES), BF16)
    out_specs = [hspec, hspec, hspec]
    out_shape = [hshape, hshape, hshape]
    if with_kv:
        out_specs += [pl.BlockSpec((tm, d), lambda i: (i, 0))] * 2
        out_shape += [jax.ShapeDtypeStruct((n, d), F32)] * 2
    lane_head = np.arange(MXU_COLS) // head_dim
    assert float(np.log2(head_dim)).is_integer()
    gmean = jnp.asarray((lane_head[:, None] == lane_head[None, :]) / head_dim, BF16)
    reps = MXU_COLS // head_dim
    qg = jnp.tile(q_gain * _attn_scale(head_dim), reps)[None, :]
    kg = jnp.tile(k_gain, reps)[None, :]
    const = lambda shape: pl.BlockSpec(shape, lambda i: (0,) * len(shape))
    return pl.pallas_call(
        functools.partial(_qkv_kernel, d=d, head_dim=head_dim),
        name="qkv_proj",
        grid=(n // tm,),
        in_specs=[
            pl.BlockSpec((tm, d), lambda i: (i, 0)),
            mod.row_spec(0, 0, lambda i: st.cond_row(i, tm)),
            mod.row_spec(0, 1, lambda i: st.cond_row(i, tm)),
            const((1, d)), const((d, 3 * d)), const((1, MXU_COLS)), const((1, MXU_COLS)),
            const((MXU_COLS, MXU_COLS)),
        ],
        out_specs=out_specs,
        out_shape=out_shape,
        compiler_params=_cparams(("arbitrary",)),
    )(x2d, mod.t4, mod.t4, g, w_bf16, qg, kg, gmean)


def _attn_scale(head_dim):
    scale = head_dim ** -0.5
    assert float(np.log2(scale)).is_integer()
    return scale


def _softmax_pv(scores, values):
    m = scores[0].max(axis=-1, keepdims=True)
    for s in scores[1:]:
        m = jnp.maximum(m, s.max(axis=-1, keepdims=True))
    es = [jnp.exp(s - m) for s in scores]
    l = es[0].sum(axis=-1, keepdims=True)
    for e in es[1:]:
        l = l + e.sum(axis=-1, keepdims=True)
    o = _dot(es[0].astype(BF16), values[0])
    for e, v in zip(es[1:], values[1:]):
        o = o + _dot(e.astype(BF16), v)
    return o / l


def _pair_attention(q2, score_keys, values, biases):
    first = lax.broadcasted_iota(I32, q2.shape, 1) < q2.shape[1] // 2
    zero = jnp.zeros_like(q2)
    outs = []
    for head in range(2):
        qh = jnp.where(first, q2, zero) if head == 0 else jnp.where(first, zero, q2)
        scores = []
        for keys, bias in zip(score_keys, biases[head]):
            s = _dot_nt(qh, keys)
            scores.append(s if bias is None else s + bias)
        outs.append(_softmax_pv(scores, values))
    return jnp.where(first, outs[0], outs[1])


def _ctx_attn_kernel(q_ref, k_ref, v_ref, o_ref):
    for hp in range(q_ref.shape[0]):
        o = _pair_attention(q_ref[hp], [k_ref[hp]], [v_ref[hp]], [[None], [None]])
        o_ref[:, LANES * hp:LANES * (hp + 1)] = o.astype(BF16)


def _ctx_attention(q, k, v, batch, seq):
    n_pairs, n, _ = q.shape
    spec = pl.BlockSpec((n_pairs, seq, LANES), lambda b: (0, b, 0))
    return pl.pallas_call(
        _ctx_attn_kernel,
        name="ctx_attention",
        grid=(batch,),
        in_specs=[spec, spec, spec],
        out_specs=pl.BlockSpec((seq, n_pairs * LANES), lambda b: (b, 0)),
        out_shape=jax.ShapeDtypeStruct((n, n_pairs * LANES), BF16),
        compiler_params=_cparams(("arbitrary",)),
    )(q, k, v)


def _nbr_geometry(rows):
    n_key_rows = Q_ROWS + WIN_ROWS
    assert rows % Q_ROWS == 0 and rows >= n_key_rows
    nblk = rows // Q_ROWS
    kstart = np.clip(np.arange(nblk) * Q_ROWS - WIN_ROWS // 2, 0, rows - n_key_rows)
    start = np.clip(np.arange(rows) - WIN_ROWS // 2, 0, rows - WIN_ROWS)
    tables, variant = [], []
    for blk in range(nblk):
        r = blk * Q_ROWS + np.arange(Q_ROWS)
        kr = kstart[blk] + np.arange(n_key_rows)
        valid = (kr[None, :] >= start[r][:, None]) & (kr[None, :] < start[r][:, None] + WIN_ROWS)
        dr = np.clip(kr[None, :] - r[:, None] + WIN_ROWS - 1, 0, 2 * WIN_ROWS - 2)
        key = (valid.tobytes(), dr.tobytes())
        keys = [t[0] for t in tables]
        if key not in keys:
            tables.append((key, valid, dr))
        variant.append([t[0] for t in tables].index(key))
    return nblk, n_key_rows, kstart, np.array(variant), [(t[1], t[2]) for t in tables]


def _nbr_bias(rpb, tables):
    h, nd, nc = rpb.shape
    w = GRID_W
    period = 2 * w
    pad_l = (w - 1) - (WIN_COLS - 1)
    r_ext = jnp.pad(rpb.astype(F32), ((0, 0), (0, 0), (pad_l, period - nc - pad_l)))
    toe = jnp.tile(r_ext, (1, 1, w))[:, :, :w * (period - 1)].reshape(h, nd, w, period - 1)
    toe = toe[..., w - 1:2 * w - 1]
    cw = np.arange(w)
    col_start = np.clip(cw - WIN_COLS // 2, 0, w - WIN_COLS)
    col_ok = (cw[None, :] >= col_start[:, None]) & (cw[None, :] < col_start[:, None] + WIN_COLS)
    toe = jnp.where(col_ok[None, None], toe, NEG_INF)
    tq = jnp.transpose(toe, (0, 2, 1, 3)).reshape(h, w, nd * w)
    out = []
    for valid, dr in tables:
        qr, kr = valid.shape
        rows = []
        for i in range(qr):
            js = np.nonzero(valid[i])[0]
            j0, j1 = int(js[0]), int(js[-1]) + 1
            assert (np.diff(dr[i, j0:j1]) == 1).all() and valid[i, j0:j1].all()
            d0 = int(dr[i, j0])
            seg = tq[:, :, d0 * w:(d0 + j1 - j0) * w]
            rows.append(jnp.pad(seg, ((0, 0), (0, 0), (j0 * w, (kr - j1) * w)), constant_values=NEG_INF))
        out.append(jnp.stack(rows, axis=1).reshape(h, qr * w, kr * w))
    return jnp.stack(out, axis=0)


def _nbr_attn_kernel(kstart_ref, q_ref, k_ref, v_ref, ck_ref, cv_ref, bias_ref, o_ref, *, n_local):
    blk = pl.program_id(0)
    k0 = pl.multiple_of(kstart_ref[blk] * GRID_W, GRID_W)
    for hp in range(q_ref.shape[0]):
        kl = k_ref[hp, pl.ds(k0, n_local), :]
        vl = v_ref[hp, pl.ds(k0, n_local), :]
        biases = [[bias_ref[0, 2 * hp + head], None] for head in range(2)]
        o = _pair_attention(q_ref[hp], [kl, ck_ref[hp, 0]], [vl, cv_ref[hp, 0]], biases)
        o_ref[:, LANES * hp:LANES * (hp + 1)] = o.astype(BF16)


def _nbr_attention(q, k, v, ck, cv, rpb, batch, seq):
    n_pairs, n, _ = q.shape
    rows = seq // GRID_W
    nblk, n_key_rows, kstart, variant, tables = _nbr_geometry(rows)
    bias = _nbr_bias(rpb, tables)
    m = Q_ROWS * GRID_W
    n_local = n_key_rows * GRID_W
    past = ck.shape[2]
    sched = jnp.asarray(np.concatenate([kstart, variant]), I32)
    npr = ATTN_PAIRS
    grid_spec = pltpu.PrefetchScalarGridSpec(
        num_scalar_prefetch=1,
        grid=(nblk, n_pairs // npr, batch),
        in_specs=[
            pl.BlockSpec((npr, m, LANES), lambda r, hg, b, ks: (hg, b * nblk + r, 0)),
            pl.BlockSpec((npr, seq, LANES), lambda r, hg, b, ks: (hg, b, 0)),
            pl.BlockSpec((npr, seq, LANES), lambda r, hg, b, ks: (hg, b, 0)),
            pl.BlockSpec((npr, 1, past, LANES), lambda r, hg, b, ks: (hg, b, 0, 0)),
            pl.BlockSpec((npr, 1, past, LANES), lambda r, hg, b, ks: (hg, b, 0, 0)),
            pl.BlockSpec((1, 2 * npr, m, n_local), lambda r, hg, b, ks: (ks[nblk + r], hg, 0, 0)),
        ],
        out_specs=pl.BlockSpec((m, npr * LANES), lambda r, hg, b, ks: (b * nblk + r, hg)),
    )
    return pl.pallas_call(
        functools.partial(_nbr_attn_kernel, n_local=n_local),
        name="nbr_attention",
        grid_spec=grid_spec,
        out_shape=jax.ShapeDtypeStruct((n, n_pairs * LANES), BF16),
        compiler_params=_cparams(("arbitrary", "arbitrary", "arbitrary")),
    )(sched, q, k, v, ck, cv, bias)


def _attn_out_kernel(x_ref, o_ref, gt_ref, w_ref, y_ref):
    y_ref[...] = x_ref[...] + gt_ref[0, 0] * _dot(o_ref[...], w_ref[...])


def _attn_out(x2d, o, st, mod, w_bf16):
    n, d = x2d.shape
    tm = TOKEN_TILE
    tile = pl.BlockSpec((tm, d), lambda i: (i, 0))
    return pl.pallas_call(
        _attn_out_kernel,
        name="attn_out",
        grid=(n // tm,),
        in_specs=[tile, tile, mod.row_spec(0, 2, st.cond_row), pl.BlockSpec((d, d), lambda i: (0, 0))],
        out_specs=tile,
        out_shape=jax.ShapeDtypeStruct((n, d), F32),
        compiler_params=_cparams(("arbitrary",)),
    )(x2d, o, mod.t4, w_bf16)


def _lru_in_kernel(y_ref, sh_ref, sc_ref, g_ref, w_ref, gate_ref, xbr_ref, *, d_rnn, latent):
    bsz, tt, d = y_ref.shape
    x = jnp.concatenate([y_ref[:, t, :] for t in range(tt)], axis=0)
    if latent:
        x3 = x.reshape(tt, bsz, d)
        hn = _norm_mod(x3, g_ref[...][None], sc_ref[0][None], sh_ref[0][None]).reshape(tt * bsz, d)
    else:
        hn = _norm_mod(x, g_ref[...], sc_ref[0, 0], sh_ref[0, 0])
    hn = hn.astype(BF16)
    gate_ref[...] = _dot(hn, w_ref[:, :d_rnn])
    xbr_ref[...] = _dot(hn, w_ref[:, d_rnn:])


def _lru_in(y3, st, mod, g, w_bf16):
    bsz, seq, d = y3.shape
    d_rnn = w_bf16.shape[1] // 2
    tt = TOKEN_TILE // bsz
    assert (not st.latent) or bsz == SUBLANES
    mspec = (lambda c: mod.latent_spec(1, c)) if st.latent else (lambda c: mod.row_spec(1, c, lambda t: 0))
    out_spec = pl.BlockSpec((tt * bsz, d_rnn), lambda t: (t, 0))
    out_shape = jax.ShapeDtypeStruct((seq * bsz, d_rnn), F32)
    return pl.pallas_call(
        functools.partial(_lru_in_kernel, d_rnn=d_rnn, latent=st.latent),
        name="lru_in",
        grid=(seq // tt,),
        in_specs=[
            pl.BlockSpec((bsz, tt, d), lambda t: (0, t, 0)),
            mspec(0), mspec(1),
            pl.BlockSpec((1, d), lambda t: (0, 0)),
            pl.BlockSpec((d, 2 * d_rnn), lambda t: (0, 0)),
        ],
        out_specs=[out_spec, out_spec],
        out_shape=[out_shape, out_shape],
        compiler_params=_cparams(("arbitrary",)),
    )(y3, mod.t3 if st.latent else mod.t4, mod.t3 if st.latent else mod.t4, g, w_bf16)


def _scan_kernel(prev_ref, cur_ref, next_ref, cw_ref, cb_ref, wa_ref, wx_ref, ba_ref, bx_ref, lam_ref,
                 h0_ref, h_ref, hfin_ref, a_scr, b_scr, carry, *, batch, steps, nchunk):
    rev = pl.program_id(0) == 1
    c = pl.program_id(1)
    c_eff = jnp.where(rev, nchunk - 1 - c, c)
    rows = steps * batch
    cb = wa_ref.shape[2]
    for n in range(wa_ref.shape[1]):
        ch = slice(n * cb, (n + 1) * cb)
        prev = jnp.where(c_eff == 0, 0.0, prev_ref[:, ch])
        nxt = jnp.where(c_eff == nchunk - 1, 0.0, next_ref[:, ch])
        xfull = jnp.concatenate([prev, cur_ref[:, ch], nxt], axis=0)
        cw = cw_ref[:, ch]
        xc = cw[0:1, :] * xfull[0:rows]
        for j in range(1, cw.shape[0]):
            xc = xc + cw[j:j + 1, :] * xfull[j * batch:j * batch + rows]
        xc = xc + cb_ref[:, ch]
        xcb = xc.astype(BF16)
        t_r = jnp.tanh(0.5 * (_dot(xcb, wa_ref[0, n]) + ba_ref[0, :, ch]))
        t_i = jnp.tanh(0.5 * (_dot(xcb, wx_ref[0, n]) + bx_ref[0, :, ch]))
        neg_lam = -lam_ref[0, :, ch]
        softplus = jnp.maximum(neg_lam, 0.0) + jnp.log1p(jnp.exp(-jnp.abs(neg_lam)))
        log_a = ((-0.5 * LRU_C) * softplus) * (t_r + 1.0)
        a = jnp.exp(log_a)
        a_scr[:, ch] = a
        b_scr[:, ch] = jnp.sqrt(jnp.tanh(-log_a) * (1.0 + a * a)) * ((0.5 * xc) * (t_i + 1.0))

    @pl.when(c == 0)
    def _():
        carry[...] = h0_ref[0]

    def step(t, h):
        te = jnp.where(rev, steps - 1 - t, t)
        r0 = pl.multiple_of(te * batch, batch)
        h = a_scr[pl.ds(r0, batch), :] * h + b_scr[pl.ds(r0, batch), :]
        h_ref[0, pl.ds(r0, batch), :] = h
        return h

    h = lax.fori_loop(0, steps, step, carry[...], unroll=min(steps, 8))
    carry[...] = h

    @pl.when(c == nchunk - 1)
    def _():
        hfin_ref[0] = h


def _lru_scan(x2, batch, seq, conv_w, conv_b, wa_bf16, wx_bf16, ba, bx, lam, h0):
    d_rnn = conv_w.shape[1]
    nb, cb = wa_bf16.shape[1], wa_bf16.shape[2]
    assert batch % SUBLANES == 0 and conv_w.shape[0] == 4 and nb * cb == d_rnn
    steps = max(SCAN_ROWS // batch, 2)
    rows = steps * batch
    nchunk = seq // steps
    halo = 2 * batch
    hpc = rows // halo
    n_halo = seq * batch // halo
    ce = lambda d, c: jnp.where(d == 1, nchunk - 1 - c, c)
    vec = lambda a: a.reshape(2, 1, d_rnn)
    vspec = pl.BlockSpec((1, 1, d_rnn), lambda d, c: (d, 0, 0))
    wspec = pl.BlockSpec((1, nb, cb, cb), lambda d, c: (d, 0, 0, 0))
    return pl.pallas_call(
        functools.partial(_scan_kernel, batch=batch, steps=steps, nchunk=nchunk),
        name="lru_scan",
        grid=(2, nchunk),
        in_specs=[
            pl.BlockSpec((halo, d_rnn), lambda d, c: (jnp.maximum(ce(d, c) * hpc - 1, 0), 0)),
            pl.BlockSpec((rows, d_rnn), lambda d, c: (ce(d, c), 0)),
            pl.BlockSpec((halo, d_rnn), lambda d, c: (jnp.minimum((ce(d, c) + 1) * hpc, n_halo - 1), 0)),
            pl.BlockSpec((4, d_rnn), lambda d, c: (0, 0)),
            pl.BlockSpec((1, d_rnn), lambda d, c: (0, 0)),
            wspec, wspec, vspec, vspec, vspec,
            pl.BlockSpec((1, batch, d_rnn), lambda d, c: (d, 0, 0)),
        ],
        out_specs=[
            pl.BlockSpec((1, rows, d_rnn), lambda d, c: (d, ce(d, c), 0)),
            pl.BlockSpec((1, batch, d_rnn), lambda d, c: (d, 0, 0)),
        ],
        out_shape=[
            jax.ShapeDtypeStruct((2, seq * batch, d_rnn), F32),
            jax.ShapeDtypeStruct((2, batch, d_rnn), F32),
        ],
        scratch_shapes=[pltpu.VMEM((rows, d_rnn), F32), pltpu.VMEM((rows, d_rnn), F32),
                        pltpu.VMEM((batch, d_rnn), F32)],
        compiler_params=_cparams(("arbitrary", "arbitrary")),
    )(x2, x2, x2, conv_w, conv_b[None, :], wa_bf16, wx_bf16, vec(ba), vec(bx), vec(lam), h0)


def _lru_out_kernel(y_ref, gate_ref, h_ref, gt_ref, w_ref, o_ref, z_scr, *, latent):
    bsz, tt, d = y_ref.shape
    z = jax.nn.gelu(gate_ref[...]) * (h_ref[0] + h_ref[1])
    z_scr[...] = z.reshape(tt, bsz, z.shape[1])
    zb = jnp.concatenate([z_scr[:, b, :] for b in range(bsz)], axis=0)
    out = _dot(zb.astype(BF16), w_ref[...]).reshape(bsz, tt, d)
    gt = gt_ref[0][:, None, :] if latent else gt_ref[0]
    o_ref[...] = y_ref[...] + gt * out


def _lru_out(y3, gate, h, st, mod, w_bf16):
    bsz, seq, d = y3.shape
    d_rnn = w_bf16.shape[0]
    tt = TOKEN_TILE // bsz
    gspec = mod.latent_spec(1, 2) if st.latent else mod.row_spec(1, 2, lambda t: 0)
    return pl.pallas_call(
        functools.partial(_lru_out_kernel, latent=st.latent),
        name="lru_out",
        grid=(seq // tt,),
        in_specs=[
            pl.BlockSpec((bsz, tt, d), lambda t: (0, t, 0)),
            pl.BlockSpec((tt * bsz, d_rnn), lambda t: (t, 0)),
            pl.BlockSpec((2, tt * bsz, d_rnn), lambda t: (0, t, 0)),
            gspec,
            pl.BlockSpec((d_rnn, d), lambda t: (0, 0)),
        ],
        out_specs=pl.BlockSpec((bsz, tt, d), lambda t: (0, t, 0)),
        out_shape=jax.ShapeDtypeStruct((bsz, seq, d), F32),
        scratch_shapes=[pltpu.VMEM((tt, bsz, d_rnn), F32)],
        compiler_params=_cparams(("arbitrary",)),
    )(y3, gate, h, mod.t3 if st.latent else mod.t4, w_bf16)


def _dual_specs(shape, sp, ss, extra=0):
    pmap = lambda i, *_: (jnp.minimum(i, sp.tiles - 1), 0)
    smap = lambda i, *_: (jnp.maximum(i - sp.tiles, 0), 0)
    return pl.BlockSpec(shape, pmap), pl.BlockSpec(shape, smap)


def _joint_cond_row(sp, ss):
    return lambda i, *_: jnp.where(i < sp.tiles, 0, ss.cond_row(jnp.maximum(i - sp.tiles, 0)))


def _router_kernel(yp_ref, ys_ref, sh_ref, sc_ref, g_ref, rwt_ref, rb_ref, bkt_ref, rank_ref, wext_ref,
                   cnt_ref, carry, *, n_experts, ntp, n_steps):
    i = pl.program_id(0)

    @pl.when(i == 0)
    def _():
        carry[...] = jnp.zeros_like(carry)

    y = jnp.where(i < ntp, yp_ref[...], ys_ref[...])
    hn = _norm_mod(y, g_ref[...], sc_ref[0, 0], sh_ref[0, 0])
    tm = hn.shape[0]
    logits = lax.dot_general(rwt_ref[...], hn, (((1,), (1,)), ((), ())),
                             precision=lax.Precision.HIGHEST, preferred_element_type=F32)
    scores = jax.nn.sigmoid(logits)
    sel = scores + rb_ref[...]
    per = n_experts // N_GROUPS
    assert per == 4
    srow = [sel[e:e + 1, :] for e in range(n_experts)]
    prow = [scores[e:e + 1, :] for e in range(n_experts)]

    def top2_sum(a, b, c, d):
        hi1, lo1, hi2, lo2 = jnp.maximum(a, b), jnp.minimum(a, b), jnp.maximum(c, d), jnp.minimum(c, d)
        return jnp.maximum(hi1, hi2) + jnp.maximum(jnp.minimum(hi1, hi2), jnp.maximum(lo1, lo2))

    gscore = [top2_sum(*srow[per * g:per * (g + 1)]) for g in range(N_GROUPS)]
    best = jnp.zeros((1, tm), I32)
    best_v = gscore[0]
    for g in range(1, N_GROUPS):
        upd = gscore[g] > best_v
        best = jnp.where(upd, g, best)
        best_v = jnp.where(upd, gscore[g], best_v)
    vs = list(srow[:per])
    ps = list(prow[:per])
    for g in range(1, N_GROUPS):
        for j in range(per):
            vs[j] = jnp.where(best == g, srow[per * g + j], vs[j])
            ps[j] = jnp.where(best == g, prow[per * g + j], ps[j])
    j1 = jnp.zeros((1, tm), I32)
    m1 = vs[0]
    for j in range(1, per):
        upd = vs[j] > m1
        j1 = jnp.where(upd, j, j1)
        m1 = jnp.where(upd, vs[j], m1)
    j2 = jnp.full((1, tm), -1, I32)
    m2 = jnp.full((1, tm), -jnp.inf, F32)
    for j in range(per):
        upd = (j1 != j) & (vs[j] > m2)
        j2 = jnp.where(upd, j, j2)
        m2 = jnp.where(upd, vs[j], m2)
    s1 = jnp.zeros((1, tm), F32)
    s2 = jnp.zeros((1, tm), F32)
    for j in range(per):
        s1 = jnp.where(j1 == j, ps[j], s1)
        s2 = jnp.where(j2 == j, ps[j], s2)
    den = s1 + s2
    w1, w2 = s1 / den, s2 / den
    lo = jnp.minimum(j1, j2)
    hi = jnp.maximum(j1, j2)
    w_lo = jnp.where(j1 < j2, w1, w2)
    w_hi = jnp.where(j1 < j2, w2, w1)
    base = jnp.where(lo == 0, 0, jnp.where(lo == 1, 3, 5))
    bucket = best * 6 + base + hi - lo - 1

    onehot = (lax.broadcasted_iota(I32, (N_BUCKET_SLOTS, tm), 0) == bucket).astype(F32)
    before = (lax.broadcasted_iota(I32, (tm, tm), 0) < lax.broadcasted_iota(I32, (tm, tm), 1)).astype(BF16)
    prefix = _dot(onehot.astype(BF16), before) + carry[...]
    rank = jnp.sum(onehot * prefix, axis=0, keepdims=True)
    carry[...] = carry[...] + jnp.sum(onehot, axis=1, keepdims=True)

    bkt_ref[0] = bucket
    rank_ref[0] = rank.astype(I32)
    slab_row = lax.broadcasted_iota(I32, (LANES, tm), 0)
    slab = jnp.where(slab_row == 0, w_lo, jnp.where(slab_row == 1, w_hi, 0.0))
    wext_ref[...] = slab.T

    @pl.when(i == n_steps - 1)
    def _():
        cnt_ref[...] = jnp.broadcast_to(carry[...], cnt_ref.shape).astype(I32)


def _router(yp, ys, sp, ss, mod, layer, g, router_wt, router_b):
    d = yp.shape[1]
    e = router_wt.shape[0]
    tm = TOKEN_TILE
    nt = sp.tiles + ss.tiles
    row = _joint_cond_row(sp, ss)
    tok_spec = pl.BlockSpec((1, 1, tm), lambda i: (i, 0, 0))
    pspec, sspec = _dual_specs((tm, d), sp, ss)
    return pl.pallas_call(
        functools.partial(_router_kernel, n_experts=e, ntp=sp.tiles, n_steps=nt),
        name="moe_router",
        grid=(nt,),
        in_specs=[
            pspec, sspec,
            mod.row_spec(layer, 3, row),
            mod.row_spec(layer, 4, row),
            pl.BlockSpec((1, d), lambda i: (0, 0)),
            pl.BlockSpec((e, d), lambda i: (0, 0)),
            pl.BlockSpec((e, 1), lambda i: (0, 0)),
        ],
        out_specs=[tok_spec, tok_spec,
                   pl.BlockSpec((tm, LANES), lambda i: (i, 0)),
                   pl.BlockSpec((N_BUCKET_SLOTS, LANES), lambda i: (0, 0))],
        out_shape=[jax.ShapeDtypeStruct((nt, 1, tm), I32), jax.ShapeDtypeStruct((nt, 1, tm), I32),
                   jax.ShapeDtypeStruct((nt * tm, LANES), F32),
                   jax.ShapeDtypeStruct((N_BUCKET_SLOTS, LANES), I32)],
        scratch_shapes=[pltpu.VMEM((N_BUCKET_SLOTS, 1), F32)],
        compiler_params=_cparams(("arbitrary",)),
    )(yp, ys, mod.t4, mod.t4, g, router_wt, router_b[:, None])


def _bucket_layout(counts, n_tokens):
    n_buckets = N_GROUPS * len(PAIR_LO)
    tf = FFN_TILE
    n_tiles = n_tokens // tf + n_buckets
    c = counts[:n_buckets]
    padded = ((c + tf - 1) // tf) * tf
    ends = jnp.cumsum(padded)
    starts = ends - padded
    n_used = ends[-1] // tf
    tile = jnp.arange(n_tiles, dtype=I32)
    valid = tile < n_used
    tile_c = jnp.minimum(tile, n_used - 1)
    tb = jnp.sum((ends[None, :] <= (tile_c * tf)[:, None]).astype(I32), axis=1)
    tb = jnp.minimum(tb, n_buckets - 1)
    grp, pair = tb // len(PAIR_LO), tb % len(PAIR_LO)
    pair_lo = sum(jnp.where(pair == p, PAIR_LO[p], 0) for p in range(len(PAIR_LO)))
    pair_hi = sum(jnp.where(pair == p, PAIR_HI[p], 0) for p in range(len(PAIR_HI)))
    starts32 = jnp.concatenate([starts.astype(I32), jnp.zeros((N_BUCKET_SLOTS - n_buckets,), I32)])
    sched = jnp.stack([tile_c, grp * 4 + pair_lo, grp * 4 + pair_hi, valid.astype(I32)]).astype(I32)
    return starts32, sched, n_tiles


def _pos_kernel(starts_ref, bkt_ref, rank_ref, pos_ref):
    bkt = bkt_ref[...]
    pos = rank_ref[...]
    for b in range(N_GROUPS * len(PAIR_LO)):
        pos = pos + jnp.where(bkt == b, starts_ref[b], 0)
    pos_ref[...] = pos


def _positions(starts32, bkt, rank):
    nt, _, tm = bkt.shape
    grid_spec = pltpu.PrefetchScalarGridSpec(
        num_scalar_prefetch=1,
        grid=(1,),
        in_specs=[pl.BlockSpec((nt, 1, tm), lambda i, st: (0, 0, 0))] * 2,
        out_specs=pl.BlockSpec((nt, 1, tm), lambda i, st: (0, 0, 0)),
    )
    return pl.pallas_call(
        _pos_kernel, name="moe_positions", grid_spec=grid_spec,
        out_shape=jax.ShapeDtypeStruct((nt, 1, tm), I32),
        compiler_params=_cparams(("arbitrary",)),
    )(starts32, bkt, rank)


def _dispatch_kernel(yp_ref, ys_ref, sh_ref, sc_ref, g_ref, wext_ref, pos_ref, init_ref,
                     xs_ref, buf0, buf1, sem0, sem1, *, d, ntp, n_steps):
    del init_ref
    tm = buf0.shape[0]
    i = pl.program_id(0)
    y = jnp.where(i < ntp, yp_ref[...], ys_ref[...])
    hn = _norm_mod(y, g_ref[...], sc_ref[0, 0], sh_ref[0, 0])

    def wait_tile(buf, sem):
        pltpu.make_async_copy(buf, xs_ref.at[pl.ds(0, tm)], sem).wait()

    def step(buf, sem, prev_buf, prev_sem):
        buf[:, :d] = hn
        buf[:, d:] = wext_ref[...]

        def issue(r, carry):
            pltpu.make_async_copy(buf.at[pl.ds(r, 1)], xs_ref.at[pl.ds(pos_ref[0, 0, r], 1)], sem).start()
            return carry

        lax.fori_loop(0, tm, issue, 0, unroll=8)

        @pl.when(i > 0)
        def _():
            wait_tile(prev_buf, prev_sem)

        @pl.when(i == n_steps - 1)
        def _():
            wait_tile(buf, sem)

    @pl.when(i % 2 == 0)
    def _():
        step(buf0, sem0, buf1, sem1)

    @pl.when(i % 2 == 1)
    def _():
        step(buf1, sem1, buf0, sem0)


def _dispatch(yp, ys, sp, ss, mod, layer, g, wext, pos, n_rows):
    d = yp.shape[1]
    tm = TOKEN_TILE
    width = d + LANES
    row = _joint_cond_row(sp, ss)
    pspec, sspec = _dual_specs((tm, d), sp, ss)
    return pl.pallas_call(
        functools.partial(_dispatch_kernel, d=d, ntp=sp.tiles, n_steps=sp.tiles + ss.tiles),
        name="moe_dispatch",
        grid=(sp.tiles + ss.tiles,),
        in_specs=[
            pspec, sspec,
            mod.row_spec(layer, 3, row), mod.row_spec(layer, 4, row),
            pl.BlockSpec((1, d), lambda i: (0, 0)),
            pl.BlockSpec((tm, LANES), lambda i: (i, 0)),
            pl.BlockSpec((1, 1, tm), lambda i: (i, 0, 0), memory_space=pltpu.SMEM),
            pl.BlockSpec(memory_space=pl.ANY),
        ],
        out_specs=pl.BlockSpec(memory_space=pl.ANY),
        out_shape=jax.ShapeDtypeStruct((n_rows, width), F32),
        scratch_shapes=[pltpu.VMEM((tm, width), F32), pltpu.VMEM((tm, width), F32),
                        pltpu.SemaphoreType.DMA(()), pltpu.SemaphoreType.DMA(())],
        input_output_aliases={7: 0},
        compiler_params=_cparams(("arbitrary",), disable_bounds_checks=True),
    )(yp, ys, mod.t4, mod.t4, g, wext, pos, jnp.zeros((n_rows, width), F32))


def _ffn_kernel(sched_ref, xs_ref, wg_lo, wu_lo, wd_lo, wg_hi, wu_hi, wd_hi, o_ref, *, d):
    valid = sched_ref[3, pl.program_id(0)] == 1

    @pl.when(jnp.logical_not(valid))
    def _():
        o_ref[...] = jnp.zeros_like(o_ref)

    @pl.when(valid)
    def _():
        x = xs_ref[:, :d].astype(BF16)

        def expert(wg, wu, wd):
            gate = _dot(x, wg[0, 0])
            he = (gate * jax.nn.sigmoid(gate)) * _dot(x, wu[0, 0])
            return _dot(he.astype(BF16), wd[0, 0])

        w_lo = xs_ref[:, d:d + 1]
        w_hi = xs_ref[:, d + 1:d + 2]
        o_ref[...] = w_lo * expert(wg_lo, wu_lo, wd_lo) + w_hi * expert(wg_hi, wu_hi, wd_hi)


def _expert_ffn(xs, sched, n_tiles, layer, wg, wu, wd):
    n_rows, width = xs.shape
    d = width - LANES
    f = wg.shape[3]
    tf = FFN_TILE
    lo = lambda i, s: (layer, s[1, i], 0, 0)
    hi = lambda i, s: (layer, s[2, i], 0, 0)
    grid_spec = pltpu.PrefetchScalarGridSpec(
        num_scalar_prefetch=1,
        grid=(n_tiles,),
        in_specs=[
            pl.BlockSpec((tf, width), lambda i, s: (s[0, i], 0)),
            pl.BlockSpec((1, 1, d, f), lo), pl.BlockSpec((1, 1, d, f), lo), pl.BlockSpec((1, 1, f, d), lo),
            pl.BlockSpec((1, 1, d, f), hi), pl.BlockSpec((1, 1, d, f), hi), pl.BlockSpec((1, 1, f, d), hi),
        ],
        out_specs=pl.BlockSpec((tf, d), lambda i, s: (i, 0)),
    )
    return pl.pallas_call(
        functools.partial(_ffn_kernel, d=d),
        name="moe_expert_ffn",
        grid_spec=grid_spec,
        out_shape=jax.ShapeDtypeStruct((n_rows, d), F32),
        compiler_params=_cparams(("arbitrary",)),
    )(sched, xs, wg, wu, wd, wg, wu, wd)


def _combine_kernel(y_ref, gt_ref, pos_ref, pos_next_ref, ms_ref, o_ref, buf0, buf1, sem0, sem1, *, n_steps):
    tm = buf0.shape[0]
    i = pl.program_id(0)

    def gather(pos, buf, sem):
        def issue(r, carry):
            pltpu.make_async_copy(ms_ref.at[pl.ds(pos[0, 0, r], 1)], buf.at[pl.ds(r, 1)], sem).start()
            return carry

        lax.fori_loop(0, tm, issue, 0, unroll=8)

    def step(buf, sem, next_buf, next_sem):
        @pl.when(i == 0)
        def _():
            gather(pos_ref, buf, sem)

        @pl.when(i + 1 < n_steps)
        def _():
            gather(pos_next_ref, next_buf, next_sem)

        pltpu.make_async_copy(ms_ref.at[pl.ds(0, tm)], buf, sem).wait()
        o_ref[...] = y_ref[...] + gt_ref[0, 0] * buf[...]

    @pl.when(i % 2 == 0)
    def _():
        step(buf0, sem0, buf1, sem1)

    @pl.when(i % 2 == 1)
    def _():
        step(buf1, sem1, buf0, sem0)


def _combine(y2d, st, mod, layer, pos, ms):
    n, d = y2d.shape
    tm = TOKEN_TILE
    tile = pl.BlockSpec((tm, d), lambda i: (i, 0))
    last = st.tile_off + n // tm - 1
    return pl.pallas_call(
        functools.partial(_combine_kernel, n_steps=n // tm),
        name="moe_combine",
        grid=(n // tm,),
        in_specs=[
            tile,
            mod.row_spec(layer, 5, st.cond_row),
            pl.BlockSpec((1, 1, tm), lambda i: (st.tile_off + i, 0, 0), memory_space=pltpu.SMEM),
            pl.BlockSpec((1, 1, tm), lambda i: (jnp.minimum(st.tile_off + i + 1, last), 0, 0),
                         memory_space=pltpu.SMEM),
            pl.BlockSpec(memory_space=pl.ANY),
        ],
        out_specs=tile,
        out_shape=jax.ShapeDtypeStruct((n, d), F32),
        scratch_shapes=[pltpu.VMEM((tm, d), F32), pltpu.VMEM((tm, d), F32),
                        pltpu.SemaphoreType.DMA(()), pltpu.SemaphoreType.DMA(())],
        compiler_params=_cparams(("arbitrary",), disable_bounds_checks=True),
    )(y2d, mod.t4, pos, pos, ms)


def _moe_residual(yp, ys, sp, ss, mod, layer, g, router_wt, router_b, wg, wu, wd):
    bkt, rank, wext, counts = _router(yp, ys, sp, ss, mod, layer, g, router_wt, router_b)
    starts32, sched, n_tiles = _bucket_layout(counts[:, 0], sp.n + ss.n)
    pos = _positions(starts32, bkt, rank)
    xs = _dispatch(yp, ys, sp, ss, mod, layer, g, wext, pos, n_tiles * FFN_TILE)
    ms = _expert_ffn(xs, sched, n_tiles, layer, wg, wu, wd)
    return _combine(yp, sp, mod, layer, pos, ms), _combine(ys, ss, mod, layer, pos, ms)


def kernel(x_prompt, x_sample, c, cache_k, cache_v, state_lru, c_ctx, ada_w, ada_b, norm_mix, norm_ffn,
           attn_w_in, attn_q_gain, attn_k_gain, attn_rpb, attn_w_out, lru_w_in, lru_conv_w, lru_conv_b,
           lru_gate_a_w, lru_gate_a_b, lru_gate_x_w, lru_gate_x_b, lru_lambda, lru_w_out, router_w,
           router_b, moe_w_gate, moe_w_up, moe_w_down):
    bp, tp, d = x_prompt.shape
    bs, ts, _ = x_sample.shape
    n_heads, head_dim = cache_k.shape[3], cache_k.shape[4]
    d_rnn = lru_conv_w.shape[2]
    assert ada_w.shape[0] == 2 and bs <= N_COND_ROWS - LATENT_COND_ROW
    assert tp % TOKEN_TILE == 0 and ts % TOKEN_TILE == 0
    sp = _Stream(bp, tp, 0, latent=False)
    ss = _Stream(bs, ts, sp.tiles, latent=True)

    cond = jnp.zeros((N_COND_ROWS, d), F32).at[0].set(c_ctx).at[LATENT_COND_ROW:LATENT_COND_ROW + bs].set(c)
    mod = _Mod(_ada_modulation(cond, ada_w, ada_b), d)

    w_in = attn_w_in[0].astype(BF16)
    w_out = attn_w_out[0].astype(BF16)
    g_mix0 = norm_mix[0][None, :]
    xp2, xs2 = x_prompt.reshape(sp.n, d), x_sample.reshape(ss.n, d)
    q_p, k_p, v_p, new_k, new_v = _qkv_proj(xp2, sp, mod, g_mix0, w_in, attn_q_gain[0], attn_k_gain[0],
                                            n_heads, True)
    q_s, k_s, v_s = _qkv_proj(xs2, ss, mod, g_mix0, w_in, attn_q_gain[0], attn_k_gain[0], n_heads, False)
    o_p = _ctx_attention(q_p, k_p, v_p, bp, tp)
    pair_major = lambda cache: jnp.transpose(
        cache[:, 0].astype(BF16).reshape(bs, cache.shape[2], n_heads // 2, LANES), (2, 0, 1, 3))
    ck, cv = pair_major(cache_k), pair_major(cache_v)
    o_s = _nbr_attention(q_s, k_s, v_s, ck, cv, attn_rpb[0], bs, ts)
    yp = _attn_out(xp2, o_p, sp, mod, w_out)
    ys = _attn_out(xs2, o_s, ss, mod, w_out)

    router_wt = router_w.T
    moe_w = (moe_w_gate.astype(BF16), moe_w_up.astype(BF16), moe_w_down.astype(BF16))
    yp, ys = _moe_residual(yp, ys, sp, ss, mod, 0, norm_ffn[0][None, :], router_wt, router_b, *moe_w)

    w_lru_in = lru_w_in[0].astype(BF16)
    w_lru_out = lru_w_out[0].astype(BF16)
    g_mix1 = norm_mix[1][None, :]
    scan_args = (lru_conv_w[0], lru_conv_b[0], lru_gate_a_w[0].astype(BF16), lru_gate_x_w[0].astype(BF16),
                 lru_gate_a_b[0], lru_gate_x_b[0], lru_lambda[0])
    hfin = {}
    ys_out = []
    for y2, st, h0 in ((yp, sp, jnp.zeros((2, bp, d_rnn), F32)),
                       (ys, ss, jnp.transpose(state_lru[:, 0], (1, 0, 2)))):
        y3 = y2.reshape(st.batch, st.seq, d)
        gate, xbr = _lru_in(y3, st, mod, g_mix1, w_lru_in)
        h, hfin[st.latent] = _lru_scan(xbr, st.batch, st.seq, *scan_args, h0)
        ys_out.append(_lru_out(y3, gate, h, st, mod, w_lru_out).reshape(st.n, d))
    yp, ys = ys_out
    yp, ys = _moe_residual(yp, ys, sp, ss, mod, 1, norm_ffn[1][None, :], router_wt, router_b, *moe_w)

    return (yp.reshape(bp, tp, d), ys.reshape(bs, ts, d),
            new_k.reshape(bp, 1, tp, n_heads, head_dim), new_v.reshape(bp, 1, tp, n_heads, head_dim),
            jnp.transpose(hfin[False], (1, 0, 2))[:, None])
```

```python
import functools

import numpy as np
import jax
import jax.numpy as jnp
from jax import lax
from jax.experimental import pallas as pl
from jax.experimental.pallas import tpu as pltpu

F32 = jnp.float32
BF16 = jnp.bfloat16
I32 = jnp.int32

EPS = 1e-6
NEG_INF = -1e30
GRID_W = 64
WIN_ROWS = 8
WIN_COLS = 16
N_GROUPS = 4
LRU_C = 8.0
N_LRU_BLOCKS = 4

LANES = 128
SUBLANES = 8
MXU_COLS = 256
TOKEN_TILE = 256
QKV_TILE = 512
FFN_TILE = 256
Q_ROWS = 4
SCAN_ROWS = 512
ATTN_PAIRS = 2
N_BUCKET_SLOTS = 32
PAIR_LO = (0, 0, 0, 1, 1, 2)
PAIR_HI = (1, 2, 3, 2, 3, 3)
N_COND_ROWS = 16
LATENT_COND_ROW = 8
VMEM_LIMIT = 48 * 1024 * 1024

def _cparams(sem, **kw):
    return pltpu.CompilerParams(dimension_semantics=sem, vmem_limit_bytes=VMEM_LIMIT, **kw)


def _norm_mod(x, g, sc, sh):
    ms = jnp.mean(x * x, axis=-1, keepdims=True)
    return (x * lax.rsqrt(ms + EPS)) * g * (1.0 + sc) + sh


def _dot(a, b):
    return jnp.dot(a, b, preferred_element_type=F32)


def _dot_nt(a, b):
    return lax.dot_general(a, b, (((1,), (1,)), ((), ())), preferred_element_type=F32)


def _ada_kernel(cond_ref, w_ref, b_ref, o_ref):
    c = cond_ref[...]
    s = (c * jax.nn.sigmoid(c)).astype(BF16)
    o_ref[0] = _dot(s, w_ref[0].astype(BF16)) + b_ref[0]


def _ada_modulation(cond, ada_w, ada_b):
    depth, d, d6 = ada_w.shape
    tn = d6 // 4
    return pl.pallas_call(
        _ada_kernel,
        name="ada_modulation",
        grid=(depth, d6 // tn),
        in_specs=[
            pl.BlockSpec((N_COND_ROWS, d), lambda l, j: (0, 0)),
            pl.BlockSpec((1, d, tn), lambda l, j: (l, 0, j)),
            pl.BlockSpec((1, 1, tn), lambda l, j: (l, 0, j)),
        ],
        out_specs=pl.BlockSpec((1, N_COND_ROWS, tn), lambda l, j: (l, 0, j)),
        out_shape=jax.ShapeDtypeStruct((depth, N_COND_ROWS, d6), F32),
        compiler_params=_cparams(("arbitrary", "arbitrary")),
    )(cond, ada_w, ada_b.reshape(depth, 1, d6))


class _Mod:
    def __init__(self, table, d):
        self.t3 = table
        self.t4 = table.reshape(table.shape[0], N_COND_ROWS, 1, table.shape[2])
        self.d = d

    def row_spec(self, layer, chunk, row_fn):
        return pl.BlockSpec((1, 1, 1, self.d), lambda *ids: (layer, row_fn(*ids), 0, chunk))

    def latent_spec(self, layer, chunk):
        return pl.BlockSpec((1, SUBLANES, self.d), lambda *ids: (layer, LATENT_COND_ROW // SUBLANES, chunk))


class _Stream:
    def __init__(self, batch, seq, tile_off, latent):
        self.batch, self.seq, self.tile_off, self.latent = batch, seq, tile_off, latent
        self.n = batch * seq
        self.tiles = self.n // TOKEN_TILE

    def cond_row(self, i, tile=TOKEN_TILE):
        if not self.latent:
            return 0
        return LATENT_COND_ROW + (i * tile) // self.seq


def _qkv_kernel(x_ref, sh_ref, sc_ref, g_ref, w_ref, qg_ref, kg_ref, gmean_ref, q_ref, k_ref, v_ref, *kv_refs,
                d, head_dim):
    hn = _norm_mod(x_ref[...], g_ref[...], sc_ref[0, 0], sh_ref[0, 0]).astype(BF16)
    outs = (q_ref, k_ref, v_ref)
    gains = (qg_ref, kg_ref)
    n_chunks = 3 * d // MXU_COLS
    n_norm = 2 * d // MXU_COLS
    accs = [_dot(hn, w_ref[:, MXU_COLS * j:MXU_COLS * (j + 1)]) for j in range(n_chunks)]
    means = []
    for j in range(n_norm):
        sq = accs[j] * accs[j]
        hi = sq.astype(BF16)
        lo = (sq - hi.astype(F32)).astype(BF16)
        means.append(_dot(hi, gmean_ref[...]) + _dot(lo, gmean_ref[...]))
    for j in range(n_chunks):
        part = (MXU_COLS * j) // d
        col = (MXU_COLS * j) % d
        acc = accs[j]
        if j < n_norm:
            acc = (acc * lax.rsqrt(means[j] + EPS)) * gains[part][...]
        for half in range(MXU_COLS // LANES):
            outs[part][col // LANES + half] = acc[:, LANES * half:LANES * (half + 1)].astype(BF16)
        if part >= 1 and kv_refs:
            kv_refs[part - 1][:, col:col + MXU_COLS] = acc


def _qkv_proj(x2d, st, mod, g, w_bf16, q_gain, k_gain, n_heads, with_kv):
    n, d = x2d.shape
    head_dim = d // n_heads
    assert head_dim * 2 == LANES
    tm = QKV_TILE
    n_pairs = n_heads // 2
    hspec = pl.BlockSpec((n_pairs, tm, LANES), lambda i: (0, i, 0))
    hshape = jax.ShapeDtypeStruct((n_pairs, n, LANES), BF16)
    out_specs = [hspec, hspec, hspec]
    out_shape = [hshape, hshape, hshape]
    if with_kv:
        out_specs += [pl.BlockSpec((tm, d), lambda i: (i, 0))] * 2
        out_shape += [jax.ShapeDtypeStruct((n, d), F32)] * 2
    lane_head = np.arange(MXU_COLS) // head_dim
    assert float(np.log2(head_dim)).is_integer()
    gmean = jnp.asarray((lane_head[:, None] == lane_head[None, :]) / head_dim, BF16)
    reps = MXU_COLS // head_dim
    qg = jnp.tile(q_gain * _attn_scale(head_dim), reps)[None, :]
    kg = jnp.tile(k_gain, reps)[None, :]
    const = lambda shape: pl.BlockSpec(shape, lambda i: (0,) * len(shape))
    return pl.pallas_call(
        functools.partial(_qkv_kernel, d=d, head_dim=head_dim),
        name="qkv_proj",
        grid=(n // tm,),
        in_specs=[
            pl.BlockSpec((tm, d), lambda i: (i, 0)),
            mod.row_spec(0, 0, lambda i: st.cond_row(i, tm)),
            mod.row_spec(0, 1, lambda i: st.cond_row(i, tm)),
            const((1, d)), const((d, 3 * d)), const((1, MXU_COLS)), const((1, MXU_COLS)),
            const((MXU_COLS, MXU_COLS)),
        ],
        out_specs=out_specs,
        out_shape=out_shape,
        compiler_params=_cparams(("arbitrary",)),
    )(x2d, mod.t4, mod.t4, g, w_bf16, qg, kg, gmean)


def _attn_scale(head_dim):
    scale = head_dim ** -0.5
    assert float(np.log2(scale)).is_integer()
    return scale


def _softmax_pv(scores, values):
    m = scores[0].max(axis=-1, keepdims=True)
    for s in scores[1:]:
        m = jnp.maximum(m, s.max(axis=-1, keepdims=True))
    es = [jnp.exp(s - m) for s in scores]
    l = es[0].sum(axis=-1, keepdims=True)
    for e in es[1:]:
        l = l + e.sum(axis=-1, keepdims=True)
    o = _dot(es[0].astype(BF16), values[0])
    for e, v in zip(es[1:], values[1:]):
        o = o + _dot(e.astype(BF16), v)
    return o / l


def _pair_attention(q2, score_keys, values, biases):
    first = lax.broadcasted_iota(I32, q2.shape, 1) < q2.shape[1] // 2
    zero = jnp.zeros_like(q2)
    outs = []
    for head in range(2):
        qh = jnp.where(first, q2, zero) if head == 0 else jnp.where(first, zero, q2)
        scores = []
        for keys, bias in zip(score_keys, biases[head]):
            s = _dot_nt(qh, keys)
            scores.append(s if bias is None else s + bias)
        outs.append(_softmax_pv(scores, values))
    return jnp.where(first, outs[0], outs[1])


def _ctx_attn_kernel(q_ref, k_ref, v_ref, o_ref):
    for hp in range(q_ref.shape[0]):
        o = _pair_attention(q_ref[hp], [k_ref[hp]], [v_ref[hp]], [[None], [None]])
        o_ref[:, LANES * hp:LANES * (hp + 1)] = o.astype(BF16)


def _ctx_attention(q, k, v, batch, seq):
    n_pairs, n, _ = q.shape
    spec = pl.BlockSpec((n_pairs, seq, LANES), lambda b: (0, b, 0))
    return pl.pallas_call(
        _ctx_attn_kernel,
        name="ctx_attention",
        grid=(batch,),
        in_specs=[spec, spec, spec],
        out_specs=pl.BlockSpec((seq, n_pairs * LANES), lambda b: (b, 0)),
        out_shape=jax.ShapeDtypeStruct((n, n_pairs * LANES), BF16),
        compiler_params=_cparams(("arbitrary",)),
    )(q, k, v)


def _nbr_geometry(rows):
    n_key_rows = Q_ROWS + WIN_ROWS
    assert rows % Q_ROWS == 0 and rows >= n_key_rows
    nblk = rows // Q_ROWS
    kstart = np.clip(np.arange(nblk) * Q_ROWS - WIN_ROWS // 2, 0, rows - n_key_rows)
    start = np.clip(np.arange(rows) - WIN_ROWS // 2, 0, rows - WIN_ROWS)
    tables, variant = [], []
    for blk in range(nblk):
        r = blk * Q_ROWS + np.arange(Q_ROWS)
        kr = kstart[blk] + np.arange(n_key_rows)
        valid = (kr[None, :] >= start[r][:, None]) & (kr[None, :] < start[r][:, None] + WIN_ROWS)
        dr = np.clip(kr[None, :] - r[:, None] + WIN_ROWS - 1, 0, 2 * WIN_ROWS - 2)
        key = (valid.tobytes(), dr.tobytes())
        keys = [t[0] for t in tables]
        if key not in keys:
            tables.append((key, valid, dr))
        variant.append([t[0] for t in tables].index(key))
    return nblk, n_key_rows, kstart, np.array(variant), [(t[1], t[2]) for t in tables]


def _nbr_bias(rpb, tables):
    h, nd, nc = rpb.shape
    w = GRID_W
    period = 2 * w
    pad_l = (w - 1) - (WIN_COLS - 1)
    r_ext = jnp.pad(rpb.astype(F32), ((0, 0), (0, 0), (pad_l, period - nc - pad_l)))
    toe = jnp.tile(r_ext, (1, 1, w))[:, :, :w * (period - 1)].reshape(h, nd, w, period - 1)
    toe = toe[..., w - 1:2 * w - 1]
    cw = np.arange(w)
    col_start = np.clip(cw - WIN_COLS // 2, 0, w - WIN_COLS)
    col_ok = (cw[None, :] >= col_start[:, None]) & (cw[None, :] < col_start[:, None] + WIN_COLS)
    toe = jnp.where(col_ok[None, None], toe, NEG_INF)
    tq = jnp.transpose(toe, (0, 2, 1, 3)).reshape(h, w, nd * w)
    out = []
    for valid, dr in tables:
        qr, kr = valid.shape
        rows = []
        for i in range(qr):
            js = np.nonzero(valid[i])[0]
            j0, j1 = int(js[0]), int(js[-1]) + 1
            assert (np.diff(dr[i, j0:j1]) == 1).all() and valid[i, j0:j1].all()
            d0 = int(dr[i, j0])
            seg = tq[:, :, d0 * w:(d0 + j1 - j0) * w]
            rows.append(jnp.pad(seg, ((0, 0), (0, 0), (j0 * w, (kr - j1) * w)), constant_values=NEG_INF))
        out.append(jnp.stack(rows, axis=1).reshape(h, qr * w, kr * w))
    return jnp.stack(out, axis=0)


def _nbr_attn_kernel(kstart_ref, q_ref, k_ref, v_ref, ck_ref, cv_ref, bias_ref, o_ref, *, n_local, nblk):
    blk = pl.program_id(2)
    k0 = pl.multiple_of(kstart_ref[blk] * GRID_W, GRID_W)
    variant = kstart_ref[nblk + blk]
    for hp in range(q_ref.shape[0]):
        kl = k_ref[hp, pl.ds(k0, n_local), :]
        vl = v_ref[hp, pl.ds(k0, n_local), :]
        biases = [[bias_ref[variant, 2 * hp + head], None] for head in range(2)]
        o = _pair_attention(q_ref[hp], [kl, ck_ref[hp, 0]], [vl, cv_ref[hp, 0]], biases)
        o_ref[:, LANES * hp:LANES * (hp + 1)] = o.astype(BF16)


def _nbr_attention(q, k, v, ck, cv, rpb, batch, seq):
    n_pairs, n, _ = q.shape
    rows = seq // GRID_W
    nblk, n_key_rows, kstart, variant, tables = _nbr_geometry(rows)
    bias = _nbr_bias(rpb, tables)
    m = Q_ROWS * GRID_W
    n_local = n_key_rows * GRID_W
    past = ck.shape[2]
    sched = jnp.asarray(np.concatenate([kstart, variant]), I32)
    npr = ATTN_PAIRS
    grid_spec = pltpu.PrefetchScalarGridSpec(
        num_scalar_prefetch=1,
        grid=(n_pairs // npr, batch, nblk),
        in_specs=[
            pl.BlockSpec((npr, m, LANES), lambda hg, b, r, ks: (hg, b * nblk + r, 0)),
            pl.BlockSpec((npr, seq, LANES), lambda hg, b, r, ks: (hg, b, 0)),
            pl.BlockSpec((npr, seq, LANES), lambda hg, b, r, ks: (hg, b, 0)),
            pl.BlockSpec((npr, 1, past, LANES), lambda hg, b, r, ks: (hg, b, 0, 0)),
            pl.BlockSpec((npr, 1, past, LANES), lambda hg, b, r, ks: (hg, b, 0, 0)),
            pl.BlockSpec((bias.shape[0], 2 * npr, m, n_local), lambda hg, b, r, ks: (0, hg, 0, 0)),
        ],
        out_specs=pl.BlockSpec((m, npr * LANES), lambda hg, b, r, ks: (b * nblk + r, hg)),
    )
    return pl.pallas_call(
        functools.partial(_nbr_attn_kernel, n_local=n_local, nblk=nblk),
        name="nbr_attention",
        grid_spec=grid_spec,
        out_shape=jax.ShapeDtypeStruct((n, n_pairs * LANES), BF16),
        compiler_params=_cparams(("arbitrary", "arbitrary", "arbitrary")),
    )(sched, q, k, v, ck, cv, bias)


def _attn_out_kernel(x_ref, o_ref, gt_ref, w_ref, y_ref):
    y_ref[...] = x_ref[...] + gt_ref[0, 0] * _dot(o_ref[...], w_ref[...])


def _attn_out(x2d, o, st, mod, w_bf16):
    n, d = x2d.shape
    tm = QKV_TILE
    tile = pl.BlockSpec((tm, d), lambda i: (i, 0))
    return pl.pallas_call(
        _attn_out_kernel,
        name="attn_out",
        grid=(n // tm,),
        in_specs=[tile, tile, mod.row_spec(0, 2, lambda i: st.cond_row(i, tm)),
                  pl.BlockSpec((d, d), lambda i: (0, 0))],
        out_specs=tile,
        out_shape=jax.ShapeDtypeStruct((n, d), F32),
        compiler_params=_cparams(("arbitrary",)),
    )(x2d, o, mod.t4, w_bf16)


def _lru_in_kernel(y_ref, sh_ref, sc_ref, g_ref, w_ref, gate_ref, xbr_ref, *, d_rnn, latent):
    bsz, tt, d = y_ref.shape
    x = jnp.concatenate([y_ref[:, t, :] for t in range(tt)], axis=0)
    if latent:
        x3 = x.reshape(tt, bsz, d)
        hn = _norm_mod(x3, g_ref[...][None], sc_ref[0][None], sh_ref[0][None]).reshape(tt * bsz, d)
    else:
        hn = _norm_mod(x, g_ref[...], sc_ref[0, 0], sh_ref[0, 0])
    hn = hn.astype(BF16)
    gate_ref[...] = _dot(hn, w_ref[:, :d_rnn])
    xbr_ref[...] = _dot(hn, w_ref[:, d_rnn:])


def _lru_in(y3, st, mod, g, w_bf16):
    bsz, seq, d = y3.shape
    d_rnn = w_bf16.shape[1] // 2
    tt = QKV_TILE // bsz
    assert (not st.latent) or bsz == SUBLANES
    mspec = (lambda c: mod.latent_spec(1, c)) if st.latent else (lambda c: mod.row_spec(1, c, lambda t: 0))
    out_spec = pl.BlockSpec((tt * bsz, d_rnn), lambda t: (t, 0))
    out_shape = jax.ShapeDtypeStruct((seq * bsz, d_rnn), F32)
    return pl.pallas_call(
        functools.partial(_lru_in_kernel, d_rnn=d_rnn, latent=st.latent),
        name="lru_in",
        grid=(seq // tt,),
        in_specs=[
            pl.BlockSpec((bsz, tt, d), lambda t: (0, t, 0)),
            mspec(0), mspec(1),
            pl.BlockSpec((1, d), lambda t: (0, 0)),
            pl.BlockSpec((d, 2 * d_rnn), lambda t: (0, 0)),
        ],
        out_specs=[out_spec, out_spec],
        out_shape=[out_shape, out_shape],
        compiler_params=_cparams(("arbitrary",)),
    )(y3, mod.t3 if st.latent else mod.t4, mod.t3 if st.latent else mod.t4, g, w_bf16)


def _scan_kernel(prev_ref, cur_ref, next_ref, cw_ref, cb_ref, wa_ref, wx_ref, ba_ref, bx_ref, lam_ref,
                 h0_ref, h_ref, hfin_ref, a_scr, b_scr, carry, *, batch, steps, nchunk):
    rev = pl.program_id(0) == 1
    c = pl.program_id(1)
    c_eff = jnp.where(rev, nchunk - 1 - c, c)
    rows = steps * batch
    cb = wa_ref.shape[2]
    for n in range(wa_ref.shape[1]):
        ch = slice(n * cb, (n + 1) * cb)
        prev = jnp.where(c_eff == 0, 0.0, prev_ref[:, ch])
        nxt = jnp.where(c_eff == nchunk - 1, 0.0, next_ref[:, ch])
        xfull = jnp.concatenate([prev, cur_ref[:, ch], nxt], axis=0)
        cw = cw_ref[:, ch]
        xc = cw[0:1, :] * xfull[0:rows]
        for j in range(1, cw.shape[0]):
            xc = xc + cw[j:j + 1, :] * xfull[j * batch:j * batch + rows]
        xc = xc + cb_ref[:, ch]
        xcb = xc.astype(BF16)
        t_r = jnp.tanh(0.5 * (_dot(xcb, wa_ref[0, n]) + ba_ref[0, :, ch]))
        t_i = jnp.tanh(0.5 * (_dot(xcb, wx_ref[0, n]) + bx_ref[0, :, ch]))
        neg_lam = -lam_ref[0, :, ch]
        softplus = jnp.maximum(neg_lam, 0.0) + jnp.log1p(jnp.exp(-jnp.abs(neg_lam)))
        log_a = ((-0.5 * LRU_C) * softplus) * (t_r + 1.0)
        a = jnp.exp(log_a)
        a_scr[:, ch] = a
        b_scr[:, ch] = jnp.sqrt(jnp.tanh(-log_a) * (1.0 + a * a)) * ((0.5 * xc) * (t_i + 1.0))

    @pl.when(c == 0)
    def _():
        carry[...] = h0_ref[0]

    def step(t, h):
        te = jnp.where(rev, steps - 1 - t, t)
        r0 = pl.multiple_of(te * batch, batch)
        h = a_scr[pl.ds(r0, batch), :] * h + b_scr[pl.ds(r0, batch), :]
        h_ref[0, pl.ds(r0, batch), :] = h
        return h

    h = lax.fori_loop(0, steps, step, carry[...], unroll=min(steps, 8))
    carry[...] = h

    @pl.when(c == nchunk - 1)
    def _():
        hfin_ref[0] = h


def _lru_scan(x2, batch, seq, conv_w, conv_b, wa_bf16, wx_bf16, ba, bx, lam, h0):
    d_rnn = conv_w.shape[1]
    nb, cb = wa_bf16.shape[1], wa_bf16.shape[2]
    assert batch % SUBLANES == 0 and conv_w.shape[0] == 4 and nb * cb == d_rnn
    steps = max(SCAN_ROWS // batch, 2)
    rows = steps * batch
    nchunk = seq // steps
    halo = 2 * batch
    hpc = rows // halo
    n_halo = seq * batch // halo
    ce = lambda d, c: jnp.where(d == 1, nchunk - 1 - c, c)
    vec = lambda a: a.reshape(2, 1, d_rnn)
    vspec = pl.BlockSpec((1, 1, d_rnn), lambda d, c: (d, 0, 0))
    wspec = pl.BlockSpec((1, nb, cb, cb), lambda d, c: (d, 0, 0, 0))
    return pl.pallas_call(
        functools.partial(_scan_kernel, batch=batch, steps=steps, nchunk=nchunk),
        name="lru_scan",
        grid=(2, nchunk),
        in_specs=[
            pl.BlockSpec((halo, d_rnn), lambda d, c: (jnp.maximum(ce(d, c) * hpc - 1, 0), 0)),
            pl.BlockSpec((rows, d_rnn), lambda d, c: (ce(d, c), 0)),
            pl.BlockSpec((halo, d_rnn), lambda d, c: (jnp.minimum((ce(d, c) + 1) * hpc, n_halo - 1), 0)),
            pl.BlockSpec((4, d_rnn), lambda d, c: (0, 0)),
            pl.BlockSpec((1, d_rnn), lambda d, c: (0, 0)),
            wspec, wspec, vspec, vspec, vspec,
            pl.BlockSpec((1, batch, d_rnn), lambda d, c: (d, 0, 0)),
        ],
        out_specs=[
            pl.BlockSpec((1, rows, d_rnn), lambda d, c: (d, ce(d, c), 0)),
            pl.BlockSpec((1, batch, d_rnn), lambda d, c: (d, 0, 0)),
        ],
        out_shape=[
            jax.ShapeDtypeStruct((2, seq * batch, d_rnn), F32),
            jax.ShapeDtypeStruct((2, batch, d_rnn), F32),
        ],
        scratch_shapes=[pltpu.VMEM((rows, d_rnn), F32), pltpu.VMEM((rows, d_rnn), F32),
                        pltpu.VMEM((batch, d_rnn), F32)],
        compiler_params=_cparams(("arbitrary", "arbitrary")),
    )(x2, x2, x2, conv_w, conv_b[None, :], wa_bf16, wx_bf16, vec(ba), vec(bx), vec(lam), h0)


def _lru_out_kernel(y_ref, gate_ref, h_ref, gt_ref, w_ref, o_ref, z_scr, *, latent):
    bsz, tt, d = y_ref.shape
    z = jax.nn.gelu(gate_ref[...]) * (h_ref[0] + h_ref[1])
    z_scr[...] = z.reshape(tt, bsz, z.shape[1])
    zb = jnp.concatenate([z_scr[:, b, :] for b in range(bsz)], axis=0)
    out = _dot(zb.astype(BF16), w_ref[...]).reshape(bsz, tt, d)
    gt = gt_ref[0][:, None, :] if latent else gt_ref[0]
    o_ref[...] = y_ref[...] + gt * out


def _lru_out(y3, gate, h, st, mod, w_bf16):
    bsz, seq, d = y3.shape
    d_rnn = w_bf16.shape[0]
    tt = QKV_TILE // bsz
    gspec = mod.latent_spec(1, 2) if st.latent else mod.row_spec(1, 2, lambda t: 0)
    return pl.pallas_call(
        functools.partial(_lru_out_kernel, latent=st.latent),
        name="lru_out",
        grid=(seq // tt,),
        in_specs=[
            pl.BlockSpec((bsz, tt, d), lambda t: (0, t, 0)),
            pl.BlockSpec((tt * bsz, d_rnn), lambda t: (t, 0)),
            pl.BlockSpec((2, tt * bsz, d_rnn), lambda t: (0, t, 0)),
            gspec,
            pl.BlockSpec((d_rnn, d), lambda t: (0, 0)),
        ],
        out_specs=pl.BlockSpec((bsz, tt, d), lambda t: (0, t, 0)),
        out_shape=jax.ShapeDtypeStruct((bsz, seq, d), F32),
        scratch_shapes=[pltpu.VMEM((tt, bsz, d_rnn), F32)],
        compiler_params=_cparams(("arbitrary",)),
    )(y3, gate, h, mod.t3 if st.latent else mod.t4, w_bf16)


def _dual_specs(shape, sp, ss, extra=0):
    pmap = lambda i, *_: (jnp.minimum(i, sp.tiles - 1), 0)
    smap = lambda i, *_: (jnp.maximum(i - sp.tiles, 0), 0)
    return pl.BlockSpec(shape, pmap), pl.BlockSpec(shape, smap)


def _joint_cond_row(sp, ss):
    return lambda i, *_: jnp.where(i < sp.tiles, 0, ss.cond_row(jnp.maximum(i - sp.tiles, 0)))


def _router_kernel(yp_ref, ys_ref, sh_ref, sc_ref, g_ref, rwt_ref, rb_ref, bkt_ref, rank_ref, wext_ref,
                   cnt_ref, carry, *, n_experts, ntp, n_steps):
    i = pl.program_id(0)

    @pl.when(i == 0)
    def _():
        carry[...] = jnp.zeros_like(carry)

    y = jnp.where(i < ntp, yp_ref[...], ys_ref[...])
    hn = _norm_mod(y, g_ref[...], sc_ref[0, 0], sh_ref[0, 0])
    tm = hn.shape[0]
    logits = lax.dot_general(rwt_ref[...], hn, (((1,), (1,)), ((), ())),
                             precision=lax.Precision.HIGHEST, preferred_element_type=F32)
    scores = jax.nn.sigmoid(logits)
    sel = scores + rb_ref[...]
    per = n_experts // N_GROUPS
    assert per == 4
    srow = [sel[e:e + 1, :] for e in range(n_experts)]
    prow = [scores[e:e + 1, :] for e in range(n_experts)]

    def top2_sum(a, b, c, d):
        hi1, lo1, hi2, lo2 = jnp.maximum(a, b), jnp.minimum(a, b), jnp.maximum(c, d), jnp.minimum(c, d)
        return jnp.maximum(hi1, hi2) + jnp.maximum(jnp.minimum(hi1, hi2), jnp.maximum(lo1, lo2))

    gscore = [top2_sum(*srow[per * g:per * (g + 1)]) for g in range(N_GROUPS)]
    best = jnp.zeros((1, tm), I32)
    best_v = gscore[0]
    for g in range(1, N_GROUPS):
        upd = gscore[g] > best_v
        best = jnp.where(upd, g, best)
        best_v = jnp.where(upd, gscore[g], best_v)
    vs = list(srow[:per])
    ps = list(prow[:per])
    for g in range(1, N_GROUPS):
        for j in range(per):
            vs[j] = jnp.where(best == g, srow[per * g + j], vs[j])
            ps[j] = jnp.where(best == g, prow[per * g + j], ps[j])
    j1 = jnp.zeros((1, tm), I32)
    m1 = vs[0]
    for j in range(1, per):
        upd = vs[j] > m1
        j1 = jnp.where(upd, j, j1)
        m1 = jnp.where(upd, vs[j], m1)
    j2 = jnp.full((1, tm), -1, I32)
    m2 = jnp.full((1, tm), -jnp.inf, F32)
    for j in range(per):
        upd = (j1 != j) & (vs[j] > m2)
        j2 = jnp.where(upd, j, j2)
        m2 = jnp.where(upd, vs[j], m2)
    s1 = jnp.zeros((1, tm), F32)
    s2 = jnp.zeros((1, tm), F32)
    for j in range(per):
        s1 = jnp.where(j1 == j, ps[j], s1)
        s2 = jnp.where(j2 == j, ps[j], s2)
    den = s1 + s2
    w1, w2 = s1 / den, s2 / den
    lo = jnp.minimum(j1, j2)
    hi = jnp.maximum(j1, j2)
    w_lo = jnp.where(j1 < j2, w1, w2)
    w_hi = jnp.where(j1 < j2, w2, w1)
    base = jnp.where(lo == 0, 0, jnp.where(lo == 1, 3, 5))
    bucket = best * 6 + base + hi - lo - 1

    onehot = (lax.broadcasted_iota(I32, (N_BUCKET_SLOTS, tm), 0) == bucket).astype(F32)
    before = (lax.broadcasted_iota(I32, (tm, tm), 0) < lax.broadcasted_iota(I32, (tm, tm), 1)).astype(BF16)
    prefix = _dot(onehot.astype(BF16), before) + carry[...]
    rank = jnp.sum(onehot * prefix, axis=0, keepdims=True)
    carry[...] = carry[...] + jnp.sum(onehot, axis=1, keepdims=True)

    bkt_ref[0] = bucket
    rank_ref[0] = rank.astype(I32)
    slab_row = lax.broadcasted_iota(I32, (LANES, tm), 0)
    slab = jnp.where(slab_row == 0, w_lo, jnp.where(slab_row == 1, w_hi, 0.0))
    wext_ref[...] = slab.T

    @pl.when(i == n_steps - 1)
    def _():
        cnt_ref[...] = jnp.broadcast_to(carry[...], cnt_ref.shape).astype(I32)


def _router(yp, ys, sp, ss, mod, layer, g, router_wt, router_b):
    d = yp.shape[1]
    e = router_wt.shape[0]
    tm = TOKEN_TILE
    nt = sp.tiles + ss.tiles
    row = _joint_cond_row(sp, ss)
    tok_spec = pl.BlockSpec((1, 1, tm), lambda i: (i, 0, 0))
    pspec, sspec = _dual_specs((tm, d), sp, ss)
    return pl.pallas_call(
        functools.partial(_router_kernel, n_experts=e, ntp=sp.tiles, n_steps=nt),
        name="moe_router",
        grid=(nt,),
        in_specs=[
            pspec, sspec,
            mod.row_spec(layer, 3, row),
            mod.row_spec(layer, 4, row),
            pl.BlockSpec((1, d), lambda i: (0, 0)),
            pl.BlockSpec((e, d), lambda i: (0, 0)),
            pl.BlockSpec((e, 1), lambda i: (0, 0)),
        ],
        out_specs=[tok_spec, tok_spec,
                   pl.BlockSpec((tm, LANES), lambda i: (i, 0)),
                   pl.BlockSpec((N_BUCKET_SLOTS, LANES), lambda i: (0, 0))],
        out_shape=[jax.ShapeDtypeStruct((nt, 1, tm), I32), jax.ShapeDtypeStruct((nt, 1, tm), I32),
                   jax.ShapeDtypeStruct((nt * tm, LANES), F32),
                   jax.ShapeDtypeStruct((N_BUCKET_SLOTS, LANES), I32)],
        scratch_shapes=[pltpu.VMEM((N_BUCKET_SLOTS, 1), F32)],
        compiler_params=_cparams(("arbitrary",)),
    )(yp, ys, mod.t4, mod.t4, g, router_wt, router_b[:, None])


def _bucket_layout(counts, n_tokens):
    n_buckets = N_GROUPS * len(PAIR_LO)
    tf = FFN_TILE
    n_tiles = n_tokens // tf + n_buckets + 2
    c = counts[:n_buckets]
    padded = ((c + tf - 1) // tf) * tf
    ends = jnp.cumsum(padded)
    starts = ends - padded
    n_used = ends[-1] // tf
    tile = jnp.arange(n_tiles, dtype=I32)
    valid = tile < n_used
    tile_c = jnp.minimum(tile, n_used - 1)
    tb = jnp.sum((ends[None, :] <= (tile_c * tf)[:, None]).astype(I32), axis=1)
    tb = jnp.minimum(tb, n_buckets - 1)
    grp, pair = tb // len(PAIR_LO), tb % len(PAIR_LO)
    pair_lo = sum(jnp.where(pair == p, PAIR_LO[p], 0) for p in range(len(PAIR_LO)))
    pair_hi = sum(jnp.where(pair == p, PAIR_HI[p], 0) for p in range(len(PAIR_HI)))
    slots = lambda v: jnp.concatenate([v.astype(I32), jnp.zeros((N_BUCKET_SLOTS - n_buckets,), I32)])
    sched = jnp.stack([tile_c, grp * 4 + pair_lo, grp * 4 + pair_hi, valid.astype(I32)]).astype(I32)
    fill = slots((starts + c) // SUBLANES).at[n_buckets].set(n_used.astype(I32))
    return slots(starts), fill, sched, n_tiles


def _pos_kernel(starts_ref, bkt_ref, rank_ref, pos_ref):
    bkt = bkt_ref[...]
    pos = rank_ref[...]
    for b in range(N_GROUPS * len(PAIR_LO)):
        pos = pos + jnp.where(bkt == b, starts_ref[b], 0)
    pos_ref[...] = pos


def _positions(starts32, bkt, rank):
    nt, _, tm = bkt.shape
    grid_spec = pltpu.PrefetchScalarGridSpec(
        num_scalar_prefetch=1,
        grid=(1,),
        in_specs=[pl.BlockSpec((nt, 1, tm), lambda i, st: (0, 0, 0))] * 2,
        out_specs=pl.BlockSpec((nt, 1, tm), lambda i, st: (0, 0, 0)),
    )
    return pl.pallas_call(
        _pos_kernel, name="moe_positions", grid_spec=grid_spec,
        out_shape=jax.ShapeDtypeStruct((nt, 1, tm), I32),
        compiler_params=_cparams(("arbitrary",)),
    )(starts32, bkt, rank)


def _dispatch_kernel(fill_ref, yp_ref, ys_ref, sh_ref, sc_ref, g_ref, wext_ref, pos_ref,
                     xs_ref, buf0, buf1, zeros, inv_rms, sem0, sem1, *, d, ntp, n_steps, n_buckets):
    groups, sub, width = buf0.shape
    i = pl.program_id(0)

    @pl.when(i == 0)
    def _():
        zgroups = zeros.shape[0] - 1
        zeros[...] = jnp.zeros_like(zeros)
        fills = [pltpu.make_async_copy(zeros, xs_ref.at[pl.ds(fill_ref[b], zgroups + 1)], sem1)
                 for b in range(n_buckets)]
        for copy in fills:
            copy.start()
        for copy in fills:
            copy.wait()

        def clear_tile(t, carry):
            copy = pltpu.make_async_copy(zeros.at[pl.ds(0, zgroups)], xs_ref.at[pl.ds(t * zgroups, zgroups)], sem1)
            copy.start()
            copy.wait()
            return carry

        lax.fori_loop(fill_ref[n_buckets], xs_ref.shape[0] // zgroups, clear_tile, 0)

    def wait_tile(buf, sem):
        pltpu.make_async_copy(buf, xs_ref.at[pl.ds(0, groups)], sem).wait()

    def step(buf, sem, prev_buf, prev_sem):
        y_tile = jnp.where(i < ntp, yp_ref[...], ys_ref[...])
        inv_rms[...] = jnp.broadcast_to(lax.rsqrt(jnp.mean(y_tile * y_tile, axis=-1, keepdims=True) + EPS),
                                        inv_rms.shape)
        gain = g_ref[...]
        scale = 1.0 + sc_ref[0, 0]
        shift = sh_ref[0, 0]

        def issue(g, carry):
            r0 = pl.multiple_of(g * sub, sub)
            y = jnp.where(i < ntp, yp_ref[pl.ds(r0, sub), :], ys_ref[pl.ds(r0, sub), :])
            inv = inv_rms[pl.ds(r0, sub), :]
            for c in range(d // LANES):
                cols = slice(c * LANES, (c + 1) * LANES)
                buf[g, :, cols] = ((y[:, cols] * inv) * gain[:, cols]) * scale[:, cols] + shift[:, cols]
            buf[g, :, d:] = wext_ref[pl.ds(r0, sub), :]
            for u in range(sub):
                pos = pos_ref[0, 0, r0 + u]
                dst = xs_ref.at[lax.shift_right_logical(pos, 3), pl.ds(pos & (sub - 1), 1)]
                pltpu.make_async_copy(buf.at[g, pl.ds(u, 1)], dst, sem).start()
            return carry

        lax.fori_loop(0, groups, issue, 0)

        @pl.when(i > 0)
        def _():
            wait_tile(prev_buf, prev_sem)

        @pl.when(i == n_steps - 1)
        def _():
            wait_tile(buf, sem)

    @pl.when(i % 2 == 0)
    def _():
        step(buf0, sem0, buf1, sem1)

    @pl.when(i % 2 == 1)
    def _():
        step(buf1, sem1, buf0, sem0)


def _dispatch(yp, ys, sp, ss, mod, layer, g, wext, pos, fill_groups, n_rows):
    d = yp.shape[1]
    tm = TOKEN_TILE
    width = d + LANES
    groups = tm // SUBLANES
    row = _joint_cond_row(sp, ss)
    pspec, sspec = _dual_specs((tm, d), sp, ss)
    tile_buf = pltpu.VMEM((groups, SUBLANES, width), F32)
    grid_spec = pltpu.PrefetchScalarGridSpec(
        num_scalar_prefetch=1,
        grid=(sp.tiles + ss.tiles,),
        in_specs=[
            pspec, sspec,
            mod.row_spec(layer, 3, row), mod.row_spec(layer, 4, row),
            pl.BlockSpec((1, d), lambda i, f: (0, 0)),
            pl.BlockSpec((tm, LANES), lambda i, f: (i, 0)),
            pl.BlockSpec((1, 1, tm), lambda i, f: (i, 0, 0), memory_space=pltpu.SMEM),
        ],
        out_specs=pl.BlockSpec(memory_space=pl.ANY),
        scratch_shapes=[tile_buf, tile_buf, pltpu.VMEM((FFN_TILE // SUBLANES + 1, SUBLANES, width), F32),
                        pltpu.VMEM((tm, LANES), F32),
                        pltpu.SemaphoreType.DMA(()), pltpu.SemaphoreType.DMA(())],
    )
    return pl.pallas_call(
        functools.partial(_dispatch_kernel, d=d, ntp=sp.tiles, n_steps=sp.tiles + ss.tiles,
                          n_buckets=N_GROUPS * len(PAIR_LO)),
        name="moe_dispatch",
        grid_spec=grid_spec,
        out_shape=jax.ShapeDtypeStruct((n_rows // SUBLANES, SUBLANES, width), F32),
        compiler_params=_cparams(("arbitrary",), disable_bounds_checks=True),
    )(fill_groups, yp, ys, mod.t4, mod.t4, g, wext, pos).reshape(n_rows, width)


def _ffn_kernel(sched_ref, xs_ref, wg_lo, wu_lo, wd_lo, wg_hi, wu_hi, wd_hi, o_ref, *, d):
    valid = sched_ref[3, pl.program_id(0)] == 1

    @pl.when(jnp.logical_not(valid))
    def _():
        o_ref[...] = jnp.zeros_like(o_ref)

    @pl.when(valid)
    def _():
        x = xs_ref[:, :d].astype(BF16)

        gates = [_dot(x, wg[0, 0]) for wg in (wg_lo, wg_hi)]
        ups = [_dot(x, wu[0, 0]) for wu in (wu_lo, wu_hi)]
        hes = [((g * jax.nn.sigmoid(g)) * u).astype(BF16) for g, u in zip(gates, ups)]
        w_lo = xs_ref[:, d:d + 1]
        w_hi = xs_ref[:, d + 1:d + 2]
        o_ref[...] = w_lo * _dot(hes[0], wd_lo[0, 0]) + w_hi * _dot(hes[1], wd_hi[0, 0])


def _expert_ffn(xs, sched, n_tiles, layer, wg, wu, wd):
    n_rows, width = xs.shape
    d = width - LANES
    f = wg.shape[3]
    tf = FFN_TILE
    lo = lambda i, s: (layer, s[1, i], 0, 0)
    hi = lambda i, s: (layer, s[2, i], 0, 0)
    grid_spec = pltpu.PrefetchScalarGridSpec(
        num_scalar_prefetch=1,
        grid=(n_tiles,),
        in_specs=[
            pl.BlockSpec((tf, width), lambda i, s: (s[0, i], 0)),
            pl.BlockSpec((1, 1, d, f), lo), pl.BlockSpec((1, 1, d, f), lo), pl.BlockSpec((1, 1, f, d), lo),
            pl.BlockSpec((1, 1, d, f), hi), pl.BlockSpec((1, 1, d, f), hi), pl.BlockSpec((1, 1, f, d), hi),
        ],
        out_specs=pl.BlockSpec((tf, d), lambda i, s: (i, 0)),
    )
    return pl.pallas_call(
        functools.partial(_ffn_kernel, d=d),
        name="moe_expert_ffn",
        grid_spec=grid_spec,
        out_shape=jax.ShapeDtypeStruct((n_rows, d), F32),
        compiler_params=_cparams(("arbitrary",)),
    )(sched, xs, wg, wu, wd, wg, wu, wd)


def _combine_kernel(y_ref, gt_ref, pos_ref, pos_next_ref, ms_ref, ms_groups_ref, o_ref, buf0, buf1, sem0, sem1, *,
                    n_steps):
    groups, sub, d = buf0.shape
    i = pl.program_id(0)

    def start_rows(pos, g, buf, sem):
        for u in range(sub):
            pltpu.make_async_copy(ms_ref.at[pl.ds(pos[0, 0, g * sub + u], 1)], buf.at[g, pl.ds(u, 1)],
                                  sem).start()

    def wait_tile(buf, sem):
        pltpu.make_async_copy(ms_groups_ref.at[pl.ds(0, groups)], buf, sem).wait()

    def step(buf, sem, next_buf, next_sem):
        @pl.when(i == 0)
        def _():
            lax.fori_loop(0, groups, lambda g, c: (start_rows(pos_ref, g, buf, sem), c)[1], 0)

        wait_tile(buf, sem)
        gt = gt_ref[0, 0]

        def body(g, carry):
            start_rows(pos_next_ref, g, next_buf, next_sem)
            r0 = pl.multiple_of(g * sub, sub)
            o_ref[pl.ds(r0, sub), :] = y_ref[pl.ds(r0, sub), :] + gt * buf[g]
            return carry

        lax.fori_loop(0, groups, body, 0)

        @pl.when(i == n_steps - 1)
        def _():
            wait_tile(next_buf, next_sem)

    @pl.when(i % 2 == 0)
    def _():
        step(buf0, sem0, buf1, sem1)

    @pl.when(i % 2 == 1)
    def _():
        step(buf1, sem1, buf0, sem0)


def _combine(y2d, st, mod, layer, pos, ms):
    n, d = y2d.shape
    tm = TOKEN_TILE
    tile = pl.BlockSpec((tm, d), lambda i: (i, 0))
    last = st.tile_off + n // tm - 1
    return pl.pallas_call(
        functools.partial(_combine_kernel, n_steps=n // tm),
        name="moe_combine",
        grid=(n // tm,),
        in_specs=[
            tile,
            mod.row_spec(layer, 5, st.cond_row),
            pl.BlockSpec((1, 1, tm), lambda i: (st.tile_off + i, 0, 0), memory_space=pltpu.SMEM),
            pl.BlockSpec((1, 1, tm), lambda i: (jnp.minimum(st.tile_off + i + 1, last), 0, 0),
                         memory_space=pltpu.SMEM),
            pl.BlockSpec(memory_space=pl.ANY),
            pl.BlockSpec(memory_space=pl.ANY),
        ],
        out_specs=tile,
        out_shape=jax.ShapeDtypeStruct((n, d), F32),
        scratch_shapes=[pltpu.VMEM((tm // SUBLANES, SUBLANES, d), F32),
                        pltpu.VMEM((tm // SUBLANES, SUBLANES, d), F32),
                        pltpu.SemaphoreType.DMA(()), pltpu.SemaphoreType.DMA(())],
        compiler_params=_cparams(("arbitrary",), disable_bounds_checks=True),
    )(y2d, mod.t4, pos, pos, ms, ms.reshape(ms.shape[0] // SUBLANES, SUBLANES, d))


def _moe_residual(yp, ys, sp, ss, mod, layer, g, router_wt, router_b, wg, wu, wd):
    bkt, rank, wext, counts = _router(yp, ys, sp, ss, mod, layer, g, router_wt, router_b)
    starts32, fill_groups, sched, n_tiles = _bucket_layout(counts[:, 0], sp.n + ss.n)
    pos = _positions(starts32, bkt, rank)
    xs = _dispatch(yp, ys, sp, ss, mod, layer, g, wext, pos, fill_groups, n_tiles * FFN_TILE)
    ms = _expert_ffn(xs, sched, n_tiles, layer, wg, wu, wd)
    return _combine(yp, sp, mod, layer, pos, ms), _combine(ys, ss, mod, layer, pos, ms)


def kernel(x_prompt, x_sample, c, cache_k, cache_v, state_lru, c_ctx, ada_w, ada_b, norm_mix, norm_ffn,
           attn_w_in, attn_q_gain, attn_k_gain, attn_rpb, attn_w_out, lru_w_in, lru_conv_w, lru_conv_b,
           lru_gate_a_w, lru_gate_a_b, lru_gate_x_w, lru_gate_x_b, lru_lambda, lru_w_out, router_w,
           router_b, moe_w_gate, moe_w_up, moe_w_down):
    bp, tp, d = x_prompt.shape
    bs, ts, _ = x_sample.shape
    n_heads, head_dim = cache_k.shape[3], cache_k.shape[4]
    d_rnn = lru_conv_w.shape[2]
    assert ada_w.shape[0] == 2 and bs <= N_COND_ROWS - LATENT_COND_ROW
    assert tp % TOKEN_TILE == 0 and ts % TOKEN_TILE == 0
    sp = _Stream(bp, tp, 0, latent=False)
    ss = _Stream(bs, ts, sp.tiles, latent=True)

    cond = jnp.zeros((N_COND_ROWS, d), F32).at[0].set(c_ctx).at[LATENT_COND_ROW:LATENT_COND_ROW + bs].set(c)
    mod = _Mod(_ada_modulation(cond, ada_w, ada_b), d)

    w_in = attn_w_in[0].astype(BF16)
    w_out = attn_w_out[0].astype(BF16)
    g_mix0 = norm_mix[0][None, :]
    xp2, xs2 = x_prompt.reshape(sp.n, d), x_sample.reshape(ss.n, d)
    q_p, k_p, v_p, new_k, new_v = _qkv_proj(xp2, sp, mod, g_mix0, w_in, attn_q_gain[0], attn_k_gain[0],
                                            n_heads, True)
    q_s, k_s, v_s = _qkv_proj(xs2, ss, mod, g_mix0, w_in, attn_q_gain[0], attn_k_gain[0], n_heads, False)
    o_p = _ctx_attention(q_p, k_p, v_p, bp, tp)
    pair_major = lambda cache: jnp.transpose(
        cache[:, 0].astype(BF16).reshape(bs, cache.shape[2], n_heads // 2, LANES), (2, 0, 1, 3))
    ck, cv = pair_major(cache_k), pair_major(cache_v)
    o_s = _nbr_attention(q_s, k_s, v_s, ck, cv, attn_rpb[0], bs, ts)
    yp = _attn_out(xp2, o_p, sp, mod, w_out)
    ys = _attn_out(xs2, o_s, ss, mod, w_out)

    router_wt = router_w.T
    moe_w = (moe_w_gate.astype(BF16), moe_w_up.astype(BF16), moe_w_down.astype(BF16))
    yp, ys = _moe_residual(yp, ys, sp, ss, mod, 0, norm_ffn[0][None, :], router_wt, router_b, *moe_w)

    w_lru_in = lru_w_in[0].astype(BF16)
    w_lru_out = lru_w_out[0].astype(BF16)
    g_mix1 = norm_mix[1][None, :]
    scan_args = (lru_conv_w[0], lru_conv_b[0], lru_gate_a_w[0].astype(BF16), lru_gate_x_w[0].astype(BF16),
                 lru_gate_a_b[0], lru_gate_x_b[0], lru_lambda[0])
    hfin = {}
    ys_out = []
    for y2, st, h0 in ((yp, sp, jnp.zeros((2, bp, d_rnn), F32)),
                       (ys, ss, jnp.transpose(state_lru[:, 0], (1, 0, 2)))):
        y3 = y2.reshape(st.batch, st.seq, d)
        gate, xbr = _lru_in(y3, st, mod, g_mix1, w_lru_in)
        h, hfin[st.latent] = _lru_scan(xbr, st.batch, st.seq, *scan_args, h0)
        ys_out.append(_lru_out(y3, gate, h, st, mod, w_lru_out).reshape(st.n, d))
    yp, ys = ys_out
    yp, ys = _moe_residual(yp, ys, sp, ss, mod, 1, norm_ffn[1][None, :], router_wt, router_b, *moe_w)

    return (yp.reshape(bp, tp, d), ys.reshape(bs, ts, d),
            new_k.reshape(bp, 1, tp, n_heads, head_dim), new_v.reshape(bp, 1, tp, n_heads, head_dim),
            jnp.transpose(hfin[False], (1, 0, 2))[:, None])
```

```python
import functools

import numpy as np
import jax
import jax.numpy as jnp
from jax import lax
from jax.experimental import pallas as pl
from jax.experimental.pallas import tpu as pltpu

F32 = jnp.float32
BF16 = jnp.bfloat16
I32 = jnp.int32

EPS = 1e-6
NEG_INF = -1e30
GRID_W = 64
WIN_ROWS = 8
WIN_COLS = 16
N_GROUPS = 4
LRU_C = 8.0
N_LRU_BLOCKS = 4

LANES = 128
SUBLANES = 8
MXU_COLS = 256
TOKEN_TILE = 256
QKV_TILE = 512
FFN_TILE = 256
Q_ROWS = 4
SCAN_ROWS = 512
ATTN_PAIRS = 2
N_BUCKET_SLOTS = 32
PAIR_LO = (0, 0, 0, 1, 1, 2)
PAIR_HI = (1, 2, 3, 2, 3, 3)
N_COND_ROWS = 16
LATENT_COND_ROW = 8
VMEM_LIMIT = 48 * 1024 * 1024

def _cparams(sem, **kw):
    return pltpu.CompilerParams(dimension_semantics=sem, vmem_limit_bytes=VMEM_LIMIT, **kw)


def _norm_mod(x, g, sc, sh):
    ms = jnp.mean(x * x, axis=-1, keepdims=True)
    return (x * lax.rsqrt(ms + EPS)) * g * (1.0 + sc) + sh


def _dot(a, b):
    return jnp.dot(a, b, preferred_element_type=F32)


def _dot_nt(a, b):
    return lax.dot_general(a, b, (((1,), (1,)), ((), ())), preferred_element_type=F32)


def _ada_kernel(cond_ref, w_ref, b_ref, o_ref):
    c = cond_ref[...]
    s = (c * jax.nn.sigmoid(c)).astype(BF16)
    o_ref[0] = _dot(s, w_ref[0].astype(BF16)) + b_ref[0]


def _ada_modulation(cond, ada_w, ada_b):
    depth, d, d6 = ada_w.shape
    tn = d6 // 4
    return pl.pallas_call(
        _ada_kernel,
        name="ada_modulation",
        grid=(depth, d6 // tn),
        in_specs=[
            pl.BlockSpec((N_COND_ROWS, d), lambda l, j: (0, 0)),
            pl.BlockSpec((1, d, tn), lambda l, j: (l, 0, j)),
            pl.BlockSpec((1, 1, tn), lambda l, j: (l, 0, j)),
        ],
        out_specs=pl.BlockSpec((1, N_COND_ROWS, tn), lambda l, j: (l, 0, j)),
        out_shape=jax.ShapeDtypeStruct((depth, N_COND_ROWS, d6), F32),
        compiler_params=_cparams(("arbitrary", "arbitrary")),
    )(cond, ada_w, ada_b.reshape(depth, 1, d6))


class _Mod:
    def __init__(self, table, d):
        self.t3 = table
        self.t4 = table.reshape(table.shape[0], N_COND_ROWS, 1, table.shape[2])
        self.d = d

    def row_spec(self, layer, chunk, row_fn):
        return pl.BlockSpec((1, 1, 1, self.d), lambda *ids: (layer, row_fn(*ids), 0, chunk))

    def latent_spec(self, layer, chunk):
        return pl.BlockSpec((1, SUBLANES, self.d), lambda *ids: (layer, LATENT_COND_ROW // SUBLANES, chunk))


class _Stream:
    def __init__(self, batch, seq, tile_off, latent):
        self.batch, self.seq, self.tile_off, self.latent = batch, seq, tile_off, latent
        self.n = batch * seq
        self.tiles = self.n // TOKEN_TILE

    def cond_row(self, i, tile=TOKEN_TILE):
        if not self.latent:
            return 0
        return LATENT_COND_ROW + (i * tile) // self.seq


def _qkv_kernel(x_ref, sh_ref, sc_ref, g_ref, w_ref, qg_ref, kg_ref, gmean_ref, q_ref, k_ref, v_ref, *kv_refs,
                d, head_dim):
    hn = _norm_mod(x_ref[...], g_ref[...], sc_ref[0, 0], sh_ref[0, 0]).astype(BF16)
    outs = (q_ref, k_ref, v_ref)
    gains = (qg_ref, kg_ref)
    n_chunks = 3 * d // MXU_COLS
    n_norm = 2 * d // MXU_COLS
    accs = [_dot(hn, w_ref[:, MXU_COLS * j:MXU_COLS * (j + 1)]) for j in range(n_chunks)]
    means = []
    for j in range(n_norm):
        sq = accs[j] * accs[j]
        hi = sq.astype(BF16)
        lo = (sq - hi.astype(F32)).astype(BF16)
        means.append(_dot(hi, gmean_ref[...]) + _dot(lo, gmean_ref[...]))
    for j in range(n_chunks):
        part = (MXU_COLS * j) // d
        col = (MXU_COLS * j) % d
        acc = accs[j]
        if j < n_norm:
            acc = (acc * lax.rsqrt(means[j] + EPS)) * gains[part][...]
        for half in range(MXU_COLS // LANES):
            outs[part][col // LANES + half] = acc[:, LANES * half:LANES * (half + 1)].astype(BF16)
        if part >= 1 and kv_refs:
            kv_refs[part - 1][:, col:col + MXU_COLS] = acc


def _qkv_proj(x2d, st, mod, g, w_bf16, q_gain, k_gain, n_heads, with_kv):
    n, d = x2d.shape
    head_dim = d // n_heads
    assert head_dim * 2 == LANES
    tm = QKV_TILE
    n_pairs = n_heads // 2
    hspec = pl.BlockSpec((n_pairs, tm, LANES), lambda i: (0, i, 0))
    hshape = jax.ShapeDtypeStruct((n_pairs, n, LANES), BF16)
    out_specs = [hspec, hspec, hspec]
    out_shape = [hshape, hshape, hshape]
    if with_kv:
        out_specs += [pl.BlockSpec((tm, d), lambda i: (i, 0))] * 2
        out_shape += [jax.ShapeDtypeStruct((n, d), F32)] * 2
    lane_head = np.arange(MXU_COLS) // head_dim
    assert float(np.log2(head_dim)).is_integer()
    gmean = jnp.asarray((lane_head[:, None] == lane_head[None, :]) / head_dim, BF16)
    reps = MXU_COLS // head_dim
    qg = jnp.tile(q_gain * _attn_scale(head_dim), reps)[None, :]
    kg = jnp.tile(k_gain, reps)[None, :]
    const = lambda shape: pl.BlockSpec(shape, lambda i: (0,) * len(shape))
    return pl.pallas_call(
        functools.partial(_qkv_kernel, d=d, head_dim=head_dim),
        name="qkv_proj",
        grid=(n // tm,),
        in_specs=[
            pl.BlockSpec((tm, d), lambda i: (i, 0)),
            mod.row_spec(0, 0, lambda i: st.cond_row(i, tm)),
            mod.row_spec(0, 1, lambda i: st.cond_row(i, tm)),
            const((1, d)), const((d, 3 * d)), const((1, MXU_COLS)), const((1, MXU_COLS)),
            const((MXU_COLS, MXU_COLS)),
        ],
        out_specs=out_specs,
        out_shape=out_shape,
        compiler_params=_cparams(("arbitrary",)),
    )(x2d, mod.t4, mod.t4, g, w_bf16, qg, kg, gmean)


def _attn_scale(head_dim):
    scale = head_dim ** -0.5
    assert float(np.log2(scale)).is_integer()
    return scale


def _softmax_pv(scores, values):
    m = scores[0].max(axis=-1, keepdims=True)
    for s in scores[1:]:
        m = jnp.maximum(m, s.max(axis=-1, keepdims=True))
    es = [jnp.exp(s - m) for s in scores]
    l = es[0].sum(axis=-1, keepdims=True)
    for e in es[1:]:
        l = l + e.sum(axis=-1, keepdims=True)
    o = _dot(es[0].astype(BF16), values[0])
    for e, v in zip(es[1:], values[1:]):
        o = o + _dot(e.astype(BF16), v)
    return o / l


def _pair_attention(q2, score_keys, values, biases):
    first = lax.broadcasted_iota(I32, q2.shape, 1) < q2.shape[1] // 2
    zero = jnp.zeros_like(q2)
    outs = []
    for head in range(2):
        qh = jnp.where(first, q2, zero) if head == 0 else jnp.where(first, zero, q2)
        scores = []
        for keys, bias in zip(score_keys, biases[head]):
            s = _dot_nt(qh, keys)
            scores.append(s if bias is None else s + bias)
        outs.append(_softmax_pv(scores, values))
    return jnp.where(first, outs[0], outs[1])


def _ctx_attn_kernel(q_ref, k_ref, v_ref, o_ref):
    for hp in range(q_ref.shape[0]):
        o = _pair_attention(q_ref[hp], [k_ref[hp]], [v_ref[hp]], [[None], [None]])
        o_ref[:, LANES * hp:LANES * (hp + 1)] = o.astype(BF16)


def _ctx_attention(q, k, v, batch, seq):
    n_pairs, n, _ = q.shape
    spec = pl.BlockSpec((n_pairs, seq, LANES), lambda b: (0, b, 0))
    return pl.pallas_call(
        _ctx_attn_kernel,
        name="ctx_attention",
        grid=(batch,),
        in_specs=[spec, spec, spec],
        out_specs=pl.BlockSpec((seq, n_pairs * LANES), lambda b: (b, 0)),
        out_shape=jax.ShapeDtypeStruct((n, n_pairs * LANES), BF16),
        compiler_params=_cparams(("arbitrary",)),
    )(q, k, v)


def _nbr_geometry(rows):
    n_key_rows = Q_ROWS + WIN_ROWS
    assert rows % Q_ROWS == 0 and rows >= n_key_rows
    nblk = rows // Q_ROWS
    kstart = np.clip(np.arange(nblk) * Q_ROWS - WIN_ROWS // 2, 0, rows - n_key_rows)
    start = np.clip(np.arange(rows) - WIN_ROWS // 2, 0, rows - WIN_ROWS)
    tables, variant = [], []
    for blk in range(nblk):
        r = blk * Q_ROWS + np.arange(Q_ROWS)
        kr = kstart[blk] + np.arange(n_key_rows)
        valid = (kr[None, :] >= start[r][:, None]) & (kr[None, :] < start[r][:, None] + WIN_ROWS)
        dr = np.clip(kr[None, :] - r[:, None] + WIN_ROWS - 1, 0, 2 * WIN_ROWS - 2)
        key = (valid.tobytes(), dr.tobytes())
        keys = [t[0] for t in tables]
        if key not in keys:
            tables.append((key, valid, dr))
        variant.append([t[0] for t in tables].index(key))
    return nblk, n_key_rows, kstart, np.array(variant), [(t[1], t[2]) for t in tables]


def _nbr_bias(rpb, tables):
    h, nd, nc = rpb.shape
    w = GRID_W
    period = 2 * w
    pad_l = (w - 1) - (WIN_COLS - 1)
    r_ext = jnp.pad(rpb.astype(F32), ((0, 0), (0, 0), (pad_l, period - nc - pad_l)))
    toe = jnp.tile(r_ext, (1, 1, w))[:, :, :w * (period - 1)].reshape(h, nd, w, period - 1)
    toe = toe[..., w - 1:2 * w - 1]
    cw = np.arange(w)
    col_start = np.clip(cw - WIN_COLS // 2, 0, w - WIN_COLS)
    col_ok = (cw[None, :] >= col_start[:, None]) & (cw[None, :] < col_start[:, None] + WIN_COLS)
    toe = jnp.where(col_ok[None, None], toe, NEG_INF)
    tq = jnp.transpose(toe, (0, 2, 1, 3)).reshape(h, w, nd * w)
    segments = []
    for valid, dr in tables:
        seg = []
        for i in range(valid.shape[0]):
            js = np.nonzero(valid[i])[0]
            j0, j1 = int(js[0]), int(js[-1]) + 1
            assert (np.diff(dr[i, j0:j1]) == 1).all() and valid[i, j0:j1].all()
            seg.append((j0, j1, int(dr[i, j0])))
        segments.append(seg)
    qr, kr = tables[0][0].shape
    return pl.pallas_call(
        functools.partial(_bias_kernel, segments=segments, w=w),
        name="nbr_bias",
        grid=(len(tables), h),
        in_specs=[pl.BlockSpec((1, w, nd * w), lambda v, hh: (hh, 0, 0))],
        out_specs=pl.BlockSpec((1, 1, qr * w, kr * w), lambda v, hh: (v, hh, 0, 0)),
        out_shape=jax.ShapeDtypeStruct((len(tables), h, qr * w, kr * w), F32),
        compiler_params=_cparams(("arbitrary", "arbitrary")),
    )(tq)


def _bias_kernel(tq_ref, o_ref, *, segments, w):
    for k, seg in enumerate(segments):
        @pl.when(pl.program_id(0) == k)
        def _(seg=seg):
            o_ref[...] = jnp.full(o_ref.shape, NEG_INF, F32)
            for i, (j0, j1, d0) in enumerate(seg):
                o_ref[0, 0, i * w:(i + 1) * w, j0 * w:j1 * w] = tq_ref[0, :, d0 * w:(d0 + j1 - j0) * w]


def _nbr_attn_kernel(kstart_ref, q_ref, k_ref, v_ref, ck_ref, cv_ref, bias_ref, o_ref, *, n_local, nblk):
    blk = pl.program_id(2)
    k0 = pl.multiple_of(kstart_ref[blk] * GRID_W, GRID_W)
    variant = kstart_ref[nblk + blk]
    for hp in range(q_ref.shape[0]):
        kl = k_ref[hp, pl.ds(k0, n_local), :]
        vl = v_ref[hp, pl.ds(k0, n_local), :]
        biases = [[bias_ref[variant, 2 * hp + head], None] for head in range(2)]
        o = _pair_attention(q_ref[hp], [kl, ck_ref[hp, 0]], [vl, cv_ref[hp, 0]], biases)
        o_ref[:, LANES * hp:LANES * (hp + 1)] = o.astype(BF16)


def _nbr_attention(q, k, v, ck, cv, rpb, batch, seq):
    n_pairs, n, _ = q.shape
    rows = seq // GRID_W
    nblk, n_key_rows, kstart, variant, tables = _nbr_geometry(rows)
    bias = _nbr_bias(rpb, tables)
    m = Q_ROWS * GRID_W
    n_local = n_key_rows * GRID_W
    past = ck.shape[2]
    sched = jnp.asarray(np.concatenate([kstart, variant]), I32)
    npr = ATTN_PAIRS
    grid_spec = pltpu.PrefetchScalarGridSpec(
        num_scalar_prefetch=1,
        grid=(n_pairs // npr, batch, nblk),
        in_specs=[
            pl.BlockSpec((npr, m, LANES), lambda hg, b, r, ks: (hg, b * nblk + r, 0)),
            pl.BlockSpec((npr, seq, LANES), lambda hg, b, r, ks: (hg, b, 0)),
            pl.BlockSpec((npr, seq, LANES), lambda hg, b, r, ks: (hg, b, 0)),
            pl.BlockSpec((npr, 1, past, LANES), lambda hg, b, r, ks: (hg, b, 0, 0)),
            pl.BlockSpec((npr, 1, past, LANES), lambda hg, b, r, ks: (hg, b, 0, 0)),
            pl.BlockSpec((bias.shape[0], 2 * npr, m, n_local), lambda hg, b, r, ks: (0, hg, 0, 0)),
        ],
        out_specs=pl.BlockSpec((m, npr * LANES), lambda hg, b, r, ks: (b * nblk + r, hg)),
    )
    return pl.pallas_call(
        functools.partial(_nbr_attn_kernel, n_local=n_local, nblk=nblk),
        name="nbr_attention",
        grid_spec=grid_spec,
        out_shape=jax.ShapeDtypeStruct((n, n_pairs * LANES), BF16),
        compiler_params=_cparams(("arbitrary", "arbitrary", "arbitrary")),
    )(sched, q, k, v, ck, cv, bias)


def _attn_out_kernel(x_ref, o_ref, gt_ref, w_ref, y_ref):
    y_ref[...] = x_ref[...] + gt_ref[0, 0] * _dot(o_ref[...], w_ref[...])


def _attn_out(x2d, o, st, mod, w_bf16):
    n, d = x2d.shape
    tm = QKV_TILE
    tile = pl.BlockSpec((tm, d), lambda i: (i, 0))
    return pl.pallas_call(
        _attn_out_kernel,
        name="attn_out",
        grid=(n // tm,),
        in_specs=[tile, tile, mod.row_spec(0, 2, lambda i: st.cond_row(i, tm)),
                  pl.BlockSpec((d, d), lambda i: (0, 0))],
        out_specs=tile,
        out_shape=jax.ShapeDtypeStruct((n, d), F32),
        compiler_params=_cparams(("arbitrary",)),
    )(x2d, o, mod.t4, w_bf16)


def _lru_in_kernel(y_ref, sh_ref, sc_ref, g_ref, w_ref, gate_ref, xbr_ref, *, d_rnn, latent):
    bsz, tt, d = y_ref.shape
    x = jnp.concatenate([y_ref[:, t, :] for t in range(tt)], axis=0)
    if latent:
        x3 = x.reshape(tt, bsz, d)
        hn = _norm_mod(x3, g_ref[...][None], sc_ref[0][None], sh_ref[0][None]).reshape(tt * bsz, d)
    else:
        hn = _norm_mod(x, g_ref[...], sc_ref[0, 0], sh_ref[0, 0])
    hn = hn.astype(BF16)
    gate_ref[...] = _dot(hn, w_ref[:, :d_rnn])
    xbr_ref[...] = _dot(hn, w_ref[:, d_rnn:])


def _lru_in(y3, st, mod, g, w_bf16):
    bsz, seq, d = y3.shape
    d_rnn = w_bf16.shape[1] // 2
    tt = QKV_TILE // bsz
    assert (not st.latent) or bsz == SUBLANES
    mspec = (lambda c: mod.latent_spec(1, c)) if st.latent else (lambda c: mod.row_spec(1, c, lambda t: 0))
    out_spec = pl.BlockSpec((tt * bsz, d_rnn), lambda t: (t, 0))
    out_shape = jax.ShapeDtypeStruct((seq * bsz, d_rnn), F32)
    return pl.pallas_call(
        functools.partial(_lru_in_kernel, d_rnn=d_rnn, latent=st.latent),
        name="lru_in",
        grid=(seq // tt,),
        in_specs=[
            pl.BlockSpec((bsz, tt, d), lambda t: (0, t, 0)),
            mspec(0), mspec(1),
            pl.BlockSpec((1, d), lambda t: (0, 0)),
            pl.BlockSpec((d, 2 * d_rnn), lambda t: (0, 0)),
        ],
        out_specs=[out_spec, out_spec],
        out_shape=[out_shape, out_shape],
        compiler_params=_cparams(("arbitrary",)),
    )(y3, mod.t3 if st.latent else mod.t4, mod.t3 if st.latent else mod.t4, g, w_bf16)


def _scan_kernel(prev_ref, cur_ref, next_ref, cw_ref, cb_ref, wa_ref, wx_ref, ba_ref, bx_ref, lam_ref,
                 h0_ref, h_ref, hfin_ref, a_scr, b_scr, carry, *, batch, steps, nchunk):
    rev = pl.program_id(0) == 1
    c = pl.program_id(1)
    c_eff = jnp.where(rev, nchunk - 1 - c, c)
    rows = steps * batch
    cb = wa_ref.shape[2]
    for n in range(wa_ref.shape[1]):
        ch = slice(n * cb, (n + 1) * cb)
        prev = jnp.where(c_eff == 0, 0.0, prev_ref[:, ch])
        nxt = jnp.where(c_eff == nchunk - 1, 0.0, next_ref[:, ch])
        xfull = jnp.concatenate([prev, cur_ref[:, ch], nxt], axis=0)
        cw = cw_ref[:, ch]
        xc = cw[0:1, :] * xfull[0:rows]
        for j in range(1, cw.shape[0]):
            xc = xc + cw[j:j + 1, :] * xfull[j * batch:j * batch + rows]
        xc = xc + cb_ref[:, ch]
        xcb = xc.astype(BF16)
        t_r = jnp.tanh(0.5 * (_dot(xcb, wa_ref[0, n]) + ba_ref[0, :, ch]))
        t_i = jnp.tanh(0.5 * (_dot(xcb, wx_ref[0, n]) + bx_ref[0, :, ch]))
        neg_lam = -lam_ref[0, :, ch]
        softplus = jnp.maximum(neg_lam, 0.0) + jnp.log1p(jnp.exp(-jnp.abs(neg_lam)))
        log_a = ((-0.5 * LRU_C) * softplus) * (t_r + 1.0)
        a = jnp.exp(log_a)
        a_scr[:, ch] = a
        b_scr[:, ch] = jnp.sqrt(jnp.tanh(-log_a) * (1.0 + a * a)) * ((0.5 * xc) * (t_i + 1.0))

    @pl.when(c == 0)
    def _():
        carry[...] = h0_ref[0]

    def step(t, h):
        te = jnp.where(rev, steps - 1 - t, t)
        r0 = pl.multiple_of(te * batch, batch)
        h = a_scr[pl.ds(r0, batch), :] * h + b_scr[pl.ds(r0, batch), :]
        h_ref[0, pl.ds(r0, batch), :] = h
        return h

    h = lax.fori_loop(0, steps, step, carry[...], unroll=min(steps, 8))
    carry[...] = h

    @pl.when(c == nchunk - 1)
    def _():
        hfin_ref[0] = h


def _lru_scan(x2, batch, seq, conv_w, conv_b, wa_bf16, wx_bf16, ba, bx, lam, h0):
    d_rnn = conv_w.shape[1]
    nb, cb = wa_bf16.shape[1], wa_bf16.shape[2]
    assert batch % SUBLANES == 0 and conv_w.shape[0] == 4 and nb * cb == d_rnn
    steps = max(SCAN_ROWS // batch, 2)
    rows = steps * batch
    nchunk = seq // steps
    halo = 2 * batch
    hpc = rows // halo
    n_halo = seq * batch // halo
    ce = lambda d, c: jnp.where(d == 1, nchunk - 1 - c, c)
    vec = lambda a: a.reshape(2, 1, d_rnn)
    vspec = pl.BlockSpec((1, 1, d_rnn), lambda d, c: (d, 0, 0))
    wspec = pl.BlockSpec((1, nb, cb, cb), lambda d, c: (d, 0, 0, 0))
    return pl.pallas_call(
        functools.partial(_scan_kernel, batch=batch, steps=steps, nchunk=nchunk),
        name="lru_scan",
        grid=(2, nchunk),
        in_specs=[
            pl.BlockSpec((halo, d_rnn), lambda d, c: (jnp.maximum(ce(d, c) * hpc - 1, 0), 0)),
            pl.BlockSpec((rows, d_rnn), lambda d, c: (ce(d, c), 0)),
            pl.BlockSpec((halo, d_rnn), lambda d, c: (jnp.minimum((ce(d, c) + 1) * hpc, n_halo - 1), 0)),
            pl.BlockSpec((4, d_rnn), lambda d, c: (0, 0)),
            pl.BlockSpec((1, d_rnn), lambda d, c: (0, 0)),
            wspec, wspec, vspec, vspec, vspec,
            pl.BlockSpec((1, batch, d_rnn), lambda d, c: (d, 0, 0)),
        ],
        out_specs=[
            pl.BlockSpec((1, rows, d_rnn), lambda d, c: (d, ce(d, c), 0)),
            pl.BlockSpec((1, batch, d_rnn), lambda d, c: (d, 0, 0)),
        ],
        out_shape=[
            jax.ShapeDtypeStruct((2, seq * batch, d_rnn), F32),
            jax.ShapeDtypeStruct((2, batch, d_rnn), F32),
        ],
        scratch_shapes=[pltpu.VMEM((rows, d_rnn), F32), pltpu.VMEM((rows, d_rnn), F32),
                        pltpu.VMEM((batch, d_rnn), F32)],
        compiler_params=_cparams(("arbitrary", "arbitrary")),
    )(x2, x2, x2, conv_w, conv_b[None, :], wa_bf16, wx_bf16, vec(ba), vec(bx), vec(lam), h0)


def _lru_out_kernel(y_ref, gate_ref, h_ref, gt_ref, w_ref, o_ref, z_scr, *, latent):
    bsz, tt, d = y_ref.shape
    z = jax.nn.gelu(gate_ref[...]) * (h_ref[0] + h_ref[1])
    z_scr[...] = z.reshape(tt, bsz, z.shape[1])
    zb = jnp.concatenate([z_scr[:, b, :] for b in range(bsz)], axis=0)
    out = _dot(zb.astype(BF16), w_ref[...]).reshape(bsz, tt, d)
    gt = gt_ref[0][:, None, :] if latent else gt_ref[0]
    o_ref[...] = y_ref[...] + gt * out


def _lru_out(y3, gate, h, st, mod, w_bf16):
    bsz, seq, d = y3.shape
    d_rnn = w_bf16.shape[0]
    tt = QKV_TILE // bsz
    gspec = mod.latent_spec(1, 2) if st.latent else mod.row_spec(1, 2, lambda t: 0)
    return pl.pallas_call(
        functools.partial(_lru_out_kernel, latent=st.latent),
        name="lru_out",
        grid=(seq // tt,),
        in_specs=[
            pl.BlockSpec((bsz, tt, d), lambda t: (0, t, 0)),
            pl.BlockSpec((tt * bsz, d_rnn), lambda t: (t, 0)),
            pl.BlockSpec((2, tt * bsz, d_rnn), lambda t: (0, t, 0)),
            gspec,
            pl.BlockSpec((d_rnn, d), lambda t: (0, 0)),
        ],
        out_specs=pl.BlockSpec((bsz, tt, d), lambda t: (0, t, 0)),
        out_shape=jax.ShapeDtypeStruct((bsz, seq, d), F32),
        scratch_shapes=[pltpu.VMEM((tt, bsz, d_rnn), F32)],
        compiler_params=_cparams(("arbitrary",)),
    )(y3, gate, h, mod.t3 if st.latent else mod.t4, w_bf16)


def _dual_specs(shape, sp, ss, extra=0):
    pmap = lambda i, *_: (jnp.minimum(i, sp.tiles - 1), 0)
    smap = lambda i, *_: (jnp.maximum(i - sp.tiles, 0), 0)
    return pl.BlockSpec(shape, pmap), pl.BlockSpec(shape, smap)


def _joint_cond_row(sp, ss):
    return lambda i, *_: jnp.where(i < sp.tiles, 0, ss.cond_row(jnp.maximum(i - sp.tiles, 0)))


def _router_kernel(yp_ref, ys_ref, sh_ref, sc_ref, g_ref, rwt_ref, rb_ref, bkt_ref, rank_ref, wext_ref,
                   cnt_ref, carry, *, n_experts, ntp, n_steps):
    i = pl.program_id(0)

    @pl.when(i == 0)
    def _():
        carry[...] = jnp.zeros_like(carry)

    y = jnp.where(i < ntp, yp_ref[...], ys_ref[...])
    hn = _norm_mod(y, g_ref[...], sc_ref[0, 0], sh_ref[0, 0])
    tm = hn.shape[0]
    logits = lax.dot_general(rwt_ref[...], hn, (((1,), (1,)), ((), ())),
                             precision=lax.Precision.HIGHEST, preferred_element_type=F32)
    scores = jax.nn.sigmoid(logits)
    sel = scores + rb_ref[...]
    per = n_experts // N_GROUPS
    assert per == 4
    srow = [sel[e:e + 1, :] for e in range(n_experts)]
    prow = [scores[e:e + 1, :] for e in range(n_experts)]

    def top2_sum(a, b, c, d):
        hi1, lo1, hi2, lo2 = jnp.maximum(a, b), jnp.minimum(a, b), jnp.maximum(c, d), jnp.minimum(c, d)
        return jnp.maximum(hi1, hi2) + jnp.maximum(jnp.minimum(hi1, hi2), jnp.maximum(lo1, lo2))

    gscore = [top2_sum(*srow[per * g:per * (g + 1)]) for g in range(N_GROUPS)]
    best = jnp.zeros((1, tm), I32)
    best_v = gscore[0]
    for g in range(1, N_GROUPS):
        upd = gscore[g] > best_v
        best = jnp.where(upd, g, best)
        best_v = jnp.where(upd, gscore[g], best_v)
    vs = list(srow[:per])
    ps = list(prow[:per])
    for g in range(1, N_GROUPS):
        for j in range(per):
            vs[j] = jnp.where(best == g, srow[per * g + j], vs[j])
            ps[j] = jnp.where(best == g, prow[per * g + j], ps[j])
    j1 = jnp.zeros((1, tm), I32)
    m1 = vs[0]
    for j in range(1, per):
        upd = vs[j] > m1
        j1 = jnp.where(upd, j, j1)
        m1 = jnp.where(upd, vs[j], m1)
    j2 = jnp.full((1, tm), -1, I32)
    m2 = jnp.full((1, tm), -jnp.inf, F32)
    for j in range(per):
        upd = (j1 != j) & (vs[j] > m2)
        j2 = jnp.where(upd, j, j2)
        m2 = jnp.where(upd, vs[j], m2)
    s1 = jnp.zeros((1, tm), F32)
    s2 = jnp.zeros((1, tm), F32)
    for j in range(per):
        s1 = jnp.where(j1 == j, ps[j], s1)
        s2 = jnp.where(j2 == j, ps[j], s2)
    den = s1 + s2
    w1, w2 = s1 / den, s2 / den
    lo = jnp.minimum(j1, j2)
    hi = jnp.maximum(j1, j2)
    w_lo = jnp.where(j1 < j2, w1, w2)
    w_hi = jnp.where(j1 < j2, w2, w1)
    base = jnp.where(lo == 0, 0, jnp.where(lo == 1, 3, 5))
    bucket = best * 6 + base + hi - lo - 1

    onehot = (lax.broadcasted_iota(I32, (N_BUCKET_SLOTS, tm), 0) == bucket).astype(F32)
    before = (lax.broadcasted_iota(I32, (tm, tm), 0) < lax.broadcasted_iota(I32, (tm, tm), 1)).astype(BF16)
    prefix = _dot(onehot.astype(BF16), before) + carry[...]
    rank = jnp.sum(onehot * prefix, axis=0, keepdims=True)
    carry[...] = carry[...] + jnp.sum(onehot, axis=1, keepdims=True)

    bkt_ref[0] = bucket
    rank_ref[0] = rank.astype(I32)
    slab_row = lax.broadcasted_iota(I32, (LANES, tm), 0)
    slab = jnp.where(slab_row == 0, w_lo, jnp.where(slab_row == 1, w_hi, 0.0))
    wext_ref[...] = slab.T

    @pl.when(i == n_steps - 1)
    def _():
        cnt_ref[...] = jnp.broadcast_to(carry[...], cnt_ref.shape).astype(I32)


def _router(yp, ys, sp, ss, mod, layer, g, router_wt, router_b):
    d = yp.shape[1]
    e = router_wt.shape[0]
    tm = TOKEN_TILE
    nt = sp.tiles + ss.tiles
    row = _joint_cond_row(sp, ss)
    tok_spec = pl.BlockSpec((1, 1, tm), lambda i: (i, 0, 0))
    pspec, sspec = _dual_specs((tm, d), sp, ss)
    return pl.pallas_call(
        functools.partial(_router_kernel, n_experts=e, ntp=sp.tiles, n_steps=nt),
        name="moe_router",
        grid=(nt,),
        in_specs=[
            pspec, sspec,
            mod.row_spec(layer, 3, row),
            mod.row_spec(layer, 4, row),
            pl.BlockSpec((1, d), lambda i: (0, 0)),
            pl.BlockSpec((e, d), lambda i: (0, 0)),
            pl.BlockSpec((e, 1), lambda i: (0, 0)),
        ],
        out_specs=[tok_spec, tok_spec,
                   pl.BlockSpec((tm, LANES), lambda i: (i, 0)),
                   pl.BlockSpec((N_BUCKET_SLOTS, LANES), lambda i: (0, 0))],
        out_shape=[jax.ShapeDtypeStruct((nt, 1, tm), I32), jax.ShapeDtypeStruct((nt, 1, tm), I32),
                   jax.ShapeDtypeStruct((nt * tm, LANES), F32),
                   jax.ShapeDtypeStruct((N_BUCKET_SLOTS, LANES), I32)],
        scratch_shapes=[pltpu.VMEM((N_BUCKET_SLOTS, 1), F32)],
        compiler_params=_cparams(("arbitrary",)),
    )(yp, ys, mod.t4, mod.t4, g, router_wt, router_b[:, None])


def _bucket_layout(counts, n_tokens):
    n_buckets = N_GROUPS * len(PAIR_LO)
    tf = FFN_TILE
    n_tiles = n_tokens // tf + n_buckets + 2
    c = counts[:n_buckets]
    padded = ((c + tf - 1) // tf) * tf
    ends = jnp.cumsum(padded)
    starts = ends - padded
    n_used = ends[-1] // tf
    tile = jnp.arange(n_tiles, dtype=I32)
    valid = tile < n_used
    tile_c = jnp.minimum(tile, n_used - 1)
    tb = jnp.sum((ends[None, :] <= (tile_c * tf)[:, None]).astype(I32), axis=1)
    tb = jnp.minimum(tb, n_buckets - 1)
    grp, pair = tb // len(PAIR_LO), tb % len(PAIR_LO)
    pair_lo = sum(jnp.where(pair == p, PAIR_LO[p], 0) for p in range(len(PAIR_LO)))
    pair_hi = sum(jnp.where(pair == p, PAIR_HI[p], 0) for p in range(len(PAIR_HI)))
    slots = lambda v: jnp.concatenate([v.astype(I32), jnp.zeros((N_BUCKET_SLOTS - n_buckets,), I32)])
    bucket_end = sum(jnp.where(tb == b, starts[b] + c[b], 0) for b in range(n_buckets))
    n_rows_used = jnp.clip(bucket_end - tile_c * tf, 0, tf)
    work = jnp.where(valid, jnp.where(n_rows_used <= tf // 2, 2, 1), 0)
    sched = jnp.stack([tile_c, grp * 4 + pair_lo, grp * 4 + pair_hi, work]).astype(I32)
    fill = slots((starts + c) // SUBLANES).at[n_buckets].set(n_used.astype(I32))
    return slots(starts), fill, sched, n_tiles


def _pos_kernel(starts_ref, bkt_ref, rank_ref, pos_ref):
    bkt = bkt_ref[...]
    pos = rank_ref[...]
    for b in range(N_GROUPS * len(PAIR_LO)):
        pos = pos + jnp.where(bkt == b, starts_ref[b], 0)
    pos_ref[...] = pos


def _positions(starts32, bkt, rank):
    nt, _, tm = bkt.shape
    grid_spec = pltpu.PrefetchScalarGridSpec(
        num_scalar_prefetch=1,
        grid=(1,),
        in_specs=[pl.BlockSpec((nt, 1, tm), lambda i, st: (0, 0, 0))] * 2,
        out_specs=pl.BlockSpec((nt, 1, tm), lambda i, st: (0, 0, 0)),
    )
    return pl.pallas_call(
        _pos_kernel, name="moe_positions", grid_spec=grid_spec,
        out_shape=jax.ShapeDtypeStruct((nt, 1, tm), I32),
        compiler_params=_cparams(("arbitrary",)),
    )(starts32, bkt, rank)


def _dispatch_kernel(fill_ref, yp_ref, ys_ref, sh_ref, sc_ref, g_ref, wext_ref, pos_ref,
                     xs_ref, buf0, buf1, zeros, sem0, sem1, *, d, ntp, n_steps, n_buckets):
    groups, sub, width = buf0.shape
    i = pl.program_id(0)
    y = jnp.where(i < ntp, yp_ref[...], ys_ref[...])
    hn = _norm_mod(y, g_ref[...], sc_ref[0, 0], sh_ref[0, 0])

    @pl.when(i == 0)
    def _():
        zgroups = zeros.shape[0] - 1
        zeros[...] = jnp.zeros_like(zeros)
        fills = [pltpu.make_async_copy(zeros, xs_ref.at[pl.ds(fill_ref[b], zgroups + 1)], sem1)
                 for b in range(n_buckets)]
        for copy in fills:
            copy.start()
        for copy in fills:
            copy.wait()

        def clear_tile(t, carry):
            copy = pltpu.make_async_copy(zeros.at[pl.ds(0, zgroups)], xs_ref.at[pl.ds(t * zgroups, zgroups)], sem1)
            copy.start()
            copy.wait()
            return carry

        lax.fori_loop(fill_ref[n_buckets], xs_ref.shape[0] // zgroups, clear_tile, 0)

    def wait_tile(buf, sem):
        pltpu.make_async_copy(buf, xs_ref.at[pl.ds(0, groups)], sem).wait()

    def step(buf, sem, prev_buf, prev_sem):
        buf[:, :, :d] = hn.reshape(groups, sub, d)
        buf[:, :, d:] = wext_ref[...].reshape(groups, sub, width - d)

        def issue(g, carry):
            for u in range(sub):
                pos = pos_ref[0, 0, g * sub + u]
                dst = xs_ref.at[lax.shift_right_logical(pos, 3), pl.ds(pos & (sub - 1), 1)]
                pltpu.make_async_copy(buf.at[g, pl.ds(u, 1)], dst, sem).start()
            return carry

        lax.fori_loop(0, groups, issue, 0)

        @pl.when(i > 0)
        def _():
            wait_tile(prev_buf, prev_sem)

        @pl.when(i == n_steps - 1)
        def _():
            wait_tile(buf, sem)

    @pl.when(i % 2 == 0)
    def _():
        step(buf0, sem0, buf1, sem1)

    @pl.when(i % 2 == 1)
    def _():
        step(buf1, sem1, buf0, sem0)


def _dispatch(yp, ys, sp, ss, mod, layer, g, wext, pos, fill_groups, n_rows):
    d = yp.shape[1]
    tm = TOKEN_TILE
    width = d + LANES
    groups = tm // SUBLANES
    row = _joint_cond_row(sp, ss)
    pspec, sspec = _dual_specs((tm, d), sp, ss)
    tile_buf = pltpu.VMEM((groups, SUBLANES, width), F32)
    grid_spec = pltpu.PrefetchScalarGridSpec(
        num_scalar_prefetch=1,
        grid=(sp.tiles + ss.tiles,),
        in_specs=[
            pspec, sspec,
            mod.row_spec(layer, 3, row), mod.row_spec(layer, 4, row),
            pl.BlockSpec((1, d), lambda i, f: (0, 0)),
            pl.BlockSpec((tm, LANES), lambda i, f: (i, 0)),
            pl.BlockSpec((1, 1, tm), lambda i, f: (i, 0, 0), memory_space=pltpu.SMEM),
        ],
        out_specs=pl.BlockSpec(memory_space=pl.ANY),
        scratch_shapes=[tile_buf, tile_buf, pltpu.VMEM((FFN_TILE // SUBLANES + 1, SUBLANES, width), F32),
                        pltpu.SemaphoreType.DMA(()), pltpu.SemaphoreType.DMA(())],
    )
    return pl.pallas_call(
        functools.partial(_dispatch_kernel, d=d, ntp=sp.tiles, n_steps=sp.tiles + ss.tiles,
                          n_buckets=N_GROUPS * len(PAIR_LO)),
        name="moe_dispatch",
        grid_spec=grid_spec,
        out_shape=jax.ShapeDtypeStruct((n_rows // SUBLANES, SUBLANES, width), F32),
        compiler_params=_cparams(("arbitrary",), disable_bounds_checks=True),
    )(fill_groups, yp, ys, mod.t4, mod.t4, g, wext, pos).reshape(n_rows, width)


def _ffn_kernel(sched_ref, xs_ref, wg_lo, wu_lo, wd_lo, wg_hi, wu_hi, wd_hi, o_ref, *, d):
    work = sched_ref[3, pl.program_id(0)]
    tf = o_ref.shape[0]

    def experts(rows):
        x = xs_ref[:rows, :d].astype(BF16)
        gates = [_dot(x, wg[0, 0]) for wg in (wg_lo, wg_hi)]
        ups = [_dot(x, wu[0, 0]) for wu in (wu_lo, wu_hi)]
        hes = [((g * jax.nn.sigmoid(g)) * u).astype(BF16) for g, u in zip(gates, ups)]
        w_lo = xs_ref[:rows, d:d + 1]
        w_hi = xs_ref[:rows, d + 1:d + 2]
        o_ref[:rows, :] = w_lo * _dot(hes[0], wd_lo[0, 0]) + w_hi * _dot(hes[1], wd_hi[0, 0])

    @pl.when(work == 0)
    def _():
        o_ref[...] = jnp.zeros_like(o_ref)

    @pl.when(work == 1)
    def _():
        experts(tf)

    @pl.when(work == 2)
    def _():
        experts(tf // 2)
        o_ref[tf // 2:, :] = jnp.zeros((tf - tf // 2, o_ref.shape[1]), F32)


def _expert_ffn(xs, sched, n_tiles, layer, wg, wu, wd):
    n_rows, width = xs.shape
    d = width - LANES
    f = wg.shape[3]
    tf = FFN_TILE
    lo = lambda i, s: (layer, s[1, i], 0, 0)
    hi = lambda i, s: (layer, s[2, i], 0, 0)
    grid_spec = pltpu.PrefetchScalarGridSpec(
        num_scalar_prefetch=1,
        grid=(n_tiles,),
        in_specs=[
            pl.BlockSpec((tf, width), lambda i, s: (s[0, i], 0)),
            pl.BlockSpec((1, 1, d, f), lo), pl.BlockSpec((1, 1, d, f), lo), pl.BlockSpec((1, 1, f, d), lo),
            pl.BlockSpec((1, 1, d, f), hi), pl.BlockSpec((1, 1, d, f), hi), pl.BlockSpec((1, 1, f, d), hi),
        ],
        out_specs=pl.BlockSpec((tf, d), lambda i, s: (i, 0)),
    )
    return pl.pallas_call(
        functools.partial(_ffn_kernel, d=d),
        name="moe_expert_ffn",
        grid_spec=grid_spec,
        out_shape=jax.ShapeDtypeStruct((n_rows, d), F32),
        compiler_params=_cparams(("arbitrary",)),
    )(sched, xs, wg, wu, wd, wg, wu, wd)


def _combine_kernel(y_ref, gt_ref, pos_ref, pos_next_ref, ms_ref, ms_groups_ref, o_ref, buf0, buf1, sem0, sem1, *,
                    n_steps):
    groups, sub, d = buf0.shape
    tm = groups * sub
    i = pl.program_id(0)

    def gather(pos, buf, sem):
        def issue(g, carry):
            for u in range(sub):
                pltpu.make_async_copy(ms_ref.at[pl.ds(pos[0, 0, g * sub + u], 1)], buf.at[g, pl.ds(u, 1)],
                                      sem).start()
            return carry

        lax.fori_loop(0, groups, issue, 0)

    def step(buf, sem, next_buf, next_sem):
        @pl.when(i == 0)
        def _():
            gather(pos_ref, buf, sem)

        @pl.when(i + 1 < n_steps)
        def _():
            gather(pos_next_ref, next_buf, next_sem)

        pltpu.make_async_copy(ms_groups_ref.at[pl.ds(0, groups)], buf, sem).wait()
        o_ref[...] = y_ref[...] + gt_ref[0, 0] * buf[...].reshape(tm, d)

    @pl.when(i % 2 == 0)
    def _():
        step(buf0, sem0, buf1, sem1)

    @pl.when(i % 2 == 1)
    def _():
        step(buf1, sem1, buf0, sem0)


def _combine(y2d, st, mod, layer, pos, ms):
    n, d = y2d.shape
    tm = TOKEN_TILE
    tile = pl.BlockSpec((tm, d), lambda i: (i, 0))
    last = st.tile_off + n // tm - 1
    return pl.pallas_call(
        functools.partial(_combine_kernel, n_steps=n // tm),
        name="moe_combine",
        grid=(n // tm,),
        in_specs=[
            tile,
            mod.row_spec(layer, 5, st.cond_row),
            pl.BlockSpec((1, 1, tm), lambda i: (st.tile_off + i, 0, 0), memory_space=pltpu.SMEM),
            pl.BlockSpec((1, 1, tm), lambda i: (jnp.minimum(st.tile_off + i + 1, last), 0, 0),
                         memory_space=pltpu.SMEM),
            pl.BlockSpec(memory_space=pl.ANY),
            pl.BlockSpec(memory_space=pl.ANY),
        ],
        out_specs=tile,
        out_shape=jax.ShapeDtypeStruct((n, d), F32),
        scratch_shapes=[pltpu.VMEM((tm // SUBLANES, SUBLANES, d), F32),
                        pltpu.VMEM((tm // SUBLANES, SUBLANES, d), F32),
                        pltpu.SemaphoreType.DMA(()), pltpu.SemaphoreType.DMA(())],
        compiler_params=_cparams(("arbitrary",), disable_bounds_checks=True),
    )(y2d, mod.t4, pos, pos, ms, ms.reshape(ms.shape[0] // SUBLANES, SUBLANES, d))


def _moe_residual(yp, ys, sp, ss, mod, layer, g, router_wt, router_b, wg, wu, wd):
    bkt, rank, wext, counts = _router(yp, ys, sp, ss, mod, layer, g, router_wt, router_b)
    starts32, fill_groups, sched, n_tiles = _bucket_layout(counts[:, 0], sp.n + ss.n)
    pos = _positions(starts32, bkt, rank)
    xs = _dispatch(yp, ys, sp, ss, mod, layer, g, wext, pos, fill_groups, n_tiles * FFN_TILE)
    ms = _expert_ffn(xs, sched, n_tiles, layer, wg, wu, wd)
    return _combine(yp, sp, mod, layer, pos, ms), _combine(ys, ss, mod, layer, pos, ms)


def kernel(x_prompt, x_sample, c, cache_k, cache_v, state_lru, c_ctx, ada_w, ada_b, norm_mix, norm_ffn,
           attn_w_in, attn_q_gain, attn_k_gain, attn_rpb, attn_w_out, lru_w_in, lru_conv_w, lru_conv_b,
           lru_gate_a_w, lru_gate_a_b, lru_gate_x_w, lru_gate_x_b, lru_lambda, lru_w_out, router_w,
           router_b, moe_w_gate, moe_w_up, moe_w_down):
    bp, tp, d = x_prompt.shape
    bs, ts, _ = x_sample.shape
    n_heads, head_dim = cache_k.shape[3], cache_k.shape[4]
    d_rnn = lru_conv_w.shape[2]
    assert ada_w.shape[0] == 2 and bs <= N_COND_ROWS - LATENT_COND_ROW
    assert tp % TOKEN_TILE == 0 and ts % TOKEN_TILE == 0
    sp = _Stream(bp, tp, 0, latent=False)
    ss = _Stream(bs, ts, sp.tiles, latent=True)

    cond = jnp.zeros((N_COND_ROWS, d), F32).at[0].set(c_ctx).at[LATENT_COND_ROW:LATENT_COND_ROW + bs].set(c)
    mod = _Mod(_ada_modulation(cond, ada_w, ada_b), d)

    w_in = attn_w_in[0].astype(BF16)
    w_out = attn_w_out[0].astype(BF16)
    g_mix0 = norm_mix[0][None, :]
    xp2, xs2 = x_prompt.reshape(sp.n, d), x_sample.reshape(ss.n, d)
    q_p, k_p, v_p, new_k, new_v = _qkv_proj(xp2, sp, mod, g_mix0, w_in, attn_q_gain[0], attn_k_gain[0],
                                            n_heads, True)
    q_s, k_s, v_s = _qkv_proj(xs2, ss, mod, g_mix0, w_in, attn_q_gain[0], attn_k_gain[0], n_heads, False)
    o_p = _ctx_attention(q_p, k_p, v_p, bp, tp)
    pair_major = lambda cache: jnp.transpose(
        cache[:, 0].astype(BF16).reshape(bs, cache.shape[2], n_heads // 2, LANES), (2, 0, 1, 3))
    ck, cv = pair_major(cache_k), pair_major(cache_v)
    o_s = _nbr_attention(q_s, k_s, v_s, ck, cv, attn_rpb[0], bs, ts)
    yp = _attn_out(xp2, o_p, sp, mod, w_out)
    ys = _attn_out(xs2, o_s, ss, mod, w_out)

    router_wt = router_w.T
    moe_w = (moe_w_gate.astype(BF16), moe_w_up.astype(BF16), moe_w_down.astype(BF16))
    yp, ys = _moe_residual(yp, ys, sp, ss, mod, 0, norm_ffn[0][None, :], router_wt, router_b, *moe_w)

    w_lru_in = lru_w_in[0].astype(BF16)
    w_lru_out = lru_w_out[0].astype(BF16)
    g_mix1 = norm_mix[1][None, :]
    scan_args = (lru_conv_w[0], lru_conv_b[0], lru_gate_a_w[0].astype(BF16), lru_gate_x_w[0].astype(BF16),
                 lru_gate_a_b[0], lru_gate_x_b[0], lru_lambda[0])
    hfin = {}
    ys_out = []
    for y2, st, h0 in ((yp, sp, jnp.zeros((2, bp, d_rnn), F32)),
                       (ys, ss, jnp.transpose(state_lru[:, 0], (1, 0, 2)))):
        y3 = y2.reshape(st.batch, st.seq, d)
        gate, xbr = _lru_in(y3, st, mod, g_mix1, w_lru_in)
        h, hfin[st.latent] = _lru_scan(xbr, st.batch, st.seq, *scan_args, h0)
        ys_out.append(_lru_out(y3, gate, h, st, mod, w_lru_out).reshape(st.n, d))
    yp, ys = ys_out
    yp, ys = _moe_residual(yp, ys, sp, ss, mod, 1, norm_ffn[1][None, :], router_wt, router_b, *moe_w)

    return (yp.reshape(bp, tp, d), ys.reshape(bs, ts, d),
            new_k.reshape(bp, 1, tp, n_heads, head_dim), new_v.reshape(bp, 1, tp, n_heads, head_dim),
            jnp.transpose(hfin[False], (1, 0, 2))[:, None])
```

```python
import functools

import numpy as np
import jax
import jax.numpy as jnp
from jax import lax
from jax.experimental import pallas as pl
from jax.experimental.pallas import tpu as pltpu

F32 = jnp.float32
BF16 = jnp.bfloat16
I32 = jnp.int32

EPS = 1e-6
NEG_INF = -1e30
GRID_W = 64
WIN_ROWS = 8
WIN_COLS = 16
N_GROUPS = 4
LRU_C = 8.0
N_LRU_BLOCKS = 4

LANES = 128
SUBLANES = 8
MXU_COLS = 256
TOKEN_TILE = 256
QKV_TILE = 512
FFN_TILE = 256
Q_ROWS = 4
SCAN_ROWS = 512
ATTN_PAIRS = 2
N_BUCKET_SLOTS = 32
PAIR_LO = (0, 0, 0, 1, 1, 2)
PAIR_HI = (1, 2, 3, 2, 3, 3)
N_COND_ROWS = 16
LATENT_COND_ROW = 8
VMEM_LIMIT = 48 * 1024 * 1024

def _cparams(sem, **kw):
    return pltpu.CompilerParams(dimension_semantics=sem, vmem_limit_bytes=VMEM_LIMIT, **kw)


def _norm_mod(x, g, sc, sh):
    ms = jnp.mean(x * x, axis=-1, keepdims=True)
    return (x * lax.rsqrt(ms + EPS)) * g * (1.0 + sc) + sh


def _dot(a, b):
    return jnp.dot(a, b, preferred_element_type=F32)


def _dot_nt(a, b):
    return lax.dot_general(a, b, (((1,), (1,)), ((), ())), preferred_element_type=F32)


def _ada_kernel(cond_ref, w_ref, b_ref, o_ref):
    c = cond_ref[...]
    s = (c * jax.nn.sigmoid(c)).astype(BF16)
    o_ref[0] = _dot(s, w_ref[0].astype(BF16)) + b_ref[0]


def _ada_modulation(cond, ada_w, ada_b):
    depth, d, d6 = ada_w.shape
    tn = d6 // 4
    return pl.pallas_call(
        _ada_kernel,
        name="ada_modulation",
        grid=(depth, d6 // tn),
        in_specs=[
            pl.BlockSpec((N_COND_ROWS, d), lambda l, j: (0, 0)),
            pl.BlockSpec((1, d, tn), lambda l, j: (l, 0, j)),
            pl.BlockSpec((1, 1, tn), lambda l, j: (l, 0, j)),
        ],
        out_specs=pl.BlockSpec((1, N_COND_ROWS, tn), lambda l, j: (l, 0, j)),
        out_shape=jax.ShapeDtypeStruct((depth, N_COND_ROWS, d6), F32),
        compiler_params=_cparams(("arbitrary", "arbitrary")),
    )(cond, ada_w, ada_b.reshape(depth, 1, d6))


class _Mod:
    def __init__(self, table, d):
        self.t3 = table
        self.t4 = table.reshape(table.shape[0], N_COND_ROWS, 1, table.shape[2])
        self.d = d

    def row_spec(self, layer, chunk, row_fn):
        return pl.BlockSpec((1, 1, 1, self.d), lambda *ids: (layer, row_fn(*ids), 0, chunk))

    def latent_spec(self, layer, chunk):
        return pl.BlockSpec((1, SUBLANES, self.d), lambda *ids: (layer, LATENT_COND_ROW // SUBLANES, chunk))


class _Stream:
    def __init__(self, batch, seq, tile_off, latent):
        self.batch, self.seq, self.tile_off, self.latent = batch, seq, tile_off, latent
        self.n = batch * seq
        self.tiles = self.n // TOKEN_TILE

    def cond_row(self, i, tile=TOKEN_TILE):
        if not self.latent:
            return 0
        return LATENT_COND_ROW + (i * tile) // self.seq


def _qkv_kernel(x_ref, sh_ref, sc_ref, g_ref, w_ref, qg_ref, kg_ref, gmean_ref, q_ref, k_ref, v_ref, *kv_refs,
                d, head_dim):
    hn = _norm_mod(x_ref[...], g_ref[...], sc_ref[0, 0], sh_ref[0, 0]).astype(BF16)
    outs = (q_ref, k_ref, v_ref)
    gains = (qg_ref, kg_ref)
    n_chunks = 3 * d // MXU_COLS
    n_norm = 2 * d // MXU_COLS
    accs = [_dot(hn, w_ref[:, MXU_COLS * j:MXU_COLS * (j + 1)]) for j in range(n_chunks)]
    means = [_dot((accs[j] * accs[j]).astype(BF16), gmean_ref[...]) for j in range(n_norm)]
    for j in range(n_chunks):
        part = (MXU_COLS * j) // d
        col = (MXU_COLS * j) % d
        acc = accs[j]
        if j < n_norm:
            acc = (acc * lax.rsqrt(means[j] + EPS)) * gains[part][...]
        for half in range(MXU_COLS // LANES):
            outs[part][col // LANES + half] = acc[:, LANES * half:LANES * (half + 1)].astype(BF16)
        if part >= 1 and kv_refs:
            kv_refs[part - 1][:, col:col + MXU_COLS] = acc


def _qkv_proj(x2d, st, mod, g, w_bf16, q_gain, k_gain, n_heads, with_kv):
    n, d = x2d.shape
    head_dim = d // n_heads
    assert head_dim * 2 == LANES
    tm = QKV_TILE
    n_pairs = n_heads // 2
    hspec = pl.BlockSpec((n_pairs, tm, LANES), lambda i: (0, i, 0))
    hshape = jax.ShapeDtypeStruct((n_pairs, n, LANES), BF16)
    out_specs = [hspec, hspec, hspec]
    out_shape = [hshape, hshape, hshape]
    if with_kv:
        out_specs += [pl.BlockSpec((tm, d), lambda i: (i, 0))] * 2
        out_shape += [jax.ShapeDtypeStruct((n, d), F32)] * 2
    lane_head = np.arange(MXU_COLS) // head_dim
    assert float(np.log2(head_dim)).is_integer()
    gmean = jnp.asarray((lane_head[:, None] == lane_head[None, :]) / head_dim, BF16)
    reps = MXU_COLS // head_dim
    qg = jnp.tile(q_gain * _attn_scale(head_dim), reps)[None, :]
    kg = jnp.tile(k_gain, reps)[None, :]
    const = lambda shape: pl.BlockSpec(shape, lambda i: (0,) * len(shape))
    return pl.pallas_call(
        functools.partial(_qkv_kernel, d=d, head_dim=head_dim),
        name="qkv_proj",
        grid=(n // tm,),
        in_specs=[
            pl.BlockSpec((tm, d), lambda i: (i, 0)),
            mod.row_spec(0, 0, lambda i: st.cond_row(i, tm)),
            mod.row_spec(0, 1, lambda i: st.cond_row(i, tm)),
            const((1, d)), const((d, 3 * d)), const((1, MXU_COLS)), const((1, MXU_COLS)),
            const((MXU_COLS, MXU_COLS)),
        ],
        out_specs=out_specs,
        out_shape=out_shape,
        compiler_params=_cparams(("arbitrary",)),
    )(x2d, mod.t4, mod.t4, g, w_bf16, qg, kg, gmean)


def _attn_scale(head_dim):
    scale = head_dim ** -0.5
    assert float(np.log2(scale)).is_integer()
    return scale


def _softmax_pv(scores, values):
    m = scores[0].max(axis=-1, keepdims=True)
    for s in scores[1:]:
        m = jnp.maximum(m, s.max(axis=-1, keepdims=True))
    es = [jnp.exp(s - m) for s in scores]
    l = es[0].sum(axis=-1, keepdims=True)
    for e in es[1:]:
        l = l + e.sum(axis=-1, keepdims=True)
    o = _dot(es[0].astype(BF16), values[0])
    for e, v in zip(es[1:], values[1:]):
        o = o + _dot(e.astype(BF16), v)
    return o / l


def _pair_attention(q2, score_keys, values, biases):
    rows = q2.shape[0]
    first = lax.broadcasted_iota(I32, q2.shape, 1) < q2.shape[1] // 2
    zero = jnp.zeros_like(q2)
    q_both = jnp.concatenate([jnp.where(first, q2, zero), jnp.where(first, zero, q2)], axis=0)
    s_both = [_dot_nt(q_both, keys) for keys in score_keys]
    outs = []
    for head in range(2):
        scores = []
        for s, bias in zip(s_both, biases[head]):
            s = s[head * rows:(head + 1) * rows]
            scores.append(s if bias is None else s + bias)
        outs.append(_softmax_pv(scores, values))
    return jnp.where(first, outs[0], outs[1])


def _ctx_attn_kernel(q_ref, k_ref, v_ref, o_ref):
    for hp in range(q_ref.shape[0]):
        o = _pair_attention(q_ref[hp], [k_ref[hp]], [v_ref[hp]], [[None], [None]])
        o_ref[:, LANES * hp:LANES * (hp + 1)] = o.astype(BF16)


def _ctx_attention(q, k, v, batch, seq):
    n_pairs, n, _ = q.shape
    spec = pl.BlockSpec((n_pairs, seq, LANES), lambda b: (0, b, 0))
    return pl.pallas_call(
        _ctx_attn_kernel,
        name="ctx_attention",
        grid=(batch,),
        in_specs=[spec, spec, spec],
        out_specs=pl.BlockSpec((seq, n_pairs * LANES), lambda b: (b, 0)),
        out_shape=jax.ShapeDtypeStruct((n, n_pairs * LANES), BF16),
        compiler_params=_cparams(("arbitrary",)),
    )(q, k, v)


def _nbr_geometry(rows):
    n_key_rows = Q_ROWS + WIN_ROWS
    assert rows % Q_ROWS == 0 and rows >= n_key_rows
    nblk = rows // Q_ROWS
    kstart = np.clip(np.arange(nblk) * Q_ROWS - WIN_ROWS // 2, 0, rows - n_key_rows)
    start = np.clip(np.arange(rows) - WIN_ROWS // 2, 0, rows - WIN_ROWS)
    tables, variant = [], []
    for blk in range(nblk):
        r = blk * Q_ROWS + np.arange(Q_ROWS)
        kr = kstart[blk] + np.arange(n_key_rows)
        valid = (kr[None, :] >= start[r][:, None]) & (kr[None, :] < start[r][:, None] + WIN_ROWS)
        dr = np.clip(kr[None, :] - r[:, None] + WIN_ROWS - 1, 0, 2 * WIN_ROWS - 2)
        key = (valid.tobytes(), dr.tobytes())
        keys = [t[0] for t in tables]
        if key not in keys:
            tables.append((key, valid, dr))
        variant.append([t[0] for t in tables].index(key))
    return nblk, n_key_rows, kstart, np.array(variant), [(t[1], t[2]) for t in tables]


def _nbr_bias(rpb, tables):
    h, nd, nc = rpb.shape
    w = GRID_W
    period = 2 * w
    pad_l = (w - 1) - (WIN_COLS - 1)
    r_ext = jnp.pad(rpb.astype(F32), ((0, 0), (0, 0), (pad_l, period - nc - pad_l)))
    toe = jnp.tile(r_ext, (1, 1, w))[:, :, :w * (period - 1)].reshape(h, nd, w, period - 1)
    toe = toe[..., w - 1:2 * w - 1]
    cw = np.arange(w)
    col_start = np.clip(cw - WIN_COLS // 2, 0, w - WIN_COLS)
    col_ok = (cw[None, :] >= col_start[:, None]) & (cw[None, :] < col_start[:, None] + WIN_COLS)
    toe = jnp.where(col_ok[None, None], toe, NEG_INF)
    tq = jnp.transpose(toe, (0, 2, 1, 3)).reshape(h, w, nd * w)
    segments = []
    for valid, dr in tables:
        seg = []
        for i in range(valid.shape[0]):
            js = np.nonzero(valid[i])[0]
            j0, j1 = int(js[0]), int(js[-1]) + 1
            assert (np.diff(dr[i, j0:j1]) == 1).all() and valid[i, j0:j1].all()
            seg.append((j0, j1, int(dr[i, j0])))
        segments.append(seg)
    qr, kr = tables[0][0].shape
    return pl.pallas_call(
        functools.partial(_bias_kernel, segments=segments, w=w),
        name="nbr_bias",
        grid=(len(tables), h),
        in_specs=[pl.BlockSpec((1, w, nd * w), lambda v, hh: (hh, 0, 0))],
        out_specs=pl.BlockSpec((1, 1, qr * w, kr * w), lambda v, hh: (v, hh, 0, 0)),
        out_shape=jax.ShapeDtypeStruct((len(tables), h, qr * w, kr * w), F32),
        compiler_params=_cparams(("arbitrary", "arbitrary")),
    )(tq)


def _bias_kernel(tq_ref, o_ref, *, segments, w):
    for k, seg in enumerate(segments):
        @pl.when(pl.program_id(0) == k)
        def _(seg=seg):
            o_ref[...] = jnp.full(o_ref.shape, NEG_INF, F32)
            for i, (j0, j1, d0) in enumerate(seg):
                o_ref[0, 0, i * w:(i + 1) * w, j0 * w:j1 * w] = tq_ref[0, :, d0 * w:(d0 + j1 - j0) * w]


def _nbr_attn_kernel(kstart_ref, q_ref, k_ref, v_ref, ck_ref, cv_ref, bias_ref, o_ref, *, n_local, nblk):
    blk = pl.program_id(2)
    k0 = pl.multiple_of(kstart_ref[blk] * GRID_W, GRID_W)
    variant = kstart_ref[nblk + blk]
    for hp in range(q_ref.shape[0]):
        kl = k_ref[hp, pl.ds(k0, n_local), :]
        vl = v_ref[hp, pl.ds(k0, n_local), :]
        biases = [[bias_ref[variant, 2 * hp + head], None] for head in range(2)]
        o = _pair_attention(q_ref[hp], [kl, ck_ref[hp, 0]], [vl, cv_ref[hp, 0]], biases)
        o_ref[:, LANES * hp:LANES * (hp + 1)] = o.astype(BF16)


def _nbr_attention(q, k, v, ck, cv, rpb, batch, seq):
    n_pairs, n, _ = q.shape
    rows = seq // GRID_W
    nblk, n_key_rows, kstart, variant, tables = _nbr_geometry(rows)
    bias = _nbr_bias(rpb, tables)
    m = Q_ROWS * GRID_W
    n_local = n_key_rows * GRID_W
    past = ck.shape[2]
    sched = jnp.asarray(np.concatenate([kstart, variant]), I32)
    npr = ATTN_PAIRS
    grid_spec = pltpu.PrefetchScalarGridSpec(
        num_scalar_prefetch=1,
        grid=(n_pairs // npr, batch, nblk),
        in_specs=[
            pl.BlockSpec((npr, m, LANES), lambda hg, b, r, ks: (hg, b * nblk + r, 0)),
            pl.BlockSpec((npr, seq, LANES), lambda hg, b, r, ks: (hg, b, 0)),
            pl.BlockSpec((npr, seq, LANES), lambda hg, b, r, ks: (hg, b, 0)),
            pl.BlockSpec((npr, 1, past, LANES), lambda hg, b, r, ks: (hg, b, 0, 0)),
            pl.BlockSpec((npr, 1, past, LANES), lambda hg, b, r, ks: (hg, b, 0, 0)),
            pl.BlockSpec((bias.shape[0], 2 * npr, m, n_local), lambda hg, b, r, ks: (0, hg, 0, 0)),
        ],
        out_specs=pl.BlockSpec((m, npr * LANES), lambda hg, b, r, ks: (b * nblk + r, hg)),
    )
    return pl.pallas_call(
        functools.partial(_nbr_attn_kernel, n_local=n_local, nblk=nblk),
        name="nbr_attention",
        grid_spec=grid_spec,
        out_shape=jax.ShapeDtypeStruct((n, n_pairs * LANES), BF16),
        compiler_params=_cparams(("arbitrary", "arbitrary", "arbitrary")),
    )(sched, q, k, v, ck, cv, bias)


def _attn_out_kernel(x_ref, o_ref, gt_ref, w_ref, y_ref):
    y_ref[...] = x_ref[...] + gt_ref[0, 0] * _dot(o_ref[...], w_ref[...])


def _attn_out(x2d, o, st, mod, w_bf16):
    n, d = x2d.shape
    tm = QKV_TILE
    tile = pl.BlockSpec((tm, d), lambda i: (i, 0))
    return pl.pallas_call(
        _attn_out_kernel,
        name="attn_out",
        grid=(n // tm,),
        in_specs=[tile, tile, mod.row_spec(0, 2, lambda i: st.cond_row(i, tm)),
                  pl.BlockSpec((d, d), lambda i: (0, 0))],
        out_specs=tile,
        out_shape=jax.ShapeDtypeStruct((n, d), F32),
        compiler_params=_cparams(("arbitrary",)),
    )(x2d, o, mod.t4, w_bf16)


def _lru_in_kernel(y_ref, sh_ref, sc_ref, g_ref, w_ref, gate_ref, xbr_ref, *, d_rnn, latent):
    bsz, tt, d = y_ref.shape
    x = jnp.concatenate([y_ref[:, t, :] for t in range(tt)], axis=0)
    if latent:
        x3 = x.reshape(tt, bsz, d)
        hn = _norm_mod(x3, g_ref[...][None], sc_ref[0][None], sh_ref[0][None]).reshape(tt * bsz, d)
    else:
        hn = _norm_mod(x, g_ref[...], sc_ref[0, 0], sh_ref[0, 0])
    hn = hn.astype(BF16)
    gate_ref[...] = _dot(hn, w_ref[:, :d_rnn])
    xbr_ref[...] = _dot(hn, w_ref[:, d_rnn:])


def _lru_in(y3, st, mod, g, w_bf16):
    bsz, seq, d = y3.shape
    d_rnn = w_bf16.shape[1] // 2
    tt = QKV_TILE // bsz
    assert (not st.latent) or bsz == SUBLANES
    mspec = (lambda c: mod.latent_spec(1, c)) if st.latent else (lambda c: mod.row_spec(1, c, lambda t: 0))
    out_spec = pl.BlockSpec((tt * bsz, d_rnn), lambda t: (t, 0))
    out_shape = jax.ShapeDtypeStruct((seq * bsz, d_rnn), F32)
    return pl.pallas_call(
        functools.partial(_lru_in_kernel, d_rnn=d_rnn, latent=st.latent),
        name="lru_in",
        grid=(seq // tt,),
        in_specs=[
            pl.BlockSpec((bsz, tt, d), lambda t: (0, t, 0)),
            mspec(0), mspec(1),
            pl.BlockSpec((1, d), lambda t: (0, 0)),
            pl.BlockSpec((d, 2 * d_rnn), lambda t: (0, 0)),
        ],
        out_specs=[out_spec, out_spec],
        out_shape=[out_shape, out_shape],
        compiler_params=_cparams(("arbitrary",)),
    )(y3, mod.t3 if st.latent else mod.t4, mod.t3 if st.latent else mod.t4, g, w_bf16)


def _scan_kernel(prev_ref, cur_ref, next_ref, cw_ref, cb_ref, wa_ref, wx_ref, ba_ref, bx_ref, lam_ref,
                 h0_ref, h_ref, hfin_ref, a_scr, b_scr, carry, *, batch, steps, nchunk):
    rev = pl.program_id(0) == 1
    c = pl.program_id(1)
    c_eff = jnp.where(rev, nchunk - 1 - c, c)
    rows = steps * batch
    cb = wa_ref.shape[2]
    for n in range(wa_ref.shape[1]):
        ch = slice(n * cb, (n + 1) * cb)
        prev = jnp.where(c_eff == 0, 0.0, prev_ref[:, ch])
        nxt = jnp.where(c_eff == nchunk - 1, 0.0, next_ref[:, ch])
        xfull = jnp.concatenate([prev, cur_ref[:, ch], nxt], axis=0)
        cw = cw_ref[:, ch]
        xc = cw[0:1, :] * xfull[0:rows]
        for j in range(1, cw.shape[0]):
            xc = xc + cw[j:j + 1, :] * xfull[j * batch:j * batch + rows]
        xc = xc + cb_ref[:, ch]
        xcb = xc.astype(BF16)
        t_r = jnp.tanh(_dot(xcb, wa_ref[0, n]) + ba_ref[0, :, ch])
        t_i = jnp.tanh(_dot(xcb, wx_ref[0, n]) + bx_ref[0, :, ch])
        neg_lam = -lam_ref[0, :, ch]
        softplus = jnp.maximum(neg_lam, 0.0) + jnp.log1p(jnp.exp(-jnp.abs(neg_lam)))
        log_a = ((-0.5 * LRU_C) * softplus) * (t_r + 1.0)
        a = jnp.exp(log_a)
        a_scr[:, ch] = a
        b_scr[:, ch] = jnp.sqrt(jnp.tanh(-log_a) * (1.0 + a * a)) * ((0.5 * xc) * (t_i + 1.0))

    @pl.when(c == 0)
    def _():
        carry[...] = h0_ref[0]

    def step(t, h):
        te = jnp.where(rev, steps - 1 - t, t)
        r0 = pl.multiple_of(te * batch, batch)
        h = a_scr[pl.ds(r0, batch), :] * h + b_scr[pl.ds(r0, batch), :]
        h_ref[0, pl.ds(r0, batch), :] = h
        return h

    h = lax.fori_loop(0, steps, step, carry[...], unroll=min(steps, 8))
    carry[...] = h

    @pl.when(c == nchunk - 1)
    def _():
        hfin_ref[0] = h


def _lru_scan(x2, batch, seq, conv_w, conv_b, wa_bf16, wx_bf16, ba, bx, lam, h0):
    d_rnn = conv_w.shape[1]
    nb, cb = wa_bf16.shape[1], wa_bf16.shape[2]
    assert batch % SUBLANES == 0 and conv_w.shape[0] == 4 and nb * cb == d_rnn
    steps = max(SCAN_ROWS // batch, 2)
    rows = steps * batch
    nchunk = seq // steps
    halo = 2 * batch
    hpc = rows // halo
    n_halo = seq * batch // halo
    ce = lambda d, c: jnp.where(d == 1, nchunk - 1 - c, c)
    vec = lambda a: a.reshape(2, 1, d_rnn)
    vspec = pl.BlockSpec((1, 1, d_rnn), lambda d, c: (d, 0, 0))
    wspec = pl.BlockSpec((1, nb, cb, cb), lambda d, c: (d, 0, 0, 0))
    return pl.pallas_call(
        functools.partial(_scan_kernel, batch=batch, steps=steps, nchunk=nchunk),
        name="lru_scan",
        grid=(2, nchunk),
        in_specs=[
            pl.BlockSpec((halo, d_rnn), lambda d, c: (jnp.maximum(ce(d, c) * hpc - 1, 0), 0)),
            pl.BlockSpec((rows, d_rnn), lambda d, c: (ce(d, c), 0)),
            pl.BlockSpec((halo, d_rnn), lambda d, c: (jnp.minimum((ce(d, c) + 1) * hpc, n_halo - 1), 0)),
            pl.BlockSpec((4, d_rnn), lambda d, c: (0, 0)),
            pl.BlockSpec((1, d_rnn), lambda d, c: (0, 0)),
            wspec, wspec, vspec, vspec, vspec,
            pl.BlockSpec((1, batch, d_rnn), lambda d, c: (d, 0, 0)),
        ],
        out_specs=[
            pl.BlockSpec((1, rows, d_rnn), lambda d, c: (d, ce(d, c), 0)),
            pl.BlockSpec((1, batch, d_rnn), lambda d, c: (d, 0, 0)),
        ],
        out_shape=[
            jax.ShapeDtypeStruct((2, seq * batch, d_rnn), F32),
            jax.ShapeDtypeStruct((2, batch, d_rnn), F32),
        ],
        scratch_shapes=[pltpu.VMEM((rows, d_rnn), F32), pltpu.VMEM((rows, d_rnn), F32),
                        pltpu.VMEM((batch, d_rnn), F32)],
        compiler_params=_cparams(("arbitrary", "arbitrary")),
    )(x2, x2, x2, conv_w, conv_b[None, :], wa_bf16, wx_bf16, vec(ba), vec(bx), vec(lam), h0)


def _lru_out_kernel(y_ref, gate_ref, h_ref, gt_ref, w_ref, o_ref, z_scr, *, latent):
    bsz, tt, d = y_ref.shape
    z = jax.nn.gelu(gate_ref[...]) * (h_ref[0] + h_ref[1])
    z_scr[...] = z.reshape(tt, bsz, z.shape[1])
    zb = jnp.concatenate([z_scr[:, b, :] for b in range(bsz)], axis=0)
    out = _dot(zb.astype(BF16), w_ref[...]).reshape(bsz, tt, d)
    gt = gt_ref[0][:, None, :] if latent else gt_ref[0]
    o_ref[...] = y_ref[...] + gt * out


def _lru_out(y3, gate, h, st, mod, w_bf16):
    bsz, seq, d = y3.shape
    d_rnn = w_bf16.shape[0]
    tt = QKV_TILE // bsz
    gspec = mod.latent_spec(1, 2) if st.latent else mod.row_spec(1, 2, lambda t: 0)
    return pl.pallas_call(
        functools.partial(_lru_out_kernel, latent=st.latent),
        name="lru_out",
        grid=(seq // tt,),
        in_specs=[
            pl.BlockSpec((bsz, tt, d), lambda t: (0, t, 0)),
            pl.BlockSpec((tt * bsz, d_rnn), lambda t: (t, 0)),
            pl.BlockSpec((2, tt * bsz, d_rnn), lambda t: (0, t, 0)),
            gspec,
            pl.BlockSpec((d_rnn, d), lambda t: (0, 0)),
        ],
        out_specs=pl.BlockSpec((bsz, tt, d), lambda t: (0, t, 0)),
        out_shape=jax.ShapeDtypeStruct((bsz, seq, d), F32),
        scratch_shapes=[pltpu.VMEM((tt, bsz, d_rnn), F32)],
        compiler_params=_cparams(("arbitrary",)),
    )(y3, gate, h, mod.t3 if st.latent else mod.t4, w_bf16)


def _dual_specs(shape, sp, ss, extra=0):
    pmap = lambda i, *_: (jnp.minimum(i, sp.tiles - 1), 0)
    smap = lambda i, *_: (jnp.maximum(i - sp.tiles, 0), 0)
    return pl.BlockSpec(shape, pmap), pl.BlockSpec(shape, smap)


def _joint_cond_row(sp, ss):
    return lambda i, *_: jnp.where(i < sp.tiles, 0, ss.cond_row(jnp.maximum(i - sp.tiles, 0)))


def _router_kernel(yp_ref, ys_ref, sh_ref, sc_ref, g_ref, rwt_ref, rb_ref, bkt_ref, rank_ref, wext_ref,
                   cnt_ref, carry, *, n_experts, ntp, n_steps):
    i = pl.program_id(0)

    @pl.when(i == 0)
    def _():
        carry[...] = jnp.zeros_like(carry)

    y = jnp.where(i < ntp, yp_ref[...], ys_ref[...])
    hn = _norm_mod(y, g_ref[...], sc_ref[0, 0], sh_ref[0, 0])
    tm = hn.shape[0]
    logits = lax.dot_general(rwt_ref[...], hn, (((1,), (1,)), ((), ())),
                             precision=lax.Precision.HIGHEST, preferred_element_type=F32)
    scores = jax.nn.sigmoid(logits)
    sel = scores + rb_ref[...]
    per = n_experts // N_GROUPS
    assert per == 4
    srow = [sel[e:e + 1, :] for e in range(n_experts)]
    prow = [scores[e:e + 1, :] for e in range(n_experts)]

    def top2_sum(a, b, c, d):
        hi1, lo1, hi2, lo2 = jnp.maximum(a, b), jnp.minimum(a, b), jnp.maximum(c, d), jnp.minimum(c, d)
        return jnp.maximum(hi1, hi2) + jnp.maximum(jnp.minimum(hi1, hi2), jnp.maximum(lo1, lo2))

    gscore = [top2_sum(*srow[per * g:per * (g + 1)]) for g in range(N_GROUPS)]
    best = jnp.zeros((1, tm), I32)
    best_v = gscore[0]
    for g in range(1, N_GROUPS):
        upd = gscore[g] > best_v
        best = jnp.where(upd, g, best)
        best_v = jnp.where(upd, gscore[g], best_v)
    vs = list(srow[:per])
    ps = list(prow[:per])
    for g in range(1, N_GROUPS):
        for j in range(per):
            vs[j] = jnp.where(best == g, srow[per * g + j], vs[j])
            ps[j] = jnp.where(best == g, prow[per * g + j], ps[j])
    j1 = jnp.zeros((1, tm), I32)
    m1 = vs[0]
    for j in range(1, per):
        upd = vs[j] > m1
        j1 = jnp.where(upd, j, j1)
        m1 = jnp.where(upd, vs[j], m1)
    j2 = jnp.full((1, tm), -1, I32)
    m2 = jnp.full((1, tm), -jnp.inf, F32)
    for j in range(per):
        upd = (j1 != j) & (vs[j] > m2)
        j2 = jnp.where(upd, j, j2)
        m2 = jnp.where(upd, vs[j], m2)
    s1 = jnp.zeros((1, tm), F32)
    s2 = jnp.zeros((1, tm), F32)
    for j in range(per):
        s1 = jnp.where(j1 == j, ps[j], s1)
        s2 = jnp.where(j2 == j, ps[j], s2)
    den = s1 + s2
    w1, w2 = s1 / den, s2 / den
    lo = jnp.minimum(j1, j2)
    hi = jnp.maximum(j1, j2)
    w_lo = jnp.where(j1 < j2, w1, w2)
    w_hi = jnp.where(j1 < j2, w2, w1)
    base = jnp.where(lo == 0, 0, jnp.where(lo == 1, 3, 5))
    bucket = best * 6 + base + hi - lo - 1

    onehot = (lax.broadcasted_iota(I32, (N_BUCKET_SLOTS, tm), 0) == bucket).astype(F32)
    before = (lax.broadcasted_iota(I32, (tm, tm), 0) < lax.broadcasted_iota(I32, (tm, tm), 1)).astype(BF16)
    prefix = _dot(onehot.astype(BF16), before) + carry[...]
    rank = jnp.sum(onehot * prefix, axis=0, keepdims=True)
    carry[...] = carry[...] + jnp.sum(onehot, axis=1, keepdims=True)

    bkt_ref[0] = bucket
    rank_ref[0] = rank.astype(I32)
    slab_row = lax.broadcasted_iota(I32, (LANES, tm), 0)
    slab = jnp.where(slab_row == 0, w_lo, jnp.where(slab_row == 1, w_hi, 0.0))
    wext_ref[...] = slab.T

    @pl.when(i == n_steps - 1)
    def _():
        cnt_ref[...] = jnp.broadcast_to(carry[...], cnt_ref.shape).astype(I32)


def _router(yp, ys, sp, ss, mod, layer, g, router_wt, router_b):
    d = yp.shape[1]
    e = router_wt.shape[0]
    tm = TOKEN_TILE
    nt = sp.tiles + ss.tiles
    row = _joint_cond_row(sp, ss)
    tok_spec = pl.BlockSpec((1, 1, tm), lambda i: (i, 0, 0))
    pspec, sspec = _dual_specs((tm, d), sp, ss)
    return pl.pallas_call(
        functools.partial(_router_kernel, n_experts=e, ntp=sp.tiles, n_steps=nt),
        name="moe_router",
        grid=(nt,),
        in_specs=[
            pspec, sspec,
            mod.row_spec(layer, 3, row),
            mod.row_spec(layer, 4, row),
            pl.BlockSpec((1, d), lambda i: (0, 0)),
            pl.BlockSpec((e, d), lambda i: (0, 0)),
            pl.BlockSpec((e, 1), lambda i: (0, 0)),
        ],
        out_specs=[tok_spec, tok_spec,
                   pl.BlockSpec((tm, LANES), lambda i: (i, 0)),
                   pl.BlockSpec((N_BUCKET_SLOTS, LANES), lambda i: (0, 0))],
        out_shape=[jax.ShapeDtypeStruct((nt, 1, tm), I32), jax.ShapeDtypeStruct((nt, 1, tm), I32),
                   jax.ShapeDtypeStruct((nt * tm, LANES), F32),
                   jax.ShapeDtypeStruct((N_BUCKET_SLOTS, LANES), I32)],
        scratch_shapes=[pltpu.VMEM((N_BUCKET_SLOTS, 1), F32)],
        compiler_params=_cparams(("arbitrary",)),
    )(yp, ys, mod.t4, mod.t4, g, router_wt, router_b[:, None])


def _bucket_layout(counts, n_tokens):
    n_buckets = N_GROUPS * len(PAIR_LO)
    tf = FFN_TILE
    n_tiles = n_tokens // tf + n_buckets + 2
    c = counts[:n_buckets]
    padded = ((c + tf - 1) // tf) * tf
    ends = jnp.cumsum(padded)
    starts = ends - padded
    n_used = ends[-1] // tf
    tile = jnp.arange(n_tiles, dtype=I32)
    valid = tile < n_used
    tile_c = jnp.minimum(tile, n_used - 1)
    tb = jnp.sum((ends[None, :] <= (tile_c * tf)[:, None]).astype(I32), axis=1)
    tb = jnp.minimum(tb, n_buckets - 1)
    grp, pair = tb // len(PAIR_LO), tb % len(PAIR_LO)
    pair_lo = sum(jnp.where(pair == p, PAIR_LO[p], 0) for p in range(len(PAIR_LO)))
    pair_hi = sum(jnp.where(pair == p, PAIR_HI[p], 0) for p in range(len(PAIR_HI)))
    slots = lambda v: jnp.concatenate([v.astype(I32), jnp.zeros((N_BUCKET_SLOTS - n_buckets,), I32)])
    sched = jnp.stack([tile_c, grp * 4 + pair_lo, grp * 4 + pair_hi, valid.astype(I32)]).astype(I32)
    fill = slots((starts + c) // SUBLANES).at[n_buckets].set(n_used.astype(I32))
    return slots(starts), fill, sched, n_tiles


def _pos_kernel(starts_ref, bkt_ref, rank_ref, pos_ref):
    bkt = bkt_ref[...]
    pos = rank_ref[...]
    for b in range(N_GROUPS * len(PAIR_LO)):
        pos = pos + jnp.where(bkt == b, starts_ref[b], 0)
    pos_ref[...] = pos


def _positions(starts32, bkt, rank):
    nt, _, tm = bkt.shape
    grid_spec = pltpu.PrefetchScalarGridSpec(
        num_scalar_prefetch=1,
        grid=(1,),
        in_specs=[pl.BlockSpec((nt, 1, tm), lambda i, st: (0, 0, 0))] * 2,
        out_specs=pl.BlockSpec((nt, 1, tm), lambda i, st: (0, 0, 0)),
    )
    return pl.pallas_call(
        _pos_kernel, name="moe_positions", grid_spec=grid_spec,
        out_shape=jax.ShapeDtypeStruct((nt, 1, tm), I32),
        compiler_params=_cparams(("arbitrary",)),
    )(starts32, bkt, rank)


def _dispatch_kernel(fill_ref, yp_ref, ys_ref, sh_ref, sc_ref, g_ref, wext_ref, pos_ref,
                     xs_ref, buf0, buf1, zeros, sem0, sem1, *, d, ntp, n_steps, n_buckets):
    groups, sub, width = buf0.shape
    i = pl.program_id(0)
    y = jnp.where(i < ntp, yp_ref[...], ys_ref[...])
    hn = _norm_mod(y, g_ref[...], sc_ref[0, 0], sh_ref[0, 0])

    @pl.when(i == 0)
    def _():
        zgroups = zeros.shape[0] - 1
        zeros[...] = jnp.zeros_like(zeros)
        fills = [pltpu.make_async_copy(zeros, xs_ref.at[pl.ds(fill_ref[b], zgroups + 1)], sem1)
                 for b in range(n_buckets)]
        for copy in fills:
            copy.start()
        for copy in fills:
            copy.wait()

        def clear_tile(t, carry):
            copy = pltpu.make_async_copy(zeros.at[pl.ds(0, zgroups)], xs_ref.at[pl.ds(t * zgroups, zgroups)], sem1)
            copy.start()
            copy.wait()
            return carry

        lax.fori_loop(fill_ref[n_buckets], xs_ref.shape[0] // zgroups, clear_tile, 0)

    def wait_tile(buf, sem):
        pltpu.make_async_copy(buf, xs_ref.at[pl.ds(0, groups)], sem).wait()

    def step(buf, sem, prev_buf, prev_sem):
        buf[:, :, :d] = hn.reshape(groups, sub, d)
        buf[:, :, d:] = wext_ref[...].reshape(groups, sub, width - d)

        def issue(g, carry):
            for u in range(sub):
                pos = pos_ref[0, 0, g * sub + u]
                dst = xs_ref.at[lax.shift_right_logical(pos, 3), pl.ds(pos & (sub - 1), 1)]
                pltpu.make_async_copy(buf.at[g, pl.ds(u, 1)], dst, sem).start()
            return carry

        lax.fori_loop(0, groups, issue, 0)

        @pl.when(i > 0)
        def _():
            wait_tile(prev_buf, prev_sem)

        @pl.when(i == n_steps - 1)
        def _():
            wait_tile(buf, sem)

    @pl.when(i % 2 == 0)
    def _():
        step(buf0, sem0, buf1, sem1)

    @pl.when(i % 2 == 1)
    def _():
        step(buf1, sem1, buf0, sem0)


def _dispatch(yp, ys, sp, ss, mod, layer, g, wext, pos, fill_groups, n_rows):
    d = yp.shape[1]
    tm = TOKEN_TILE
    width = d + LANES
    groups = tm // SUBLANES
    row = _joint_cond_row(sp, ss)
    pspec, sspec = _dual_specs((tm, d), sp, ss)
    tile_buf = pltpu.VMEM((groups, SUBLANES, width), F32)
    grid_spec = pltpu.PrefetchScalarGridSpec(
        num_scalar_prefetch=1,
        grid=(sp.tiles + ss.tiles,),
        in_specs=[
            pspec, sspec,
            mod.row_spec(layer, 3, row), mod.row_spec(layer, 4, row),
            pl.BlockSpec((1, d), lambda i, f: (0, 0)),
            pl.BlockSpec((tm, LANES), lambda i, f: (i, 0)),
            pl.BlockSpec((1, 1, tm), lambda i, f: (i, 0, 0), memory_space=pltpu.SMEM),
        ],
        out_specs=pl.BlockSpec(memory_space=pl.ANY),
        scratch_shapes=[tile_buf, tile_buf, pltpu.VMEM((FFN_TILE // SUBLANES + 1, SUBLANES, width), F32),
                        pltpu.SemaphoreType.DMA(()), pltpu.SemaphoreType.DMA(())],
    )
    return pl.pallas_call(
        functools.partial(_dispatch_kernel, d=d, ntp=sp.tiles, n_steps=sp.tiles + ss.tiles,
                          n_buckets=N_GROUPS * len(PAIR_LO)),
        name="moe_dispatch",
        grid_spec=grid_spec,
        out_shape=jax.ShapeDtypeStruct((n_rows // SUBLANES, SUBLANES, width), F32),
        compiler_params=_cparams(("arbitrary",), disable_bounds_checks=True),
    )(fill_groups, yp, ys, mod.t4, mod.t4, g, wext, pos).reshape(n_rows, width)


def _ffn_kernel(sched_ref, xs_ref, wg_lo, wu_lo, wd_lo, wg_hi, wu_hi, wd_hi, o_ref, *, d):
    valid = sched_ref[3, pl.program_id(0)] == 1

    @pl.when(jnp.logical_not(valid))
    def _():
        o_ref[...] = jnp.zeros_like(o_ref)

    @pl.when(valid)
    def _():
        x = xs_ref[:, :d].astype(BF16)
        gates = [_dot(x, wg[0, 0]) for wg in (wg_lo, wg_hi)]
        ups = [_dot(x, wu[0, 0]) for wu in (wu_lo, wu_hi)]
        hes = [((g * jax.nn.sigmoid(g)) * u).astype(BF16) for g, u in zip(gates, ups)]
        w_lo = xs_ref[:, d:d + 1]
        w_hi = xs_ref[:, d + 1:d + 2]
        o_ref[...] = w_lo * _dot(hes[0], wd_lo[0, 0]) + w_hi * _dot(hes[1], wd_hi[0, 0])


def _expert_ffn(xs, sched, n_tiles, layer, wg, wu, wd):
    n_rows, width = xs.shape
    d = width - LANES
    f = wg.shape[3]
    tf = FFN_TILE
    lo = lambda i, s: (layer, s[1, i], 0, 0)
    hi = lambda i, s: (layer, s[2, i], 0, 0)
    grid_spec = pltpu.PrefetchScalarGridSpec(
        num_scalar_prefetch=1,
        grid=(n_tiles,),
        in_specs=[
            pl.BlockSpec((tf, width), lambda i, s: (s[0, i], 0)),
            pl.BlockSpec((1, 1, d, f), lo), pl.BlockSpec((1, 1, d, f), lo), pl.BlockSpec((1, 1, f, d), lo),
            pl.BlockSpec((1, 1, d, f), hi), pl.BlockSpec((1, 1, d, f), hi), pl.BlockSpec((1, 1, f, d), hi),
        ],
        out_specs=pl.BlockSpec((tf, d), lambda i, s: (i, 0)),
    )
    return pl.pallas_call(
        functools.partial(_ffn_kernel, d=d),
        name="moe_expert_ffn",
        grid_spec=grid_spec,
        out_shape=jax.ShapeDtypeStruct((n_rows, d), F32),
        compiler_params=_cparams(("arbitrary",)),
    )(sched, xs, wg, wu, wd, wg, wu, wd)


def _combine_kernel(y_ref, gt_ref, pos_ref, pos_next_ref, ms_ref, ms_groups_ref, o_ref, buf0, buf1, sem0, sem1, *,
                    n_steps):
    groups, sub, d = buf0.shape
    tm = groups * sub
    i = pl.program_id(0)

    def gather(pos, buf, sem):
        def issue(g, carry):
            for u in range(sub):
                pltpu.make_async_copy(ms_ref.at[pl.ds(pos[0, 0, g * sub + u], 1)], buf.at[g, pl.ds(u, 1)],
                                      sem).start()
            return carry

        lax.fori_loop(0, groups, issue, 0)

    def step(buf, sem, next_buf, next_sem):
        @pl.when(i == 0)
        def _():
            gather(pos_ref, buf, sem)

        @pl.when(i + 1 < n_steps)
        def _():
            gather(pos_next_ref, next_buf, next_sem)

        pltpu.make_async_copy(ms_groups_ref.at[pl.ds(0, groups)], buf, sem).wait()
        o_ref[...] = y_ref[...] + gt_ref[0, 0] * buf[...].reshape(tm, d)

    @pl.when(i % 2 == 0)
    def _():
        step(buf0, sem0, buf1, sem1)

    @pl.when(i % 2 == 1)
    def _():
        step(buf1, sem1, buf0, sem0)


def _combine(y2d, st, mod, layer, pos, ms):
    n, d = y2d.shape
    tm = TOKEN_TILE
    tile = pl.BlockSpec((tm, d), lambda i: (i, 0))
    last = st.tile_off + n // tm - 1
    return pl.pallas_call(
        functools.partial(_combine_kernel, n_steps=n // tm),
        name="moe_combine",
        grid=(n // tm,),
        in_specs=[
            tile,
            mod.row_spec(layer, 5, st.cond_row),
            pl.BlockSpec((1, 1, tm), lambda i: (st.tile_off + i, 0, 0), memory_space=pltpu.SMEM),
            pl.BlockSpec((1, 1, tm), lambda i: (jnp.minimum(st.tile_off + i + 1, last), 0, 0),
                         memory_space=pltpu.SMEM),
            pl.BlockSpec(memory_space=pl.ANY),
            pl.BlockSpec(memory_space=pl.ANY),
        ],
        out_specs=tile,
        out_shape=jax.ShapeDtypeStruct((n, d), F32),
        scratch_shapes=[pltpu.VMEM((tm // SUBLANES, SUBLANES, d), F32),
                        pltpu.VMEM((tm // SUBLANES, SUBLANES, d), F32),
                        pltpu.SemaphoreType.DMA(()), pltpu.SemaphoreType.DMA(())],
        compiler_params=_cparams(("arbitrary",), disable_bounds_checks=True),
    )(y2d, mod.t4, pos, pos, ms, ms.reshape(ms.shape[0] // SUBLANES, SUBLANES, d))


def _moe_residual(yp, ys, sp, ss, mod, layer, g, router_wt, router_b, wg, wu, wd):
    bkt, rank, wext, counts = _router(yp, ys, sp, ss, mod, layer, g, router_wt, router_b)
    starts32, fill_groups, sched, n_tiles = _bucket_layout(counts[:, 0], sp.n + ss.n)
    pos = _positions(starts32, bkt, rank)
    xs = _dispatch(yp, ys, sp, ss, mod, layer, g, wext, pos, fill_groups, n_tiles * FFN_TILE)
    ms = _expert_ffn(xs, sched, n_tiles, layer, wg, wu, wd)
    return _combine(yp, sp, mod, layer, pos, ms), _combine(ys, ss, mod, layer, pos, ms)


def kernel(x_prompt, x_sample, c, cache_k, cache_v, state_lru, c_ctx, ada_w, ada_b, norm_mix, norm_ffn,
           attn_w_in, attn_q_gain, attn_k_gain, attn_rpb, attn_w_out, lru_w_in, lru_conv_w, lru_conv_b,
           lru_gate_a_w, lru_gate_a_b, lru_gate_x_w, lru_gate_x_b, lru_lambda, lru_w_out, router_w,
           router_b, moe_w_gate, moe_w_up, moe_w_down):
    bp, tp, d = x_prompt.shape
    bs, ts, _ = x_sample.shape
    n_heads, head_dim = cache_k.shape[3], cache_k.shape[4]
    d_rnn = lru_conv_w.shape[2]
    assert ada_w.shape[0] == 2 and bs <= N_COND_ROWS - LATENT_COND_ROW
    assert tp % TOKEN_TILE == 0 and ts % TOKEN_TILE == 0
    sp = _Stream(bp, tp, 0, latent=False)
    ss = _Stream(bs, ts, sp.tiles, latent=True)

    cond = jnp.zeros((N_COND_ROWS, d), F32).at[0].set(c_ctx).at[LATENT_COND_ROW:LATENT_COND_ROW + bs].set(c)
    mod = _Mod(_ada_modulation(cond, ada_w, ada_b), d)

    w_in = attn_w_in[0].astype(BF16)
    w_out = attn_w_out[0].astype(BF16)
    g_mix0 = norm_mix[0][None, :]
    xp2, xs2 = x_prompt.reshape(sp.n, d), x_sample.reshape(ss.n, d)
    q_p, k_p, v_p, new_k, new_v = _qkv_proj(xp2, sp, mod, g_mix0, w_in, attn_q_gain[0], attn_k_gain[0],
                                            n_heads, True)
    q_s, k_s, v_s = _qkv_proj(xs2, ss, mod, g_mix0, w_in, attn_q_gain[0], attn_k_gain[0], n_heads, False)
    o_p = _ctx_attention(q_p, k_p, v_p, bp, tp)
    pair_major = lambda cache: jnp.transpose(
        cache[:, 0].astype(BF16).reshape(bs, cache.shape[2], n_heads // 2, LANES), (2, 0, 1, 3))
    ck, cv = pair_major(cache_k), pair_major(cache_v)
    o_s = _nbr_attention(q_s, k_s, v_s, ck, cv, attn_rpb[0], bs, ts)
    yp = _attn_out(xp2, o_p, sp, mod, w_out)
    ys = _attn_out(xs2, o_s, ss, mod, w_out)

    router_wt = router_w.T
    moe_w = (moe_w_gate.astype(BF16), moe_w_up.astype(BF16), moe_w_down.astype(BF16))
    yp, ys = _moe_residual(yp, ys, sp, ss, mod, 0, norm_ffn[0][None, :], router_wt, router_b, *moe_w)

    w_lru_in = lru_w_in[0].astype(BF16)
    w_lru_out = lru_w_out[0].astype(BF16)
    g_mix1 = norm_mix[1][None, :]
    scan_args = (lru_conv_w[0], lru_conv_b[0], (0.5 * lru_gate_a_w[0]).astype(BF16),
                 (0.5 * lru_gate_x_w[0]).astype(BF16), 0.5 * lru_gate_a_b[0], 0.5 * lru_gate_x_b[0],
                 lru_lambda[0])
    hfin = {}
    ys_out = []
    for y2, st, h0 in ((yp, sp, jnp.zeros((2, bp, d_rnn), F32)),
                       (ys, ss, jnp.transpose(state_lru[:, 0], (1, 0, 2)))):
        y3 = y2.reshape(st.batch, st.seq, d)
        gate, xbr = _lru_in(y3, st, mod, g_mix1, w_lru_in)
        h, hfin[st.latent] = _lru_scan(xbr, st.batch, st.seq, *scan_args, h0)
        ys_out.append(_lru_out(y3, gate, h, st, mod, w_lru_out).reshape(st.n, d))
    yp, ys = ys_out
    yp, ys = _moe_residual(yp, ys, sp, ss, mod, 1, norm_ffn[1][None, :], router_wt, router_b, *moe_w)

    return (yp.reshape(bp, tp, d), ys.reshape(bs, ts, d),
            new_k.reshape(bp, 1, tp, n_heads, head_dim), new_v.reshape(bp, 1, tp, n_heads, head_dim),
            jnp.transpose(hfin[False], (1, 0, 2))[:, None])
```

```python
import functools

import numpy as np
import jax
import jax.numpy as jnp
from jax import lax
from jax.experimental import pallas as pl
from jax.experimental.pallas import tpu as pltpu

F32 = jnp.float32
BF16 = jnp.bfloat16
I32 = jnp.int32

EPS = 1e-6
NEG_INF = -1e30
GRID_W = 64
WIN_ROWS = 8
WIN_COLS = 16
N_GROUPS = 4
LRU_C = 8.0
N_LRU_BLOCKS = 4

LANES = 128
SUBLANES = 8
MXU_COLS = 256
TOKEN_TILE = 256
QKV_TILE = 512
FFN_TILE = 256
Q_ROWS = 4
SCAN_ROWS = 512
ATTN_PAIRS = 2
N_BUCKET_SLOTS = 32
PAIR_LO = (0, 0, 0, 1, 1, 2)
PAIR_HI = (1, 2, 3, 2, 3, 3)
N_COND_ROWS = 16
LATENT_COND_ROW = 8
VMEM_LIMIT = 48 * 1024 * 1024

def _cparams(sem, **kw):
    return pltpu.CompilerParams(dimension_semantics=sem, vmem_limit_bytes=VMEM_LIMIT, **kw)


def _norm_mod(x, g, sc, sh):
    ms = jnp.mean(x * x, axis=-1, keepdims=True)
    return (x * lax.rsqrt(ms + EPS)) * g * (1.0 + sc) + sh


def _dot(a, b):
    return jnp.dot(a, b, preferred_element_type=F32)


def _dot_nt(a, b):
    return lax.dot_general(a, b, (((1,), (1,)), ((), ())), preferred_element_type=F32)


def _ada_kernel(cond_ref, w_ref, b_ref, o_ref):
    c = cond_ref[...]
    s = (c * jax.nn.sigmoid(c)).astype(BF16)
    o_ref[0] = _dot(s, w_ref[0].astype(BF16)) + b_ref[0]


def _ada_modulation(cond, ada_w, ada_b):
    depth, d, d6 = ada_w.shape
    tn = d6 // 4
    return pl.pallas_call(
        _ada_kernel,
        name="ada_modulation",
        grid=(depth, d6 // tn),
        in_specs=[
            pl.BlockSpec((N_COND_ROWS, d), lambda l, j: (0, 0)),
            pl.BlockSpec((1, d, tn), lambda l, j: (l, 0, j)),
            pl.BlockSpec((1, 1, tn), lambda l, j: (l, 0, j)),
        ],
        out_specs=pl.BlockSpec((1, N_COND_ROWS, tn), lambda l, j: (l, 0, j)),
        out_shape=jax.ShapeDtypeStruct((depth, N_COND_ROWS, d6), F32),
        compiler_params=_cparams(("arbitrary", "arbitrary")),
    )(cond, ada_w, ada_b.reshape(depth, 1, d6))


class _Mod:
    def __init__(self, table, d):
        self.t3 = table
        self.t4 = table.reshape(table.shape[0], N_COND_ROWS, 1, table.shape[2])
        self.d = d

    def row_spec(self, layer, chunk, row_fn):
        return pl.BlockSpec((1, 1, 1, self.d), lambda *ids: (layer, row_fn(*ids), 0, chunk))

    def latent_spec(self, layer, chunk):
        return pl.BlockSpec((1, SUBLANES, self.d), lambda *ids: (layer, LATENT_COND_ROW // SUBLANES, chunk))


class _Stream:
    def __init__(self, batch, seq, tile_off, latent):
        self.batch, self.seq, self.tile_off, self.latent = batch, seq, tile_off, latent
        self.n = batch * seq
        self.tiles = self.n // TOKEN_TILE

    def cond_row(self, i, tile=TOKEN_TILE):
        if not self.latent:
            return 0
        return LATENT_COND_ROW + (i * tile) // self.seq


def _qkv_kernel(x_ref, sh_ref, sc_ref, g_ref, w_ref, qg_ref, kg_ref, gmean_ref, q_ref, k_ref, v_ref, *kv_refs,
                d, head_dim):
    hn = _norm_mod(x_ref[...], g_ref[...], sc_ref[0, 0], sh_ref[0, 0]).astype(BF16)
    outs = (q_ref, k_ref, v_ref)
    gains = (qg_ref, kg_ref)
    n_chunks = 3 * d // MXU_COLS
    n_norm = 2 * d // MXU_COLS
    accs = [_dot(hn, w_ref[:, MXU_COLS * j:MXU_COLS * (j + 1)]) for j in range(n_chunks)]
    means = [_dot((accs[j] * accs[j]).astype(BF16), gmean_ref[...]) for j in range(n_norm)]
    for j in range(n_chunks):
        part = (MXU_COLS * j) // d
        col = (MXU_COLS * j) % d
        acc = accs[j]
        if j < n_norm:
            acc = (acc * lax.rsqrt(means[j] + EPS)) * gains[part][...]
        for half in range(MXU_COLS // LANES):
            outs[part][col // LANES + half] = acc[:, LANES * half:LANES * (half + 1)].astype(BF16)
        if part >= 1 and kv_refs:
            kv_refs[part - 1][:, col:col + MXU_COLS] = acc


def _qkv_proj(x2d, st, mod, g, w_bf16, q_gain, k_gain, n_heads, with_kv):
    n, d = x2d.shape
    head_dim = d // n_heads
    assert head_dim * 2 == LANES
    tm = QKV_TILE
    n_pairs = n_heads // 2
    hspec = pl.BlockSpec((n_pairs, tm, LANES), lambda i: (0, i, 0))
    hshape = jax.ShapeDtypeStruct((n_pairs, n, LANES), BF16)
    out_specs = [hspec, hspec, hspec]
    out_shape = [hshape, hshape, hshape]
    if with_kv:
        out_specs += [pl.BlockSpec((tm, d), lambda i: (i, 0))] * 2
        out_shape += [jax.ShapeDtypeStruct((n, d), F32)] * 2
    lane_head = np.arange(MXU_COLS) // head_dim
    assert float(np.log2(head_dim)).is_integer()
    gmean = jnp.asarray((lane_head[:, None] == lane_head[None, :]) / head_dim, BF16)
    reps = MXU_COLS // head_dim
    qg = jnp.tile(q_gain * _attn_scale(head_dim), reps)[None, :]
    kg = jnp.tile(k_gain, reps)[None, :]
    const = lambda shape: pl.BlockSpec(shape, lambda i: (0,) * len(shape))
    return pl.pallas_call(
        functools.partial(_qkv_kernel, d=d, head_dim=head_dim),
        name="qkv_proj",
        grid=(n // tm,),
        in_specs=[
            pl.BlockSpec((tm, d), lambda i: (i, 0)),
            mod.row_spec(0, 0, lambda i: st.cond_row(i, tm)),
            mod.row_spec(0, 1, lambda i: st.cond_row(i, tm)),
            const((1, d)), const((d, 3 * d)), const((1, MXU_COLS)), const((1, MXU_COLS)),
            const((MXU_COLS, MXU_COLS)),
        ],
        out_specs=out_specs,
        out_shape=out_shape,
        compiler_params=_cparams(("arbitrary",)),
    )(x2d, mod.t4, mod.t4, g, w_bf16, qg, kg, gmean)


def _attn_scale(head_dim):
    scale = head_dim ** -0.5
    assert float(np.log2(scale)).is_integer()
    return scale


def _softmax_pv(scores, values):
    m = scores[0].max(axis=-1, keepdims=True)
    for s in scores[1:]:
        m = jnp.maximum(m, s.max(axis=-1, keepdims=True))
    es = [jnp.exp(s - m) for s in scores]
    l = es[0].sum(axis=-1, keepdims=True)
    for e in es[1:]:
        l = l + e.sum(axis=-1, keepdims=True)
    o = _dot(es[0].astype(BF16), values[0])
    for e, v in zip(es[1:], values[1:]):
        o = o + _dot(e.astype(BF16), v)
    return o / l


def _pair_attention(q2, score_keys, values, biases):
    rows = q2.shape[0]
    first = lax.broadcasted_iota(I32, q2.shape, 1) < q2.shape[1] // 2
    zero = jnp.zeros_like(q2)
    q_both = jnp.concatenate([jnp.where(first, q2, zero), jnp.where(first, zero, q2)], axis=0)
    s_both = [_dot_nt(q_both, keys) for keys in score_keys]
    outs = []
    for head in range(2):
        scores = []
        for s, bias in zip(s_both, biases[head]):
            s = s[head * rows:(head + 1) * rows]
            scores.append(s if bias is None else s + bias)
        outs.append(_softmax_pv(scores, values))
    return jnp.where(first, outs[0], outs[1])


def _ctx_attn_kernel(q_ref, k_ref, v_ref, o_ref):
    for hp in range(q_ref.shape[0]):
        o = _pair_attention(q_ref[hp], [k_ref[hp]], [v_ref[hp]], [[None], [None]])
        o_ref[:, LANES * hp:LANES * (hp + 1)] = o.astype(BF16)


def _ctx_attention(q, k, v, batch, seq):
    n_pairs, n, _ = q.shape
    spec = pl.BlockSpec((n_pairs, seq, LANES), lambda b: (0, b, 0))
    return pl.pallas_call(
        _ctx_attn_kernel,
        name="ctx_attention",
        grid=(batch,),
        in_specs=[spec, spec, spec],
        out_specs=pl.BlockSpec((seq, n_pairs * LANES), lambda b: (b, 0)),
        out_shape=jax.ShapeDtypeStruct((n, n_pairs * LANES), BF16),
        compiler_params=_cparams(("arbitrary",)),
    )(q, k, v)


def _nbr_geometry(rows):
    n_key_rows = Q_ROWS + WIN_ROWS
    assert rows % Q_ROWS == 0 and rows >= n_key_rows
    nblk = rows // Q_ROWS
    kstart = np.clip(np.arange(nblk) * Q_ROWS - WIN_ROWS // 2, 0, rows - n_key_rows)
    start = np.clip(np.arange(rows) - WIN_ROWS // 2, 0, rows - WIN_ROWS)
    tables, variant = [], []
    for blk in range(nblk):
        r = blk * Q_ROWS + np.arange(Q_ROWS)
        kr = kstart[blk] + np.arange(n_key_rows)
        valid = (kr[None, :] >= start[r][:, None]) & (kr[None, :] < start[r][:, None] + WIN_ROWS)
        dr = np.clip(kr[None, :] - r[:, None] + WIN_ROWS - 1, 0, 2 * WIN_ROWS - 2)
        key = (valid.tobytes(), dr.tobytes())
        keys = [t[0] for t in tables]
        if key not in keys:
            tables.append((key, valid, dr))
        variant.append([t[0] for t in tables].index(key))
    return nblk, n_key_rows, kstart, np.array(variant), [(t[1], t[2]) for t in tables]


def _nbr_bias(rpb, tables):
    h, nd, nc = rpb.shape
    w = GRID_W
    period = 2 * w
    pad_l = (w - 1) - (WIN_COLS - 1)
    r_ext = jnp.pad(rpb.astype(F32), ((0, 0), (0, 0), (pad_l, period - nc - pad_l)))
    toe = jnp.tile(r_ext, (1, 1, w))[:, :, :w * (period - 1)].reshape(h, nd, w, period - 1)
    toe = toe[..., w - 1:2 * w - 1]
    cw = np.arange(w)
    col_start = np.clip(cw - WIN_COLS // 2, 0, w - WIN_COLS)
    col_ok = (cw[None, :] >= col_start[:, None]) & (cw[None, :] < col_start[:, None] + WIN_COLS)
    toe = jnp.where(col_ok[None, None], toe, NEG_INF)
    tq = jnp.transpose(toe, (0, 2, 1, 3)).reshape(h, w, nd * w)
    segments = []
    for valid, dr in tables:
        seg = []
        for i in range(valid.shape[0]):
            js = np.nonzero(valid[i])[0]
            j0, j1 = int(js[0]), int(js[-1]) + 1
            assert (np.diff(dr[i, j0:j1]) == 1).all() and valid[i, j0:j1].all()
            seg.append((j0, j1, int(dr[i, j0])))
        segments.append(seg)
    qr, kr = tables[0][0].shape
    return pl.pallas_call(
        functools.partial(_bias_kernel, segments=segments, w=w),
        name="nbr_bias",
        grid=(len(tables), h),
        in_specs=[pl.BlockSpec((1, w, nd * w), lambda v, hh: (hh, 0, 0))],
        out_specs=pl.BlockSpec((1, 1, qr * w, kr * w), lambda v, hh: (v, hh, 0, 0)),
        out_shape=jax.ShapeDtypeStruct((len(tables), h, qr * w, kr * w), F32),
        compiler_params=_cparams(("arbitrary", "arbitrary")),
    )(tq)


def _bias_kernel(tq_ref, o_ref, *, segments, w):
    for k, seg in enumerate(segments):
        @pl.when(pl.program_id(0) == k)
        def _(seg=seg):
            o_ref[...] = jnp.full(o_ref.shape, NEG_INF, F32)
            for i, (j0, j1, d0) in enumerate(seg):
                o_ref[0, 0, i * w:(i + 1) * w, j0 * w:j1 * w] = tq_ref[0, :, d0 * w:(d0 + j1 - j0) * w]


def _nbr_attn_kernel(kstart_ref, q_ref, k_ref, v_ref, ck_ref, cv_ref, bias_ref, o_ref, *, n_local, nblk):
    blk = pl.program_id(2)
    k0 = pl.multiple_of(kstart_ref[blk] * GRID_W, GRID_W)
    variant = kstart_ref[nblk + blk]
    for hp in range(q_ref.shape[0]):
        kl = k_ref[hp, pl.ds(k0, n_local), :]
        vl = v_ref[hp, pl.ds(k0, n_local), :]
        biases = [[bias_ref[variant, 2 * hp + head], None] for head in range(2)]
        o = _pair_attention(q_ref[hp], [kl, ck_ref[hp, 0]], [vl, cv_ref[hp, 0]], biases)
        o_ref[:, LANES * hp:LANES * (hp + 1)] = o.astype(BF16)


def _nbr_attention(q, k, v, ck, cv, rpb, batch, seq):
    n_pairs, n, _ = q.shape
    rows = seq // GRID_W
    nblk, n_key_rows, kstart, variant, tables = _nbr_geometry(rows)
    bias = _nbr_bias(rpb, tables)
    m = Q_ROWS * GRID_W
    n_local = n_key_rows * GRID_W
    past = ck.shape[2]
    sched = jnp.asarray(np.concatenate([kstart, variant]), I32)
    npr = ATTN_PAIRS
    grid_spec = pltpu.PrefetchScalarGridSpec(
        num_scalar_prefetch=1,
        grid=(n_pairs // npr, batch, nblk),
        in_specs=[
            pl.BlockSpec((npr, m, LANES), lambda hg, b, r, ks: (hg, b * nblk + r, 0)),
            pl.BlockSpec((npr, seq, LANES), lambda hg, b, r, ks: (hg, b, 0)),
            pl.BlockSpec((npr, seq, LANES), lambda hg, b, r, ks: (hg, b, 0)),
            pl.BlockSpec((npr, 1, past, LANES), lambda hg, b, r, ks: (hg, b, 0, 0)),
            pl.BlockSpec((npr, 1, past, LANES), lambda hg, b, r, ks: (hg, b, 0, 0)),
            pl.BlockSpec((bias.shape[0], 2 * npr, m, n_local), lambda hg, b, r, ks: (0, hg, 0, 0)),
        ],
        out_specs=pl.BlockSpec((m, npr * LANES), lambda hg, b, r, ks: (b * nblk + r, hg)),
    )
    return pl.pallas_call(
        functools.partial(_nbr_attn_kernel, n_local=n_local, nblk=nblk),
        name="nbr_attention",
        grid_spec=grid_spec,
        out_shape=jax.ShapeDtypeStruct((n, n_pairs * LANES), BF16),
        compiler_params=_cparams(("arbitrary", "arbitrary", "arbitrary")),
    )(sched, q, k, v, ck, cv, bias)


def _attn_out_kernel(x_ref, o_ref, gt_ref, w_ref, y_ref):
    y_ref[...] = x_ref[...] + gt_ref[0, 0] * _dot(o_ref[...], w_ref[...])


def _attn_out(x2d, o, st, mod, w_bf16):
    n, d = x2d.shape
    tm = QKV_TILE
    tile = pl.BlockSpec((tm, d), lambda i: (i, 0))
    return pl.pallas_call(
        _attn_out_kernel,
        name="attn_out",
        grid=(n // tm,),
        in_specs=[tile, tile, mod.row_spec(0, 2, lambda i: st.cond_row(i, tm)),
                  pl.BlockSpec((d, d), lambda i: (0, 0))],
        out_specs=tile,
        out_shape=jax.ShapeDtypeStruct((n, d), F32),
        compiler_params=_cparams(("arbitrary",)),
    )(x2d, o, mod.t4, w_bf16)


def _lru_in_kernel(y_ref, sh_ref, sc_ref, g_ref, w_ref, gate_ref, xbr_ref, *, d_rnn, latent):
    bsz, tt, d = y_ref.shape
    x = jnp.concatenate([y_ref[:, t, :] for t in range(tt)], axis=0)
    if latent:
        x3 = x.reshape(tt, bsz, d)
        hn = _norm_mod(x3, g_ref[...][None], sc_ref[0][None], sh_ref[0][None]).reshape(tt * bsz, d)
    else:
        hn = _norm_mod(x, g_ref[...], sc_ref[0, 0], sh_ref[0, 0])
    hn = hn.astype(BF16)
    gate_ref[...] = _dot(hn, w_ref[:, :d_rnn])
    xbr_ref[...] = _dot(hn, w_ref[:, d_rnn:])


def _lru_in(y3, st, mod, g, w_bf16):
    bsz, seq, d = y3.shape
    d_rnn = w_bf16.shape[1] // 2
    tt = QKV_TILE // bsz
    assert (not st.latent) or bsz == SUBLANES
    mspec = (lambda c: mod.latent_spec(1, c)) if st.latent else (lambda c: mod.row_spec(1, c, lambda t: 0))
    out_spec = pl.BlockSpec((tt * bsz, d_rnn), lambda t: (t, 0))
    out_shape = jax.ShapeDtypeStruct((seq * bsz, d_rnn), F32)
    return pl.pallas_call(
        functools.partial(_lru_in_kernel, d_rnn=d_rnn, latent=st.latent),
        name="lru_in",
        grid=(seq // tt,),
        in_specs=[
            pl.BlockSpec((bsz, tt, d), lambda t: (0, t, 0)),
            mspec(0), mspec(1),
            pl.BlockSpec((1, d), lambda t: (0, 0)),
            pl.BlockSpec((d, 2 * d_rnn), lambda t: (0, 0)),
        ],
        out_specs=[out_spec, out_spec],
        out_shape=[out_shape, out_shape],
        compiler_params=_cparams(("arbitrary",)),
    )(y3, mod.t3 if st.latent else mod.t4, mod.t3 if st.latent else mod.t4, g, w_bf16)


def _scan_kernel(prev_ref, cur_ref, next_ref, cw_ref, cb_ref, wa_ref, wx_ref, ba_ref, bx_ref, lam_ref,
                 h0_ref, h_ref, hfin_ref, a_scr, b_scr, carry, *, batch, steps, nchunk):
    rev = pl.program_id(0) == 1
    c = pl.program_id(1)
    c_eff = jnp.where(rev, nchunk - 1 - c, c)
    rows = steps * batch
    cb = wa_ref.shape[2]
    for n in range(wa_ref.shape[1]):
        ch = slice(n * cb, (n + 1) * cb)
        prev = jnp.where(c_eff == 0, 0.0, prev_ref[:, ch])
        nxt = jnp.where(c_eff == nchunk - 1, 0.0, next_ref[:, ch])
        xfull = jnp.concatenate([prev, cur_ref[:, ch], nxt], axis=0)
        cw = cw_ref[:, ch]
        xc = cw[0:1, :] * xfull[0:rows]
        for j in range(1, cw.shape[0]):
            xc = xc + cw[j:j + 1, :] * xfull[j * batch:j * batch + rows]
        xc = xc + cb_ref[:, ch]
        xcb = xc.astype(BF16)
        t_r = jnp.tanh(_dot(xcb, wa_ref[0, n]) + ba_ref[0, :, ch])
        t_i = jnp.tanh(_dot(xcb, wx_ref[0, n]) + bx_ref[0, :, ch])
        neg_lam = -lam_ref[0, :, ch]
        softplus = jnp.maximum(neg_lam, 0.0) + jnp.log1p(jnp.exp(-jnp.abs(neg_lam)))
        log_a = ((-0.5 * LRU_C) * softplus) * (t_r + 1.0)
        a = jnp.exp(log_a)
        a_scr[:, ch] = a
        b_scr[:, ch] = jnp.sqrt(jnp.tanh(-log_a) * (1.0 + a * a)) * ((0.5 * xc) * (t_i + 1.0))

    @pl.when(c == 0)
    def _():
        carry[...] = h0_ref[0]

    def step(t, h):
        te = jnp.where(rev, steps - 1 - t, t)
        r0 = pl.multiple_of(te * batch, batch)
        h = a_scr[pl.ds(r0, batch), :] * h + b_scr[pl.ds(r0, batch), :]
        h_ref[0, pl.ds(r0, batch), :] = h
        return h

    h = lax.fori_loop(0, steps, step, carry[...], unroll=min(steps, 8))
    carry[...] = h

    @pl.when(c == nchunk - 1)
    def _():
        hfin_ref[0] = h


def _lru_scan(x2, batch, seq, conv_w, conv_b, wa_bf16, wx_bf16, ba, bx, lam, h0):
    d_rnn = conv_w.shape[1]
    nb, cb = wa_bf16.shape[1], wa_bf16.shape[2]
    assert batch % SUBLANES == 0 and conv_w.shape[0] == 4 and nb * cb == d_rnn
    steps = max(SCAN_ROWS // batch, 2)
    rows = steps * batch
    nchunk = seq // steps
    halo = 2 * batch
    hpc = rows // halo
    n_halo = seq * batch // halo
    ce = lambda d, c: jnp.where(d == 1, nchunk - 1 - c, c)
    vec = lambda a: a.reshape(2, 1, d_rnn)
    vspec = pl.BlockSpec((1, 1, d_rnn), lambda d, c: (d, 0, 0))
    wspec = pl.BlockSpec((1, nb, cb, cb), lambda d, c: (d, 0, 0, 0))
    return pl.pallas_call(
        functools.partial(_scan_kernel, batch=batch, steps=steps, nchunk=nchunk),
        name="lru_scan",
        grid=(2, nchunk),
        in_specs=[
            pl.BlockSpec((halo, d_rnn), lambda d, c: (jnp.maximum(ce(d, c) * hpc - 1, 0), 0)),
            pl.BlockSpec((rows, d_rnn), lambda d, c: (ce(d, c), 0)),
            pl.BlockSpec((halo, d_rnn), lambda d, c: (jnp.minimum((ce(d, c) + 1) * hpc, n_halo - 1), 0)),
            pl.BlockSpec((4, d_rnn), lambda d, c: (0, 0)),
            pl.BlockSpec((1, d_rnn), lambda d, c: (0, 0)),
            wspec, wspec, vspec, vspec, vspec,
            pl.BlockSpec((1, batch, d_rnn), lambda d, c: (d, 0, 0)),
        ],
        out_specs=[
            pl.BlockSpec((1, rows, d_rnn), lambda d, c: (d, ce(d, c), 0)),
            pl.BlockSpec((1, batch, d_rnn), lambda d, c: (d, 0, 0)),
        ],
        out_shape=[
            jax.ShapeDtypeStruct((2, seq * batch, d_rnn), F32),
            jax.ShapeDtypeStruct((2, batch, d_rnn), F32),
        ],
        scratch_shapes=[pltpu.VMEM((rows, d_rnn), F32), pltpu.VMEM((rows, d_rnn), F32),
                        pltpu.VMEM((batch, d_rnn), F32)],
        compiler_params=_cparams(("arbitrary", "arbitrary")),
    )(x2, x2, x2, conv_w, conv_b[None, :], wa_bf16, wx_bf16, vec(ba), vec(bx), vec(lam), h0)


def _lru_out_kernel(y_ref, gate_ref, h_ref, gt_ref, w_ref, o_ref, z_scr, *, latent):
    bsz, tt, d = y_ref.shape
    z = jax.nn.gelu(gate_ref[...]) * (h_ref[0] + h_ref[1])
    z_scr[...] = z.reshape(tt, bsz, z.shape[1])
    zb = jnp.concatenate([z_scr[:, b, :] for b in range(bsz)], axis=0)
    out = _dot(zb.astype(BF16), w_ref[...]).reshape(bsz, tt, d)
    gt = gt_ref[0][:, None, :] if latent else gt_ref[0]
    o_ref[...] = y_ref[...] + gt * out


def _lru_out(y3, gate, h, st, mod, w_bf16):
    bsz, seq, d = y3.shape
    d_rnn = w_bf16.shape[0]
    tt = QKV_TILE // bsz
    gspec = mod.latent_spec(1, 2) if st.latent else mod.row_spec(1, 2, lambda t: 0)
    return pl.pallas_call(
        functools.partial(_lru_out_kernel, latent=st.latent),
        name="lru_out",
        grid=(seq // tt,),
        in_specs=[
            pl.BlockSpec((bsz, tt, d), lambda t: (0, t, 0)),
            pl.BlockSpec((tt * bsz, d_rnn), lambda t: (t, 0)),
            pl.BlockSpec((2, tt * bsz, d_rnn), lambda t: (0, t, 0)),
            gspec,
            pl.BlockSpec((d_rnn, d), lambda t: (0, 0)),
        ],
        out_specs=pl.BlockSpec((bsz, tt, d), lambda t: (0, t, 0)),
        out_shape=jax.ShapeDtypeStruct((bsz, seq, d), F32),
        scratch_shapes=[pltpu.VMEM((tt, bsz, d_rnn), F32)],
        compiler_params=_cparams(("arbitrary",)),
    )(y3, gate, h, mod.t3 if st.latent else mod.t4, w_bf16)


def _dual_specs(shape, sp, ss, extra=0):
    pmap = lambda i, *_: (jnp.minimum(i, sp.tiles - 1), 0)
    smap = lambda i, *_: (jnp.maximum(i - sp.tiles, 0), 0)
    return pl.BlockSpec(shape, pmap), pl.BlockSpec(shape, smap)


def _joint_cond_row(sp, ss):
    return lambda i, *_: jnp.where(i < sp.tiles, 0, ss.cond_row(jnp.maximum(i - sp.tiles, 0)))


def _router_kernel(yp_ref, ys_ref, sh_ref, sc_ref, g_ref, rwt_ref, rb_ref, bkt_ref, rank_ref, wext_ref,
                   cnt_ref, carry, *, n_experts, ntp, n_steps):
    i = pl.program_id(0)

    @pl.when(i == 0)
    def _():
        carry[...] = jnp.zeros_like(carry)

    y = jnp.where(i < ntp, yp_ref[...], ys_ref[...])
    hn = _norm_mod(y, g_ref[...], sc_ref[0, 0], sh_ref[0, 0])
    tm = hn.shape[0]
    logits = lax.dot_general(rwt_ref[...], hn, (((1,), (1,)), ((), ())),
                             precision=lax.Precision.HIGHEST, preferred_element_type=F32)
    scores = jax.nn.sigmoid(logits)
    sel = scores + rb_ref[...]
    per = n_experts // N_GROUPS
    assert per == 4
    srow = [sel[e:e + 1, :] for e in range(n_experts)]
    prow = [scores[e:e + 1, :] for e in range(n_experts)]

    def top2_sum(a, b, c, d):
        hi1, lo1, hi2, lo2 = jnp.maximum(a, b), jnp.minimum(a, b), jnp.maximum(c, d), jnp.minimum(c, d)
        return jnp.maximum(hi1, hi2) + jnp.maximum(jnp.minimum(hi1, hi2), jnp.maximum(lo1, lo2))

    gscore = [top2_sum(*srow[per * g:per * (g + 1)]) for g in range(N_GROUPS)]
    best = jnp.zeros((1, tm), I32)
    best_v = gscore[0]
    for g in range(1, N_GROUPS):
        upd = gscore[g] > best_v
        best = jnp.where(upd, g, best)
        best_v = jnp.where(upd, gscore[g], best_v)
    vs = list(srow[:per])
    ps = list(prow[:per])
    for g in range(1, N_GROUPS):
        for j in range(per):
            vs[j] = jnp.where(best == g, srow[per * g + j], vs[j])
            ps[j] = jnp.where(best == g, prow[per * g + j], ps[j])
    j1 = jnp.zeros((1, tm), I32)
    m1 = vs[0]
    for j in range(1, per):
        upd = vs[j] > m1
        j1 = jnp.where(upd, j, j1)
        m1 = jnp.where(upd, vs[j], m1)
    j2 = jnp.full((1, tm), -1, I32)
    m2 = jnp.full((1, tm), -jnp.inf, F32)
    for j in range(per):
        upd = (j1 != j) & (vs[j] > m2)
        j2 = jnp.where(upd, j, j2)
        m2 = jnp.where(upd, vs[j], m2)
    s1 = jnp.zeros((1, tm), F32)
    s2 = jnp.zeros((1, tm), F32)
    for j in range(per):
        s1 = jnp.where(j1 == j, ps[j], s1)
        s2 = jnp.where(j2 == j, ps[j], s2)
    den = s1 + s2
    w1, w2 = s1 / den, s2 / den
    lo = jnp.minimum(j1, j2)
    hi = jnp.maximum(j1, j2)
    w_lo = jnp.where(j1 < j2, w1, w2)
    w_hi = jnp.where(j1 < j2, w2, w1)
    base = jnp.where(lo == 0, 0, jnp.where(lo == 1, 3, 5))
    bucket = best * 6 + base + hi - lo - 1

    onehot = (lax.broadcasted_iota(I32, (N_BUCKET_SLOTS, tm), 0) == bucket).astype(F32)
    before = (lax.broadcasted_iota(I32, (tm, tm), 0) < lax.broadcasted_iota(I32, (tm, tm), 1)).astype(BF16)
    prefix = _dot(onehot.astype(BF16), before) + carry[...]
    rank = jnp.sum(onehot * prefix, axis=0, keepdims=True)
    carry[...] = carry[...] + jnp.sum(onehot, axis=1, keepdims=True)

    bkt_ref[0] = bucket
    rank_ref[0] = rank.astype(I32)
    slab_row = lax.broadcasted_iota(I32, (LANES, tm), 0)
    slab = jnp.where(slab_row == 0, w_lo, jnp.where(slab_row == 1, w_hi, 0.0))
    wext_ref[...] = slab.T

    @pl.when(i == n_steps - 1)
    def _():
        cnt_ref[...] = jnp.broadcast_to(carry[...], cnt_ref.shape).astype(I32)


def _router(yp, ys, sp, ss, mod, layer, g, router_wt, router_b):
    d = yp.shape[1]
    e = router_wt.shape[0]
    tm = TOKEN_TILE
    nt = sp.tiles + ss.tiles
    row = _joint_cond_row(sp, ss)
    tok_spec = pl.BlockSpec((1, 1, tm), lambda i: (i, 0, 0))
    pspec, sspec = _dual_specs((tm, d), sp, ss)
    return pl.pallas_call(
        functools.partial(_router_kernel, n_experts=e, ntp=sp.tiles, n_steps=nt),
        name="moe_router",
        grid=(nt,),
        in_specs=[
            pspec, sspec,
            mod.row_spec(layer, 3, row),
            mod.row_spec(layer, 4, row),
            pl.BlockSpec((1, d), lambda i: (0, 0)),
            pl.BlockSpec((e, d), lambda i: (0, 0)),
            pl.BlockSpec((e, 1), lambda i: (0, 0)),
        ],
        out_specs=[tok_spec, tok_spec,
                   pl.BlockSpec((tm, LANES), lambda i: (i, 0)),
                   pl.BlockSpec((N_BUCKET_SLOTS, LANES), lambda i: (0, 0))],
        out_shape=[jax.ShapeDtypeStruct((nt, 1, tm), I32), jax.ShapeDtypeStruct((nt, 1, tm), I32),
                   jax.ShapeDtypeStruct((nt * tm, LANES), F32),
                   jax.ShapeDtypeStruct((N_BUCKET_SLOTS, LANES), I32)],
        scratch_shapes=[pltpu.VMEM((N_BUCKET_SLOTS, 1), F32)],
        compiler_params=_cparams(("arbitrary",)),
    )(yp, ys, mod.t4, mod.t4, g, router_wt, router_b[:, None])


def _bucket_layout(counts, n_tokens):
    n_buckets = N_GROUPS * len(PAIR_LO)
    tf = FFN_TILE
    n_tiles = n_tokens // tf + n_buckets + 2
    c = counts[:n_buckets]
    padded = ((c + tf - 1) // tf) * tf
    ends = jnp.cumsum(padded)
    starts = ends - padded
    n_used = ends[-1] // tf
    tile = jnp.arange(n_tiles, dtype=I32)
    valid = tile < n_used
    tile_c = jnp.minimum(tile, n_used - 1)
    tb = jnp.sum((ends[None, :] <= (tile_c * tf)[:, None]).astype(I32), axis=1)
    tb = jnp.minimum(tb, n_buckets - 1)
    grp, pair = tb // len(PAIR_LO), tb % len(PAIR_LO)
    pair_lo = sum(jnp.where(pair == p, PAIR_LO[p], 0) for p in range(len(PAIR_LO)))
    pair_hi = sum(jnp.where(pair == p, PAIR_HI[p], 0) for p in range(len(PAIR_HI)))
    slots = lambda v: jnp.concatenate([v.astype(I32), jnp.zeros((N_BUCKET_SLOTS - n_buckets,), I32)])
    sched = jnp.stack([tile_c, grp * 4 + pair_lo, grp * 4 + pair_hi, valid.astype(I32)]).astype(I32)
    fill = slots((starts + c) // SUBLANES).at[n_buckets].set(n_used.astype(I32))
    return slots(starts), fill, sched, n_tiles


def _pos_kernel(starts_ref, bkt_ref, rank_ref, pos_ref):
    bkt = bkt_ref[...]
    pos = rank_ref[...]
    for b in range(N_GROUPS * len(PAIR_LO)):
        pos = pos + jnp.where(bkt == b, starts_ref[b], 0)
    pos_ref[...] = pos


def _positions(starts32, bkt, rank):
    nt, _, tm = bkt.shape
    grid_spec = pltpu.PrefetchScalarGridSpec(
        num_scalar_prefetch=1,
        grid=(1,),
        in_specs=[pl.BlockSpec((nt, 1, tm), lambda i, st: (0, 0, 0))] * 2,
        out_specs=pl.BlockSpec((nt, 1, tm), lambda i, st: (0, 0, 0)),
    )
    return pl.pallas_call(
        _pos_kernel, name="moe_positions", grid_spec=grid_spec,
        out_shape=jax.ShapeDtypeStruct((nt, 1, tm), I32),
        compiler_params=_cparams(("arbitrary",)),
    )(starts32, bkt, rank)


def _dispatch_kernel(fill_ref, yp_ref, ys_ref, sh_ref, sc_ref, g_ref, wext_ref, pos_ref,
                     xs_ref, buf0, buf1, zeros, sem0, sem1, *, d, ntp, n_steps, n_buckets):
    groups, sub, width = buf0.shape
    i = pl.program_id(0)
    y = jnp.where(i < ntp, yp_ref[...], ys_ref[...])
    hn = _norm_mod(y, g_ref[...], sc_ref[0, 0], sh_ref[0, 0])

    @pl.when(i == 0)
    def _():
        zgroups = zeros.shape[0] - 1
        zeros[...] = jnp.zeros_like(zeros)
        fills = [pltpu.make_async_copy(zeros, xs_ref.at[pl.ds(fill_ref[b], zgroups + 1)], sem1)
                 for b in range(n_buckets)]
        for copy in fills:
            copy.start()
        for copy in fills:
            copy.wait()

        def clear_tile(t, carry):
            copy = pltpu.make_async_copy(zeros.at[pl.ds(0, zgroups)], xs_ref.at[pl.ds(t * zgroups, zgroups)], sem1)
            copy.start()
            copy.wait()
            return carry

        lax.fori_loop(fill_ref[n_buckets], xs_ref.shape[0] // zgroups, clear_tile, 0)

    def wait_tile(buf, sem):
        pltpu.make_async_copy(buf, xs_ref.at[pl.ds(0, groups)], sem).wait()

    def step(buf, sem, prev_buf, prev_sem):
        buf[:, :, :d] = hn.reshape(groups, sub, d)
        buf[:, :, d:] = wext_ref[...].reshape(groups, sub, width - d)

        def issue(g, carry):
            for u in range(sub):
                pos = pos_ref[0, 0, g * sub + u]
                dst = xs_ref.at[lax.shift_right_logical(pos, 3), pl.ds(pos & (sub - 1), 1)]
                pltpu.make_async_copy(buf.at[g, pl.ds(u, 1)], dst, sem).start(priority=u % 2)
            return carry

        lax.fori_loop(0, groups, issue, 0)

        @pl.when(i > 0)
        def _():
            wait_tile(prev_buf, prev_sem)

        @pl.when(i == n_steps - 1)
        def _():
            wait_tile(buf, sem)

    @pl.when(i % 2 == 0)
    def _():
        step(buf0, sem0, buf1, sem1)

    @pl.when(i % 2 == 1)
    def _():
        step(buf1, sem1, buf0, sem0)


def _dispatch(yp, ys, sp, ss, mod, layer, g, wext, pos, fill_groups, n_rows):
    d = yp.shape[1]
    tm = TOKEN_TILE
    width = d + LANES
    groups = tm // SUBLANES
    row = _joint_cond_row(sp, ss)
    pspec, sspec = _dual_specs((tm, d), sp, ss)
    tile_buf = pltpu.VMEM((groups, SUBLANES, width), F32)
    grid_spec = pltpu.PrefetchScalarGridSpec(
        num_scalar_prefetch=1,
        grid=(sp.tiles + ss.tiles,),
        in_specs=[
            pspec, sspec,
            mod.row_spec(layer, 3, row), mod.row_spec(layer, 4, row),
            pl.BlockSpec((1, d), lambda i, f: (0, 0)),
            pl.BlockSpec((tm, LANES), lambda i, f: (i, 0)),
            pl.BlockSpec((1, 1, tm), lambda i, f: (i, 0, 0), memory_space=pltpu.SMEM),
        ],
        out_specs=pl.BlockSpec(memory_space=pl.ANY),
        scratch_shapes=[tile_buf, tile_buf, pltpu.VMEM((FFN_TILE // SUBLANES + 1, SUBLANES, width), F32),
                        pltpu.SemaphoreType.DMA(()), pltpu.SemaphoreType.DMA(())],
    )
    return pl.pallas_call(
        functools.partial(_dispatch_kernel, d=d, ntp=sp.tiles, n_steps=sp.tiles + ss.tiles,
                          n_buckets=N_GROUPS * len(PAIR_LO)),
        name="moe_dispatch",
        grid_spec=grid_spec,
        out_shape=jax.ShapeDtypeStruct((n_rows // SUBLANES, SUBLANES, width), F32),
        compiler_params=_cparams(("arbitrary",), disable_bounds_checks=True),
    )(fill_groups, yp, ys, mod.t4, mod.t4, g, wext, pos).reshape(n_rows, width)


def _ffn_kernel(sched_ref, xs_ref, wg_lo, wu_lo, wd_lo, wg_hi, wu_hi, wd_hi, o_ref, *, d):
    valid = sched_ref[3, pl.program_id(0)] == 1

    @pl.when(jnp.logical_not(valid))
    def _():
        o_ref[...] = jnp.zeros_like(o_ref)

    @pl.when(valid)
    def _():
        x = xs_ref[:, :d].astype(BF16)
        gates = [_dot(x, wg[0, 0]) for wg in (wg_lo, wg_hi)]
        ups = [_dot(x, wu[0, 0]) for wu in (wu_lo, wu_hi)]
        hes = [((g * jax.nn.sigmoid(g)) * u).astype(BF16) for g, u in zip(gates, ups)]
        w_lo = xs_ref[:, d:d + 1]
        w_hi = xs_ref[:, d + 1:d + 2]
        o_ref[...] = w_lo * _dot(hes[0], wd_lo[0, 0]) + w_hi * _dot(hes[1], wd_hi[0, 0])


def _expert_ffn(xs, sched, n_tiles, layer, wg, wu, wd):
    n_rows, width = xs.shape
    d = width - LANES
    f = wg.shape[3]
    tf = FFN_TILE
    lo = lambda i, s: (layer, s[1, i], 0, 0)
    hi = lambda i, s: (layer, s[2, i], 0, 0)
    grid_spec = pltpu.PrefetchScalarGridSpec(
        num_scalar_prefetch=1,
        grid=(n_tiles,),
        in_specs=[
            pl.BlockSpec((tf, width), lambda i, s: (s[0, i], 0)),
            pl.BlockSpec((1, 1, d, f), lo), pl.BlockSpec((1, 1, d, f), lo), pl.BlockSpec((1, 1, f, d), lo),
            pl.BlockSpec((1, 1, d, f), hi), pl.BlockSpec((1, 1, d, f), hi), pl.BlockSpec((1, 1, f, d), hi),
        ],
        out_specs=pl.BlockSpec((tf, d), lambda i, s: (i, 0)),
    )
    return pl.pallas_call(
        functools.partial(_ffn_kernel, d=d),
        name="moe_expert_ffn",
        grid_spec=grid_spec,
        out_shape=jax.ShapeDtypeStruct((n_rows, d), F32),
        compiler_params=_cparams(("arbitrary",)),
    )(sched, xs, wg, wu, wd, wg, wu, wd)


def _combine_kernel(y_ref, gt_ref, pos_ref, pos_next_ref, ms_ref, ms_groups_ref, o_ref, buf0, buf1, sem0, sem1, *,
                    n_steps):
    groups, sub, d = buf0.shape
    tm = groups * sub
    i = pl.program_id(0)

    def gather(pos, buf, sem):
        def issue(g, carry):
            for u in range(sub):
                pltpu.make_async_copy(ms_ref.at[pl.ds(pos[0, 0, g * sub + u], 1)], buf.at[g, pl.ds(u, 1)],
                                      sem).start(priority=u % 2)
            return carry

        lax.fori_loop(0, groups, issue, 0)

    def step(buf, sem, next_buf, next_sem):
        @pl.when(i == 0)
        def _():
            gather(pos_ref, buf, sem)

        @pl.when(i + 1 < n_steps)
        def _():
            gather(pos_next_ref, next_buf, next_sem)

        pltpu.make_async_copy(ms_groups_ref.at[pl.ds(0, groups)], buf, sem).wait()
        o_ref[...] = y_ref[...] + gt_ref[0, 0] * buf[...].reshape(tm, d)

    @pl.when(i % 2 == 0)
    def _():
        step(buf0, sem0, buf1, sem1)

    @pl.when(i % 2 == 1)
    def _():
        step(buf1, sem1, buf0, sem0)


def _combine(y2d, st, mod, layer, pos, ms):
    n, d = y2d.shape
    tm = TOKEN_TILE
    tile = pl.BlockSpec((tm, d), lambda i: (i, 0))
    last = st.tile_off + n // tm - 1
    return pl.pallas_call(
        functools.partial(_combine_kernel, n_steps=n // tm),
        name="moe_combine",
        grid=(n // tm,),
        in_specs=[
            tile,
            mod.row_spec(layer, 5, st.cond_row),
            pl.BlockSpec((1, 1, tm), lambda i: (st.tile_off + i, 0, 0), memory_space=pltpu.SMEM),
            pl.BlockSpec((1, 1, tm), lambda i: (jnp.minimum(st.tile_off + i + 1, last), 0, 0),
                         memory_space=pltpu.SMEM),
            pl.BlockSpec(memory_space=pl.ANY),
            pl.BlockSpec(memory_space=pl.ANY),
        ],
        out_specs=tile,
        out_shape=jax.ShapeDtypeStruct((n, d), F32),
        scratch_shapes=[pltpu.VMEM((tm // SUBLANES, SUBLANES, d), F32),
                        pltpu.VMEM((tm // SUBLANES, SUBLANES, d), F32),
                        pltpu.SemaphoreType.DMA(()), pltpu.SemaphoreType.DMA(())],
        compiler_params=_cparams(("arbitrary",), disable_bounds_checks=True),
    )(y2d, mod.t4, pos, pos, ms, ms.reshape(ms.shape[0] // SUBLANES, SUBLANES, d))


def _moe_residual(yp, ys, sp, ss, mod, layer, g, router_wt, router_b, wg, wu, wd):
    bkt, rank, wext, counts = _router(yp, ys, sp, ss, mod, layer, g, router_wt, router_b)
    starts32, fill_groups, sched, n_tiles = _bucket_layout(counts[:, 0], sp.n + ss.n)
    pos = _positions(starts32, bkt, rank)
    xs = _dispatch(yp, ys, sp, ss, mod, layer, g, wext, pos, fill_groups, n_tiles * FFN_TILE)
    ms = _expert_ffn(xs, sched, n_tiles, layer, wg, wu, wd)
    return _combine(yp, sp, mod, layer, pos, ms), _combine(ys, ss, mod, layer, pos, ms)


def kernel(x_prompt, x_sample, c, cache_k, cache_v, state_lru, c_ctx, ada_w, ada_b, norm_mix, norm_ffn,
           attn_w_in, attn_q_gain, attn_k_gain, attn_rpb, attn_w_out, lru_w_in, lru_conv_w, lru_conv_b,
           lru_gate_a_w, lru_gate_a_b, lru_gate_x_w, lru_gate_x_b, lru_lambda, lru_w_out, router_w,
           router_b, moe_w_gate, moe_w_up, moe_w_down):
    bp, tp, d = x_prompt.shape
    bs, ts, _ = x_sample.shape
    n_heads, head_dim = cache_k.shape[3], cache_k.shape[4]
    d_rnn = lru_conv_w.shape[2]
    assert ada_w.shape[0] == 2 and bs <= N_COND_ROWS - LATENT_COND_ROW
    assert tp % TOKEN_TILE == 0 and ts % TOKEN_TILE == 0
    sp = _Stream(bp, tp, 0, latent=False)
    ss = _Stream(bs, ts, sp.tiles, latent=True)

    cond = jnp.zeros((N_COND_ROWS, d), F32).at[0].set(c_ctx).at[LATENT_COND_ROW:LATENT_COND_ROW + bs].set(c)
    mod = _Mod(_ada_modulation(cond, ada_w, ada_b), d)

    w_in = attn_w_in[0].astype(BF16)
    w_out = attn_w_out[0].astype(BF16)
    g_mix0 = norm_mix[0][None, :]
    xp2, xs2 = x_prompt.reshape(sp.n, d), x_sample.reshape(ss.n, d)
    q_p, k_p, v_p, new_k, new_v = _qkv_proj(xp2, sp, mod, g_mix0, w_in, attn_q_gain[0], attn_k_gain[0],
                                            n_heads, True)
    q_s, k_s, v_s = _qkv_proj(xs2, ss, mod, g_mix0, w_in, attn_q_gain[0], attn_k_gain[0], n_heads, False)
    o_p = _ctx_attention(q_p, k_p, v_p, bp, tp)
    pair_major = lambda cache: jnp.transpose(
        cache[:, 0].astype(BF16).reshape(bs, cache.shape[2], n_heads // 2, LANES), (2, 0, 1, 3))
    ck, cv = pair_major(cache_k), pair_major(cache_v)
    o_s = _nbr_attention(q_s, k_s, v_s, ck, cv, attn_rpb[0], bs, ts)
    yp = _attn_out(xp2, o_p, sp, mod, w_out)
    ys = _attn_out(xs2, o_s, ss, mod, w_out)

    router_wt = router_w.T
    moe_w = (moe_w_gate.astype(BF16), moe_w_up.astype(BF16), moe_w_down.astype(BF16))
    yp, ys = _moe_residual(yp, ys, sp, ss, mod, 0, norm_ffn[0][None, :], router_wt, router_b, *moe_w)

    w_lru_in = lru_w_in[0].astype(BF16)
    w_lru_out = lru_w_out[0].astype(BF16)
    g_mix1 = norm_mix[1][None, :]
    scan_args = (lru_conv_w[0], lru_conv_b[0], (0.5 * lru_gate_a_w[0]).astype(BF16),
                 (0.5 * lru_gate_x_w[0]).astype(BF16), 0.5 * lru_gate_a_b[0], 0.5 * lru_gate_x_b[0],
                 lru_lambda[0])
    hfin = {}
    ys_out = []
    for y2, st, h0 in ((yp, sp, jnp.zeros((2, bp, d_rnn), F32)),
                       (ys, ss, jnp.transpose(state_lru[:, 0], (1, 0, 2)))):
        y3 = y2.reshape(st.batch, st.seq, d)
        gate, xbr = _lru_in(y3, st, mod, g_mix1, w_lru_in)
        h, hfin[st.latent] = _lru_scan(xbr, st.batch, st.seq, *scan_args, h0)
        ys_out.append(_lru_out(y3, gate, h, st, mod, w_lru_out).reshape(st.n, d))
    yp, ys = ys_out
    yp, ys = _moe_residual(yp, ys, sp, ss, mod, 1, norm_ffn[1][None, :], router_wt, router_b, *moe_w)

    return (yp.reshape(bp, tp, d), ys.reshape(bs, ts, d),
            new_k.reshape(bp, 1, tp, n_heads, head_dim), new_v.reshape(bp, 1, tp, n_heads, head_dim),
            jnp.transpose(hfin[False], (1, 0, 2))[:, None])
```
